```python
import jax, jax.numpy as jnp
from jax import lax
import numpy as np

D_MODEL = 1024
BATCH = 8
SEQ = 2048
DEPTH = 1
DEC_BATCH = 128
DEC_SEQ = 1
PAST_LEN = 16384
PAGE_SIZE = 128

D_CONV = D_MODEL
D_POOL = D_MODEL // 2
N_POOL_GROUPS = 4
POOL_GROUP = D_POOL // N_POOL_GROUPS
POOL_OUT_GROUP = D_MODEL // N_POOL_GROUPS
POOL_WINDOWS = (2, 4, 8, 16)
POOL_STATE = max(POOL_WINDOWS) - 1
CONV_WIDTH = 3
D_FF = ((8 * D_MODEL // 3 + 127) // 128) * 128
ALPHA = (2.0 * DEPTH) ** 0.25
BETA = (8.0 * DEPTH) ** -0.25
LN_EPS = 1e-5
SPLITS = [D_CONV, 2 * D_CONV, 3 * D_CONV, 3 * D_CONV + D_POOL, 3 * D_CONV + D_POOL + D_MODEL]
D_IN = 3 * D_CONV + D_POOL + 2 * D_MODEL

kernel_name = "gated_conv_pool_macaron_deepnorm_step"


def layer_norm(x, g, b):
    xf = x.astype(jnp.float32)
    mu = jnp.mean(xf, axis=-1, keepdims=True)
    var = jnp.mean(jnp.square(xf - mu), axis=-1, keepdims=True)
    y = (xf - mu) * lax.rsqrt(var + LN_EPS)
    return (y * g.astype(jnp.float32) + b.astype(jnp.float32)).astype(x.dtype)


def swiglu(x, wg, wu, wd):
    hid = jax.nn.silu(jnp.einsum('btd,df->btf', x, wg)) * jnp.einsum('btd,df->btf', x, wu)
    return jnp.einsum('btf,fd->btd', hid, wd)


def short_conv(u_ext, conv_w, t_new):
    out = conv_w[0] * u_ext[:, 0:t_new]
    for k in range(1, CONV_WIDTH):
        out = out + conv_w[k] * u_ext[:, k:k + t_new]
    return out


def multiscale_pool(p_ext, pos0, t_new):
    pf = p_ext.astype(jnp.float32)
    bsz = pf.shape[0]
    cs = jnp.concatenate([jnp.zeros((bsz, 1, D_POOL), jnp.float32), jnp.cumsum(pf, axis=1)], axis=1)
    pos = pos0 + jnp.arange(t_new)
    outs = []
    for g, w in enumerate(POOL_WINDOWS):
        sl = slice(g * POOL_GROUP, (g + 1) * POOL_GROUP)
        hi = cs[:, POOL_STATE + 1:POOL_STATE + 1 + t_new, sl]
        lo = cs[:, POOL_STATE + 1 - w:POOL_STATE + 1 - w + t_new, sl]
        cnt = jnp.minimum(pos + 1, w).astype(jnp.float32)[None, :, None]
        outs.append((hi - lo) / cnt)
    mean = jnp.concatenate(outs, axis=-1)
    return (mean - pf[:, POOL_STATE:]).astype(p_ext.dtype)


def hybrid_layer(x, conv_state, pool_state, pos0, ln1_g, ln1_b, ffn1_wg, ffn1_wu, ffn1_wd,
                 w_in, conv_w, pool_w, pool_scale, w_out, ln2_g, ln2_b,
                 ffn2_wg, ffn2_wu, ffn2_wd, ln3_g, ln3_b):
    bsz, t_new, _ = x.shape
    x = layer_norm(ALPHA * x + 0.5 * swiglu(x, ffn1_wg, ffn1_wu, ffn1_wd), ln1_g, ln1_b)
    z = jnp.einsum('btd,de->bte', x, w_in)
    c_g, b_g, h, p, g_a, g_p = jnp.split(z, SPLITS, axis=-1)
    u = c_g * h
    u_ext = jnp.concatenate([conv_state.astype(u.dtype), u], axis=1)
    y_a = b_g * short_conv(u_ext, conv_w, t_new)
    p_ext = jnp.concatenate([pool_state.astype(p.dtype), p], axis=1)
    q = multiscale_pool(p_ext, pos0, t_new)
    y_p = jnp.einsum('btgc,gce->btge', q.reshape(bsz, t_new, N_POOL_GROUPS, POOL_GROUP), pool_w)
    y_p = y_p.reshape(bsz, t_new, D_MODEL) * pool_scale
    m = jax.nn.sigmoid(g_a) * y_a + jax.nn.sigmoid(g_p) * y_p
    x = layer_norm(ALPHA * x + jnp.einsum('btd,de->bte', m, w_out), ln2_g, ln2_b)
    x = layer_norm(ALPHA * x + 0.5 * swiglu(x, ffn2_wg, ffn2_wu, ffn2_wd), ln3_g, ln3_b)
    return x, u_ext[:, -(CONV_WIDTH - 1):], p_ext[:, -POOL_STATE:]


def setup_inputs(seed: int = 0) -> dict:
    key = jax.random.key(seed)
    ks = jax.random.split(key, 24)
    f32 = jnp.float32
    nrm = lambda k, shape, s: jax.random.normal(k, shape, f32) * s
    L = DEPTH
    return {
        "x_prompt": nrm(ks[0], (BATCH, SEQ, D_MODEL), 1.0),
        "x_sample": nrm(ks[1], (DEC_BATCH, DEC_SEQ, D_MODEL), 1.0),
        "state_conv": nrm(ks[2], (L, DEC_BATCH, CONV_WIDTH - 1, D_CONV), 1.0),
        "state_pool": nrm(ks[3], (L, DEC_BATCH, POOL_STATE, D_POOL), 1.0),
        "ln1_g": 1.0 + nrm(ks[4], (L, D_MODEL), 0.02),
        "ln1_b": nrm(ks[5], (L, D_MODEL), 0.02),
        "ffn1_wg": nrm(ks[6], (L, D_MODEL, D_FF), D_MODEL ** -0.5),
        "ffn1_wu": nrm(ks[7], (L, D_MODEL, D_FF), D_MODEL ** -0.5),
        "ffn1_wd": nrm(ks[8], (L, D_FF, D_MODEL), BETA * D_FF ** -0.5),
        "w_in": nrm(ks[9], (L, D_MODEL, D_IN), D_MODEL ** -0.5),
        "conv_w": nrm(ks[10], (L, CONV_WIDTH, D_CONV), CONV_WIDTH ** -0.5),
        "pool_w": nrm(ks[11], (L, N_POOL_GROUPS, POOL_GROUP, POOL_OUT_GROUP), POOL_GROUP ** -0.5),
        "pool_scale": 1.0 + nrm(ks[12], (L, D_MODEL), 0.02),
        "w_out": nrm(ks[13], (L, D_MODEL, D_MODEL), BETA * D_MODEL ** -0.5),
        "ln2_g": 1.0 + nrm(ks[14], (L, D_MODEL), 0.02),
        "ln2_b": nrm(ks[15], (L, D_MODEL), 0.02),
        "ffn2_wg": nrm(ks[16], (L, D_MODEL, D_FF), D_MODEL ** -0.5),
        "ffn2_wu": nrm(ks[17], (L, D_MODEL, D_FF), D_MODEL ** -0.5),
        "ffn2_wd": nrm(ks[18], (L, D_FF, D_MODEL), BETA * D_FF ** -0.5),
        "ln3_g": 1.0 + nrm(ks[19], (L, D_MODEL), 0.02),
        "ln3_b": nrm(ks[20], (L, D_MODEL), 0.02),
    }


def reference(x_prompt, x_sample, state_conv, state_pool, ln1_g, ln1_b, ffn1_wg, ffn1_wu, ffn1_wd,
              w_in, conv_w, pool_w, pool_scale, w_out, ln2_g, ln2_b,
              ffn2_wg, ffn2_wu, ffn2_wd, ln3_g, ln3_b):
    bp = x_prompt.shape[0]
    yp, ys = x_prompt, x_sample
    conv_p, pool_p, conv_s, pool_s = [], [], [], []
    for l in range(DEPTH):
        w = (ln1_g[l], ln1_b[l], ffn1_wg[l], ffn1_wu[l], ffn1_wd[l], w_in[l], conv_w[l], pool_w[l],
             pool_scale[l], w_out[l], ln2_g[l], ln2_b[l], ffn2_wg[l], ffn2_wu[l], ffn2_wd[l],
             ln3_g[l], ln3_b[l])
        zc = jnp.zeros((bp, CONV_WIDTH - 1, D_CONV), yp.dtype)
        zp = jnp.zeros((bp, POOL_STATE, D_POOL), yp.dtype)
        yp, cp, pp = hybrid_layer(yp, zc, zp, 0, *w)
        ys, cs_, ps_ = hybrid_layer(ys, state_conv[l], state_pool[l], PAST_LEN, *w)
        conv_p.append(cp); pool_p.append(pp); conv_s.append(cs_); pool_s.append(ps_)
    return (yp, ys, jnp.stack(conv_p), jnp.stack(pool_p), jnp.stack(conv_s), jnp.stack(pool_s))
```

```python
import functools

import jax
import jax.numpy as jnp
from jax import lax
from jax.experimental import pallas as pl
from jax.experimental.pallas import tpu as pltpu

D_MODEL = 1024
D_CONV = D_MODEL
D_POOL = D_MODEL // 2
N_POOL_GROUPS = 4
POOL_GROUP = D_POOL // N_POOL_GROUPS
POOL_OUT_GROUP = D_MODEL // N_POOL_GROUPS
POOL_WINDOWS = (2, 4, 8, 16)
POOL_STATE = max(POOL_WINDOWS) - 1
CONV_WIDTH = 3
D_FF = 2816
DEPTH = 1
ALPHA = (2.0 * DEPTH) ** 0.25
LN_EPS = 1e-5
D_IN = 3 * D_CONV + D_POOL + 2 * D_MODEL

OFF_C = 0
OFF_B = D_CONV
OFF_H = 2 * D_CONV
OFF_P = 3 * D_CONV
OFF_GA = 3 * D_CONV + D_POOL
OFF_GP = 3 * D_CONV + D_POOL + D_MODEL

FF_CHUNK = 256
POOL_HALO = 16
CONV_HALO = 8
VMEM_LIMIT_BYTES = 52 * 1024 * 1024


def _layer_norm(v, g, b):
    mu = jnp.mean(v, axis=-1, keepdims=True)
    d = v - mu
    var = jnp.mean(d * d, axis=-1, keepdims=True)
    return d * lax.rsqrt(var + LN_EPS) * g + b


def _const_spec(shape):
    zeros = (0,) * len(shape)
    return pl.BlockSpec(shape, lambda *_: zeros, pipeline_mode=pl.Buffered(1))


def _ffn_kernel(x_ref, wg_ref, wu_ref, wd_ref, g_ref, b_ref, o_ref, h_ref):
    x = x_ref[...]
    xb = x.astype(jnp.bfloat16)
    for c in range(D_FF // FF_CHUNK):
        sl = slice(c * FF_CHUNK, (c + 1) * FF_CHUNK)
        gate = jnp.dot(xb, wg_ref[:, sl], preferred_element_type=jnp.float32)
        up = jnp.dot(xb, wu_ref[:, sl], preferred_element_type=jnp.float32)
        h_ref[:, sl] = (gate * jax.nn.sigmoid(gate) * up).astype(jnp.bfloat16)
    y = jnp.dot(h_ref[...], wd_ref[...], preferred_element_type=jnp.float32)
    o_ref[...] = _layer_norm(ALPHA * x + 0.5 * y, g_ref[...], b_ref[...])


def _ffn(x2d, wg, wu, wd, g, b, tm):
    n = x2d.shape[0]
    return pl.pallas_call(
        _ffn_kernel,
        out_shape=jax.ShapeDtypeStruct((n, D_MODEL), jnp.float32),
        grid=(n // tm,),
        in_specs=[
            pl.BlockSpec((tm, D_MODEL), lambda i: (i, 0)),
            _const_spec((D_MODEL, D_FF)),
            _const_spec((D_MODEL, D_FF)),
            _const_spec((D_FF, D_MODEL)),
            _const_spec((1, D_MODEL)),
            _const_spec((1, D_MODEL)),
        ],
        out_specs=pl.BlockSpec((tm, D_MODEL), lambda i: (i, 0)),
        scratch_shapes=[pltpu.VMEM((tm, D_FF), jnp.bfloat16)],
        compiler_params=pltpu.CompilerParams(
            dimension_semantics=("arbitrary",), vmem_limit_bytes=VMEM_LIMIT_BYTES),
        name="ffn",
    )(x2d, wg, wu, wd, g, b)


def _proj(xb, w_in_ref, off, width):
    return jnp.dot(xb, w_in_ref[:, off:off + width], preferred_element_type=jnp.float32)


def _mixer_prompt_kernel(x_ref, w_in_ref, conv_w_ref, pool_w_ref, pool_scale_ref, w_out_ref,
                         g_ref, b_ref, o_ref, conv_out_ref, pool_out_ref,
                         u_buf, p_buf, m_buf, *, tm):
    t = pl.program_id(1)

    @pl.when(t == 0)
    def _():
        u_buf[0:CONV_HALO, :] = jnp.zeros((CONV_HALO, D_CONV), jnp.float32)
        p_buf[0:POOL_HALO, :] = jnp.zeros((POOL_HALO, D_POOL), jnp.float32)

    x = x_ref[0]
    xb = x.astype(jnp.bfloat16)

    p = _proj(xb, w_in_ref, OFF_P, D_POOL)
    p_buf[POOL_HALO:POOL_HALO + tm, :] = p

    pos = t * tm + lax.broadcasted_iota(jnp.int32, (tm, 1), 0)

    for j in range(N_POOL_GROUPS):
        csl = slice(j * POOL_OUT_GROUP, (j + 1) * POOL_OUT_GROUP)
        c_g = _proj(xb, w_in_ref, OFF_C + j * POOL_OUT_GROUP, POOL_OUT_GROUP)
        h = _proj(xb, w_in_ref, OFF_H + j * POOL_OUT_GROUP, POOL_OUT_GROUP)
        u = c_g * h
        u_buf[CONV_HALO:CONV_HALO + tm, csl] = u
        cw = conv_w_ref[:, csl]
        conv = (cw[0:1] * u_buf[CONV_HALO - 2:CONV_HALO - 2 + tm, csl]
                + cw[1:2] * u_buf[CONV_HALO - 1:CONV_HALO - 1 + tm, csl]
                + cw[2:3] * u)
        b_g = _proj(xb, w_in_ref, OFF_B + j * POOL_OUT_GROUP, POOL_OUT_GROUP)
        y_a = b_g * conv

        psl = slice(j * POOL_GROUP, (j + 1) * POOL_GROUP)
        w = POOL_WINDOWS[j]
        pj = p[:, psl]
        wsum = pj
        for k in range(1, w):
            wsum = wsum + p_buf[POOL_HALO - k:POOL_HALO - k + tm, psl]
        cnt = jnp.minimum(pos + 1, w).astype(jnp.float32)
        q = wsum / cnt - pj
        y_p = jnp.dot(q.astype(jnp.bfloat16), pool_w_ref[j],
                      preferred_element_type=jnp.float32) * pool_scale_ref[:, csl]

        g_a = _proj(xb, w_in_ref, OFF_GA + j * POOL_OUT_GROUP, POOL_OUT_GROUP)
        g_p = _proj(xb, w_in_ref, OFF_GP + j * POOL_OUT_GROUP, POOL_OUT_GROUP)
        m = jax.nn.sigmoid(g_a) * y_a + jax.nn.sigmoid(g_p) * y_p
        m_buf[:, csl] = m.astype(jnp.bfloat16)

    out = jnp.dot(m_buf[...], w_out_ref[...], preferred_element_type=jnp.float32)
    o_ref[0] = _layer_norm(ALPHA * x + out, g_ref[...], b_ref[...])

    conv_tail = u_buf[CONV_HALO + tm - (CONV_WIDTH - 1):CONV_HALO + tm, :]
    pool_tail = p_buf[POOL_HALO + tm - POOL_STATE:POOL_HALO + tm, :]
    u_buf[CONV_HALO - (CONV_WIDTH - 1):CONV_HALO, :] = conv_tail
    p_buf[POOL_HALO - POOL_STATE:POOL_HALO, :] = pool_tail
    conv_out_ref[0] = conv_tail
    pool_out_ref[0] = pool_tail


def _mixer_prompt(x, w_in, conv_w, pool_w, pool_scale, w_out, g, b, tm):
    bsz, seq, _ = x.shape
    kern = functools.partial(_mixer_prompt_kernel, tm=tm)
    return pl.pallas_call(
        kern,
        out_shape=(
            jax.ShapeDtypeStruct((bsz, seq, D_MODEL), jnp.float32),
            jax.ShapeDtypeStruct((bsz, CONV_WIDTH - 1, D_CONV), jnp.float32),
            jax.ShapeDtypeStruct((bsz, POOL_STATE, D_POOL), jnp.float32),
        ),
        grid=(bsz, seq // tm),
        in_specs=[
            pl.BlockSpec((1, tm, D_MODEL), lambda s, t: (s, t, 0)),
            _const_spec((D_MODEL, D_IN)),
            _const_spec((CONV_WIDTH, D_CONV)),
            _const_spec((N_POOL_GROUPS, POOL_GROUP, POOL_OUT_GROUP)),
            _const_spec((1, D_MODEL)),
            _const_spec((D_MODEL, D_MODEL)),
            _const_spec((1, D_MODEL)),
            _const_spec((1, D_MODEL)),
        ],
        out_specs=(
            pl.BlockSpec((1, tm, D_MODEL), lambda s, t: (s, t, 0)),
            pl.BlockSpec((1, CONV_WIDTH - 1, D_CONV), lambda s, t: (s, 0, 0)),
            pl.BlockSpec((1, POOL_STATE, D_POOL), lambda s, t: (s, 0, 0)),
        ),
        scratch_shapes=[
            pltpu.VMEM((CONV_HALO + tm, D_CONV), jnp.float32),
            pltpu.VMEM((POOL_HALO + tm, D_POOL), jnp.float32),
            pltpu.VMEM((tm, D_MODEL), jnp.bfloat16),
        ],
        compiler_params=pltpu.CompilerParams(
            dimension_semantics=("arbitrary", "arbitrary"), vmem_limit_bytes=VMEM_LIMIT_BYTES),
        name="mixer_prompt",
    )(x, w_in, conv_w, pool_w, pool_scale, w_out, g, b)


def _mixer_sample_kernel(x_ref, cs_ref, ps_ref, w_in_ref, conv_w_ref, pool_w_ref, pool_scale_ref,
                         w_out_ref, g_ref, b_ref, o_ref, conv_out_ref, pool_out_ref, m_buf):
    x = x_ref[...]
    xb = x.astype(jnp.bfloat16)
    p = _proj(xb, w_in_ref, OFF_P, D_POOL)

    for j in range(N_POOL_GROUPS):
        csl = slice(j * POOL_OUT_GROUP, (j + 1) * POOL_OUT_GROUP)
        c_g = _proj(xb, w_in_ref, OFF_C + j * POOL_OUT_GROUP, POOL_OUT_GROUP)
        h = _proj(xb, w_in_ref, OFF_H + j * POOL_OUT_GROUP, POOL_OUT_GROUP)
        u = c_g * h
        conv_out_ref[:, D_CONV + j * POOL_OUT_GROUP:D_CONV + (j + 1) * POOL_OUT_GROUP] = u
        cw = conv_w_ref[:, csl]
        u_prev2 = cs_ref[:, j * POOL_OUT_GROUP:(j + 1) * POOL_OUT_GROUP]
        u_prev1 = cs_ref[:, D_CONV + j * POOL_OUT_GROUP:D_CONV + (j + 1) * POOL_OUT_GROUP]
        conv = cw[0:1] * u_prev2 + cw[1:2] * u_prev1 + cw[2:3] * u
        b_g = _proj(xb, w_in_ref, OFF_B + j * POOL_OUT_GROUP, POOL_OUT_GROUP)
        y_a = b_g * conv

        w = POOL_WINDOWS[j]
        pj = p[:, j * POOL_GROUP:(j + 1) * POOL_GROUP]
        wsum = pj
        for k in range(1, w):
            row = POOL_STATE - k
            wsum = wsum + ps_ref[:, row * D_POOL + j * POOL_GROUP:row * D_POOL + (j + 1) * POOL_GROUP]
        q = wsum / float(w) - pj
        y_p = jnp.dot(q.astype(jnp.bfloat16), pool_w_ref[j],
                      preferred_element_type=jnp.float32) * pool_scale_ref[:, csl]

        g_a = _proj(xb, w_in_ref, OFF_GA + j * POOL_OUT_GROUP, POOL_OUT_GROUP)
        g_p = _proj(xb, w_in_ref, OFF_GP + j * POOL_OUT_GROUP, POOL_OUT_GROUP)
        m = jax.nn.sigmoid(g_a) * y_a + jax.nn.sigmoid(g_p) * y_p
        m_buf[:, csl] = m.astype(jnp.bfloat16)

    out = jnp.dot(m_buf[...], w_out_ref[...], preferred_element_type=jnp.float32)
    o_ref[...] = _layer_norm(ALPHA * x + out, g_ref[...], b_ref[...])

    conv_out_ref[:, 0:D_CONV] = cs_ref[:, D_CONV:2 * D_CONV]
    pool_out_ref[:, 0:(POOL_STATE - 1) * D_POOL] = ps_ref[:, D_POOL:POOL_STATE * D_POOL]
    pool_out_ref[:, (POOL_STATE - 1) * D_POOL:POOL_STATE * D_POOL] = p


def _mixer_sample(x2d, cs2d, ps2d, w_in, conv_w, pool_w, pool_scale, w_out, g, b):
    nb = x2d.shape[0]
    shapes = [x2d.shape, cs2d.shape, ps2d.shape, w_in.shape, conv_w.shape, pool_w.shape,
              pool_scale.shape, w_out.shape, g.shape, b.shape]
    return pl.pallas_call(
        _mixer_sample_kernel,
        out_shape=(
            jax.ShapeDtypeStruct((nb, D_MODEL), jnp.float32),
            jax.ShapeDtypeStruct(cs2d.shape, jnp.float32),
            jax.ShapeDtypeStruct(ps2d.shape, jnp.float32),
        ),
        grid=(1,),
        in_specs=[_const_spec(s) for s in shapes],
        out_specs=(
            pl.BlockSpec((nb, D_MODEL), lambda i: (0, 0)),
            pl.BlockSpec(cs2d.shape, lambda i: (0, 0)),
            pl.BlockSpec(ps2d.shape, lambda i: (0, 0)),
        ),
        scratch_shapes=[pltpu.VMEM((nb, D_MODEL), jnp.bfloat16)],
        compiler_params=pltpu.CompilerParams(
            dimension_semantics=("arbitrary",), vmem_limit_bytes=VMEM_LIMIT_BYTES),
        name="mixer_sample",
    )(x2d, cs2d, ps2d, w_in, conv_w, pool_w, pool_scale, w_out, g, b)


def kernel(x_prompt, x_sample, state_conv, state_pool, ln1_g, ln1_b, ffn1_wg, ffn1_wu, ffn1_wd,
           w_in, conv_w, pool_w, pool_scale, w_out, ln2_g, ln2_b,
           ffn2_wg, ffn2_wu, ffn2_wd, ln3_g, ln3_b):
    bsz, seq, _ = x_prompt.shape
    nb = x_sample.shape[0]
    assert x_sample.shape[1] == 1 and DEPTH == 1
    bf = jnp.bfloat16
    tm = 256

    yp = x_prompt.reshape(bsz * seq, D_MODEL)
    ys = x_sample.reshape(nb, D_MODEL)
    l = 0
    wg1, wu1, wd1 = ffn1_wg[l].astype(bf), ffn1_wu[l].astype(bf), ffn1_wd[l].astype(bf)
    wg2, wu2, wd2 = ffn2_wg[l].astype(bf), ffn2_wu[l].astype(bf), ffn2_wd[l].astype(bf)
    w_in_b, w_out_b, pool_w_b = w_in[l].astype(bf), w_out[l].astype(bf), pool_w[l].astype(bf)
    g1, b1 = ln1_g[l][None], ln1_b[l][None]
    g2, b2 = ln2_g[l][None], ln2_b[l][None]
    g3, b3 = ln3_g[l][None], ln3_b[l][None]
    scale = pool_scale[l][None]

    yp = _ffn(yp, wg1, wu1, wd1, g1, b1, tm)
    yp, conv_p, pool_p = _mixer_prompt(yp.reshape(bsz, seq, D_MODEL), w_in_b, conv_w[l], pool_w_b,
                                       scale, w_out_b, g2, b2, tm)
    yp = _ffn(yp.reshape(bsz * seq, D_MODEL), wg2, wu2, wd2, g3, b3, tm)

    cs2d = state_conv[l].reshape(nb, (CONV_WIDTH - 1) * D_CONV)
    ps2d = state_pool[l].reshape(nb, POOL_STATE * D_POOL)
    ys = _ffn(ys, wg1, wu1, wd1, g1, b1, nb)
    ys, conv_s, pool_s = _mixer_sample(ys, cs2d, ps2d, w_in_b, conv_w[l], pool_w_b, scale,
                                       w_out_b, g2, b2)
    ys = _ffn(ys, wg2, wu2, wd2, g3, b3, nb)

    return (yp.reshape(bsz, seq, D_MODEL),
            ys.reshape(nb, 1, D_MODEL),
            conv_p[None],
            pool_p[None],
            conv_s.reshape(1, nb, CONV_WIDTH - 1, D_CONV),
            pool_s.reshape(1, nb, POOL_STATE, D_POOL))
```

```python
import functools

import jax
import jax.numpy as jnp
from jax import lax
from jax.experimental import pallas as pl
from jax.experimental.pallas import tpu as pltpu

D_MODEL = 1024
D_CONV = D_MODEL
D_POOL = D_MODEL // 2
N_POOL_GROUPS = 4
POOL_GROUP = D_POOL // N_POOL_GROUPS
POOL_OUT_GROUP = D_MODEL // N_POOL_GROUPS
POOL_WINDOWS = (2, 4, 8, 16)
POOL_STATE = max(POOL_WINDOWS) - 1
CONV_WIDTH = 3
D_FF = 2816
DEPTH = 1
ALPHA = (2.0 * DEPTH) ** 0.25
LN_EPS = 1e-5
D_IN = 3 * D_CONV + D_POOL + 2 * D_MODEL

OFF_C = 0
OFF_B = D_CONV
OFF_H = 2 * D_CONV
OFF_P = 3 * D_CONV
OFF_GA = 3 * D_CONV + D_POOL
OFF_GP = 3 * D_CONV + D_POOL + D_MODEL

SUB_ROWS = 256
FF_CHUNK = 256
POOL_HALO = 16
CONV_HALO = 8
VMEM_LIMIT_BYTES = 52 * 1024 * 1024


def _layer_norm(v, g, b):
    mu = jnp.mean(v, axis=-1, keepdims=True)
    d = v - mu
    var = jnp.mean(d * d, axis=-1, keepdims=True)
    return d * lax.rsqrt(var + LN_EPS) * g + b


def _const_spec(shape):
    zeros = (0,) * len(shape)
    return pl.BlockSpec(shape, lambda *_: zeros, pipeline_mode=pl.Buffered(1))


def _ffn_kernel(x_ref, wg_ref, wu_ref, wd_ref, g_ref, b_ref, o_ref, h_ref, *, sub):
    for s in range(x_ref.shape[0] // sub):
        rows = slice(s * sub, (s + 1) * sub)
        x = x_ref[rows, :]
        xb = x.astype(jnp.bfloat16)
        for c in range(D_FF // FF_CHUNK):
            sl = slice(c * FF_CHUNK, (c + 1) * FF_CHUNK)
            gate = jnp.dot(xb, wg_ref[:, sl], preferred_element_type=jnp.float32)
            up = jnp.dot(xb, wu_ref[:, sl], preferred_element_type=jnp.float32)
            h_ref[rows, sl] = (gate * jax.nn.sigmoid(gate) * up).astype(jnp.bfloat16)
        y = jnp.dot(h_ref[rows, :], wd_ref[...], preferred_element_type=jnp.float32)
        o_ref[rows, :] = _layer_norm(ALPHA * x + 0.5 * y, g_ref[...], b_ref[...])


def _ffn(x2d, wg, wu, wd, g, b, tm):
    n = x2d.shape[0]
    return pl.pallas_call(
        functools.partial(_ffn_kernel, sub=min(tm, SUB_ROWS)),
        out_shape=jax.ShapeDtypeStruct((n, D_MODEL), jnp.float32),
        grid=(n // tm,),
        in_specs=[
            pl.BlockSpec((tm, D_MODEL), lambda i: (i, 0)),
            _const_spec((D_MODEL, D_FF)),
            _const_spec((D_MODEL, D_FF)),
            _const_spec((D_FF, D_MODEL)),
            _const_spec((1, D_MODEL)),
            _const_spec((1, D_MODEL)),
        ],
        out_specs=pl.BlockSpec((tm, D_MODEL), lambda i: (i, 0)),
        scratch_shapes=[pltpu.VMEM((tm, D_FF), jnp.bfloat16)],
        compiler_params=pltpu.CompilerParams(
            dimension_semantics=("arbitrary",), vmem_limit_bytes=VMEM_LIMIT_BYTES),
        name="ffn",
    )(x2d, wg, wu, wd, g, b)


def _proj(xb, w_in_ref, off, width):
    return jnp.dot(xb, w_in_ref[:, off:off + width], preferred_element_type=jnp.float32)


def _mixer_prompt_kernel(x_ref, w_in_ref, conv_w_ref, pool_w_ref, pool_scale_ref, w_out_ref,
                         g_ref, b_ref, o_ref, conv_out_ref, pool_out_ref,
                         u_buf, p_buf, m_buf, *, tm):
    t = pl.program_id(1)

    @pl.when(t == 0)
    def _():
        u_buf[0:CONV_HALO, :] = jnp.zeros((CONV_HALO, D_CONV), jnp.float32)
        p_buf[0:POOL_HALO, :] = jnp.zeros((POOL_HALO, D_POOL), jnp.float32)

    x = x_ref[0]
    xb = x.astype(jnp.bfloat16)

    p = _proj(xb, w_in_ref, OFF_P, D_POOL)
    p_buf[POOL_HALO:POOL_HALO + tm, :] = p

    pos = t * tm + lax.broadcasted_iota(jnp.int32, (tm, 1), 0)

    for j in range(N_POOL_GROUPS):
        csl = slice(j * POOL_OUT_GROUP, (j + 1) * POOL_OUT_GROUP)
        c_g = _proj(xb, w_in_ref, OFF_C + j * POOL_OUT_GROUP, POOL_OUT_GROUP)
        h = _proj(xb, w_in_ref, OFF_H + j * POOL_OUT_GROUP, POOL_OUT_GROUP)
        u = c_g * h
        u_buf[CONV_HALO:CONV_HALO + tm, csl] = u
        cw = conv_w_ref[:, csl]
        conv = (cw[0:1] * u_buf[CONV_HALO - 2:CONV_HALO - 2 + tm, csl]
                + cw[1:2] * u_buf[CONV_HALO - 1:CONV_HALO - 1 + tm, csl]
                + cw[2:3] * u)
        b_g = _proj(xb, w_in_ref, OFF_B + j * POOL_OUT_GROUP, POOL_OUT_GROUP)
        y_a = b_g * conv

        psl = slice(j * POOL_GROUP, (j + 1) * POOL_GROUP)
        w = POOL_WINDOWS[j]
        pj = p[:, psl]
        wsum = pj
        for k in range(1, w):
            wsum = wsum + p_buf[POOL_HALO - k:POOL_HALO - k + tm, psl]
        cnt = jnp.minimum(pos + 1, w).astype(jnp.float32)
        q = wsum / cnt - pj
        y_p = jnp.dot(q.astype(jnp.bfloat16), pool_w_ref[j],
                      preferred_element_type=jnp.float32) * pool_scale_ref[:, csl]

        g_a = _proj(xb, w_in_ref, OFF_GA + j * POOL_OUT_GROUP, POOL_OUT_GROUP)
        g_p = _proj(xb, w_in_ref, OFF_GP + j * POOL_OUT_GROUP, POOL_OUT_GROUP)
        m = jax.nn.sigmoid(g_a) * y_a + jax.nn.sigmoid(g_p) * y_p
        m_buf[:, csl] = m.astype(jnp.bfloat16)

    out = jnp.dot(m_buf[...], w_out_ref[...], preferred_element_type=jnp.float32)
    o_ref[0] = _layer_norm(ALPHA * x + out, g_ref[...], b_ref[...])

    conv_tail = u_buf[CONV_HALO + tm - (CONV_WIDTH - 1):CONV_HALO + tm, :]
    pool_tail = p_buf[POOL_HALO + tm - POOL_STATE:POOL_HALO + tm, :]
    u_buf[CONV_HALO - (CONV_WIDTH - 1):CONV_HALO, :] = conv_tail
    p_buf[POOL_HALO - POOL_STATE:POOL_HALO, :] = pool_tail
    conv_out_ref[0] = conv_tail
    pool_out_ref[0] = pool_tail


def _mixer_prompt(x, w_in, conv_w, pool_w, pool_scale, w_out, g, b, tm):
    bsz, seq, _ = x.shape
    kern = functools.partial(_mixer_prompt_kernel, tm=tm)
    return pl.pallas_call(
        kern,
        out_shape=(
            jax.ShapeDtypeStruct((bsz, seq, D_MODEL), jnp.float32),
            jax.ShapeDtypeStruct((bsz, CONV_WIDTH - 1, D_CONV), jnp.float32),
            jax.ShapeDtypeStruct((bsz, POOL_STATE, D_POOL), jnp.float32),
        ),
        grid=(bsz, seq // tm),
        in_specs=[
            pl.BlockSpec((1, tm, D_MODEL), lambda s, t: (s, t, 0)),
            _const_spec((D_MODEL, D_IN)),
            _const_spec((CONV_WIDTH, D_CONV)),
            _const_spec((N_POOL_GROUPS, POOL_GROUP, POOL_OUT_GROUP)),
            _const_spec((1, D_MODEL)),
            _const_spec((D_MODEL, D_MODEL)),
            _const_spec((1, D_MODEL)),
            _const_spec((1, D_MODEL)),
        ],
        out_specs=(
            pl.BlockSpec((1, tm, D_MODEL), lambda s, t: (s, t, 0)),
            pl.BlockSpec((1, CONV_WIDTH - 1, D_CONV), lambda s, t: (s, 0, 0)),
            pl.BlockSpec((1, POOL_STATE, D_POOL), lambda s, t: (s, 0, 0)),
        ),
        scratch_shapes=[
            pltpu.VMEM((CONV_HALO + tm, D_CONV), jnp.float32),
            pltpu.VMEM((POOL_HALO + tm, D_POOL), jnp.float32),
            pltpu.VMEM((tm, D_MODEL), jnp.bfloat16),
        ],
        compiler_params=pltpu.CompilerParams(
            dimension_semantics=("arbitrary", "arbitrary"), vmem_limit_bytes=VMEM_LIMIT_BYTES),
        name="mixer_prompt",
    )(x, w_in, conv_w, pool_w, pool_scale, w_out, g, b)


def _mixer_sample_kernel(x_ref, cs_ref, ps_ref, w_in_ref, conv_w_ref, pool_w_ref, pool_scale_ref,
                         w_out_ref, g_ref, b_ref, o_ref, conv_out_ref, pool_out_ref, m_buf):
    x = x_ref[...]
    xb = x.astype(jnp.bfloat16)
    p = _proj(xb, w_in_ref, OFF_P, D_POOL)

    for j in range(N_POOL_GROUPS):
        csl = slice(j * POOL_OUT_GROUP, (j + 1) * POOL_OUT_GROUP)
        c_g = _proj(xb, w_in_ref, OFF_C + j * POOL_OUT_GROUP, POOL_OUT_GROUP)
        h = _proj(xb, w_in_ref, OFF_H + j * POOL_OUT_GROUP, POOL_OUT_GROUP)
        u = c_g * h
        conv_out_ref[:, D_CONV + j * POOL_OUT_GROUP:D_CONV + (j + 1) * POOL_OUT_GROUP] = u
        cw = conv_w_ref[:, csl]
        u_prev2 = cs_ref[:, j * POOL_OUT_GROUP:(j + 1) * POOL_OUT_GROUP]
        u_prev1 = cs_ref[:, D_CONV + j * POOL_OUT_GROUP:D_CONV + (j + 1) * POOL_OUT_GROUP]
        conv = cw[0:1] * u_prev2 + cw[1:2] * u_prev1 + cw[2:3] * u
        b_g = _proj(xb, w_in_ref, OFF_B + j * POOL_OUT_GROUP, POOL_OUT_GROUP)
        y_a = b_g * conv

        w = POOL_WINDOWS[j]
        pj = p[:, j * POOL_GROUP:(j + 1) * POOL_GROUP]
        wsum = pj
        for k in range(1, w):
            row = POOL_STATE - k
            wsum = wsum + ps_ref[:, row * D_POOL + j * POOL_GROUP:row * D_POOL + (j + 1) * POOL_GROUP]
        q = wsum / float(w) - pj
        y_p = jnp.dot(q.astype(jnp.bfloat16), pool_w_ref[j],
                      preferred_element_type=jnp.float32) * pool_scale_ref[:, csl]

        g_a = _proj(xb, w_in_ref, OFF_GA + j * POOL_OUT_GROUP, POOL_OUT_GROUP)
        g_p = _proj(xb, w_in_ref, OFF_GP + j * POOL_OUT_GROUP, POOL_OUT_GROUP)
        m = jax.nn.sigmoid(g_a) * y_a + jax.nn.sigmoid(g_p) * y_p
        m_buf[:, csl] = m.astype(jnp.bfloat16)

    out = jnp.dot(m_buf[...], w_out_ref[...], preferred_element_type=jnp.float32)
    o_ref[...] = _layer_norm(ALPHA * x + out, g_ref[...], b_ref[...])

    conv_out_ref[:, 0:D_CONV] = cs_ref[:, D_CONV:2 * D_CONV]
    pool_out_ref[:, 0:(POOL_STATE - 1) * D_POOL] = ps_ref[:, D_POOL:POOL_STATE * D_POOL]
    pool_out_ref[:, (POOL_STATE - 1) * D_POOL:POOL_STATE * D_POOL] = p


def _mixer_sample(x2d, cs2d, ps2d, w_in, conv_w, pool_w, pool_scale, w_out, g, b):
    nb = x2d.shape[0]
    shapes = [x2d.shape, cs2d.shape, ps2d.shape, w_in.shape, conv_w.shape, pool_w.shape,
              pool_scale.shape, w_out.shape, g.shape, b.shape]
    return pl.pallas_call(
        _mixer_sample_kernel,
        out_shape=(
            jax.ShapeDtypeStruct((nb, D_MODEL), jnp.float32),
            jax.ShapeDtypeStruct(cs2d.shape, jnp.float32),
            jax.ShapeDtypeStruct(ps2d.shape, jnp.float32),
        ),
        grid=(1,),
        in_specs=[_const_spec(s) for s in shapes],
        out_specs=(
            pl.BlockSpec((nb, D_MODEL), lambda i: (0, 0)),
            pl.BlockSpec(cs2d.shape, lambda i: (0, 0)),
            pl.BlockSpec(ps2d.shape, lambda i: (0, 0)),
        ),
        scratch_shapes=[pltpu.VMEM((nb, D_MODEL), jnp.bfloat16)],
        compiler_params=pltpu.CompilerParams(
            dimension_semantics=("arbitrary",), vmem_limit_bytes=VMEM_LIMIT_BYTES),
        name="mixer_sample",
    )(x2d, cs2d, ps2d, w_in, conv_w, pool_w, pool_scale, w_out, g, b)


def kernel(x_prompt, x_sample, state_conv, state_pool, ln1_g, ln1_b, ffn1_wg, ffn1_wu, ffn1_wd,
           w_in, conv_w, pool_w, pool_scale, w_out, ln2_g, ln2_b,
           ffn2_wg, ffn2_wu, ffn2_wd, ln3_g, ln3_b):
    bsz, seq, _ = x_prompt.shape
    nb = x_sample.shape[0]
    assert x_sample.shape[1] == 1 and DEPTH == 1
    bf = jnp.bfloat16
    tm = 256
    tm_ffn = 1024

    yp = x_prompt.reshape(bsz * seq, D_MODEL)
    ys = x_sample.reshape(nb, D_MODEL)
    l = 0
    wg1, wu1, wd1 = ffn1_wg[l].astype(bf), ffn1_wu[l].astype(bf), ffn1_wd[l].astype(bf)
    wg2, wu2, wd2 = ffn2_wg[l].astype(bf), ffn2_wu[l].astype(bf), ffn2_wd[l].astype(bf)
    w_in_b, w_out_b, pool_w_b = w_in[l].astype(bf), w_out[l].astype(bf), pool_w[l].astype(bf)
    g1, b1 = ln1_g[l][None], ln1_b[l][None]
    g2, b2 = ln2_g[l][None], ln2_b[l][None]
    g3, b3 = ln3_g[l][None], ln3_b[l][None]
    scale = pool_scale[l][None]

    yp = _ffn(yp, wg1, wu1, wd1, g1, b1, tm_ffn)
    yp, conv_p, pool_p = _mixer_prompt(yp.reshape(bsz, seq, D_MODEL), w_in_b, conv_w[l], pool_w_b,
                                       scale, w_out_b, g2, b2, tm)
    yp = _ffn(yp.reshape(bsz * seq, D_MODEL), wg2, wu2, wd2, g3, b3, tm_ffn)

    cs2d = state_conv[l].reshape(nb, (CONV_WIDTH - 1) * D_CONV)
    ps2d = state_pool[l].reshape(nb, POOL_STATE * D_POOL)
    ys = _ffn(ys, wg1, wu1, wd1, g1, b1, nb)
    ys, conv_s, pool_s = _mixer_sample(ys, cs2d, ps2d, w_in_b, conv_w[l], pool_w_b, scale,
                                       w_out_b, g2, b2)
    ys = _ffn(ys, wg2, wu2, wd2, g3, b3, nb)

    return (yp.reshape(bsz, seq, D_MODEL),
            ys.reshape(nb, 1, D_MODEL),
            conv_p[None],
            pool_p[None],
            conv_s.reshape(1, nb, CONV_WIDTH - 1, D_CONV),
            pool_s.reshape(1, nb, POOL_STATE, D_POOL))
```

```python
import functools

import jax
import jax.numpy as jnp
from jax import lax
from jax.experimental import pallas as pl
from jax.experimental.pallas import tpu as pltpu

D_MODEL = 1024
D_CONV = D_MODEL
D_POOL = D_MODEL // 2
N_POOL_GROUPS = 4
POOL_GROUP = D_POOL // N_POOL_GROUPS
POOL_OUT_GROUP = D_MODEL // N_POOL_GROUPS
POOL_WINDOWS = (2, 4, 8, 16)
POOL_STATE = max(POOL_WINDOWS) - 1
CONV_WIDTH = 3
D_FF = 2816
DEPTH = 1
ALPHA = (2.0 * DEPTH) ** 0.25
LN_EPS = 1e-5
D_IN = 3 * D_CONV + D_POOL + 2 * D_MODEL

OFF_C = 0
OFF_B = D_CONV
OFF_H = 2 * D_CONV
OFF_P = 3 * D_CONV
OFF_GA = 3 * D_CONV + D_POOL
OFF_GP = 3 * D_CONV + D_POOL + D_MODEL

SUB_ROWS = 256
FF_CHUNK = 256
POOL_HALO = 16
CONV_HALO = 8
VMEM_LIMIT_BYTES = 52 * 1024 * 1024


def _layer_norm(v, g, b):
    mu = jnp.mean(v, axis=-1, keepdims=True)
    d = v - mu
    var = jnp.mean(d * d, axis=-1, keepdims=True)
    return d * lax.rsqrt(var + LN_EPS) * g + b


def _const_spec(shape):
    zeros = (0,) * len(shape)
    return pl.BlockSpec(shape, lambda *_: zeros, pipeline_mode=pl.Buffered(1))


def _ffn_kernel(x_ref, wg_ref, wu_ref, wd_ref, g_ref, b_ref, o_ref, h_ref, *, sub):
    for s in range(x_ref.shape[0] // sub):
        rows = slice(s * sub, (s + 1) * sub)
        x = x_ref[rows, :]
        xb = x.astype(jnp.bfloat16)
        for c in range(D_FF // FF_CHUNK):
            sl = slice(c * FF_CHUNK, (c + 1) * FF_CHUNK)
            gate = jnp.dot(xb, wg_ref[:, sl], preferred_element_type=jnp.float32)
            up = jnp.dot(xb, wu_ref[:, sl], preferred_element_type=jnp.float32)
            h_ref[rows, sl] = (gate * jax.nn.sigmoid(gate) * up).astype(jnp.bfloat16)
        y = jnp.dot(h_ref[rows, :], wd_ref[...], preferred_element_type=jnp.float32)
        o_ref[rows, :] = _layer_norm(ALPHA * x + 0.5 * y, g_ref[...], b_ref[...])


def _ffn(x2d, wg, wu, wd, g, b, tm):
    n = x2d.shape[0]
    return pl.pallas_call(
        functools.partial(_ffn_kernel, sub=min(tm, SUB_ROWS)),
        out_shape=jax.ShapeDtypeStruct((n, D_MODEL), jnp.float32),
        grid=(n // tm,),
        in_specs=[
            pl.BlockSpec((tm, D_MODEL), lambda i: (i, 0)),
            _const_spec((D_MODEL, D_FF)),
            _const_spec((D_MODEL, D_FF)),
            _const_spec((D_FF, D_MODEL)),
            _const_spec((1, D_MODEL)),
            _const_spec((1, D_MODEL)),
        ],
        out_specs=pl.BlockSpec((tm, D_MODEL), lambda i: (i, 0)),
        scratch_shapes=[pltpu.VMEM((tm, D_FF), jnp.bfloat16)],
        compiler_params=pltpu.CompilerParams(
            dimension_semantics=("arbitrary",), vmem_limit_bytes=VMEM_LIMIT_BYTES),
        name="ffn",
    )(x2d, wg, wu, wd, g, b)


def _proj(xb, w_in_ref, off, width):
    return jnp.dot(xb, w_in_ref[:, off:off + width], preferred_element_type=jnp.float32)


def _shift_rows(ext, k, halo):
    return pltpu.roll(ext, k, 0)[halo:]


def _mixer_prompt_kernel(x_ref, w_in_ref, conv_w_ref, pool_w_ref, pool_scale_ref, w_out_ref,
                         g_ref, b_ref, o_ref, conv_out_ref, pool_out_ref,
                         u_carry, p_carry, m_buf, *, sub):
    t = pl.program_id(1)
    tm = x_ref.shape[1]

    @pl.when(t == 0)
    def _():
        u_carry[...] = jnp.zeros(u_carry.shape, jnp.float32)
        p_carry[...] = jnp.zeros(p_carry.shape, jnp.float32)

    u_prev = u_carry[...]
    p_prev = p_carry[...]

    for s in range(tm // sub):
        rows = slice(s * sub, (s + 1) * sub)
        x = x_ref[0, rows, :]
        xb = x.astype(jnp.bfloat16)

        p = _proj(xb, w_in_ref, OFF_P, D_POOL)
        p_ext = jnp.concatenate([p_prev, p], axis=0)
        pos = t * tm + s * sub + lax.broadcasted_iota(jnp.int32, (sub, POOL_GROUP), 0)

        u_tails = []
        for j in range(N_POOL_GROUPS):
            csl = slice(j * POOL_OUT_GROUP, (j + 1) * POOL_OUT_GROUP)
            c_g = _proj(xb, w_in_ref, OFF_C + j * POOL_OUT_GROUP, POOL_OUT_GROUP)
            h = _proj(xb, w_in_ref, OFF_H + j * POOL_OUT_GROUP, POOL_OUT_GROUP)
            u = c_g * h
            u_ext = jnp.concatenate([u_prev[:, csl], u], axis=0)
            u_tails.append(u[sub - CONV_HALO:, :])
            cw = conv_w_ref[:, csl]
            conv = (cw[0:1] * _shift_rows(u_ext, 2, CONV_HALO)
                    + cw[1:2] * _shift_rows(u_ext, 1, CONV_HALO)
                    + cw[2:3] * u)
            b_g = _proj(xb, w_in_ref, OFF_B + j * POOL_OUT_GROUP, POOL_OUT_GROUP)
            y_a = b_g * conv

            psl = slice(j * POOL_GROUP, (j + 1) * POOL_GROUP)
            w = POOL_WINDOWS[j]
            acc = p_ext[:, psl]
            span = 1
            while span < w:
                acc = acc + pltpu.roll(acc, span, 0)
                span *= 2
            inv_cnt = 1.0 / jnp.minimum(pos + 1, w).astype(jnp.float32)
            q = acc[POOL_HALO:] * inv_cnt - p[:, psl]
            y_p = jnp.dot(q.astype(jnp.bfloat16), pool_w_ref[j],
                          preferred_element_type=jnp.float32) * pool_scale_ref[:, csl]

            g_a = _proj(xb, w_in_ref, OFF_GA + j * POOL_OUT_GROUP, POOL_OUT_GROUP)
            g_p = _proj(xb, w_in_ref, OFF_GP + j * POOL_OUT_GROUP, POOL_OUT_GROUP)
            m = jax.nn.sigmoid(g_a) * y_a + jax.nn.sigmoid(g_p) * y_p
            m_buf[rows, csl] = m.astype(jnp.bfloat16)

        out = jnp.dot(m_buf[rows, :], w_out_ref[...], preferred_element_type=jnp.float32)
        o_ref[0, rows, :] = _layer_norm(ALPHA * x + out, g_ref[...], b_ref[...])
        u_prev = jnp.concatenate(u_tails, axis=1)
        p_prev = p_ext[sub:, :]

    u_carry[...] = u_prev
    p_carry[...] = p_prev
    conv_out_ref[0] = u_carry[CONV_HALO - (CONV_WIDTH - 1):CONV_HALO, :]
    pool_out_ref[0] = p_carry[POOL_HALO - POOL_STATE:POOL_HALO, :]


def _mixer_prompt(x, w_in, conv_w, pool_w, pool_scale, w_out, g, b, tm):
    bsz, seq, _ = x.shape
    kern = functools.partial(_mixer_prompt_kernel, sub=min(tm, SUB_ROWS))
    return pl.pallas_call(
        kern,
        out_shape=(
            jax.ShapeDtypeStruct((bsz, seq, D_MODEL), jnp.float32),
            jax.ShapeDtypeStruct((bsz, CONV_WIDTH - 1, D_CONV), jnp.float32),
            jax.ShapeDtypeStruct((bsz, POOL_STATE, D_POOL), jnp.float32),
        ),
        grid=(bsz, seq // tm),
        in_specs=[
            pl.BlockSpec((1, tm, D_MODEL), lambda s, t: (s, t, 0)),
            _const_spec((D_MODEL, D_IN)),
            _const_spec((CONV_WIDTH, D_CONV)),
            _const_spec((N_POOL_GROUPS, POOL_GROUP, POOL_OUT_GROUP)),
            _const_spec((1, D_MODEL)),
            _const_spec((D_MODEL, D_MODEL)),
            _const_spec((1, D_MODEL)),
            _const_spec((1, D_MODEL)),
        ],
        out_specs=(
            pl.BlockSpec((1, tm, D_MODEL), lambda s, t: (s, t, 0)),
            pl.BlockSpec((1, CONV_WIDTH - 1, D_CONV), lambda s, t: (s, 0, 0)),
            pl.BlockSpec((1, POOL_STATE, D_POOL), lambda s, t: (s, 0, 0)),
        ),
        scratch_shapes=[
            pltpu.VMEM((CONV_HALO, D_CONV), jnp.float32),
            pltpu.VMEM((POOL_HALO, D_POOL), jnp.float32),
            pltpu.VMEM((tm, D_MODEL), jnp.bfloat16),
        ],
        compiler_params=pltpu.CompilerParams(
            dimension_semantics=("arbitrary", "arbitrary"), vmem_limit_bytes=VMEM_LIMIT_BYTES),
        name="mixer_prompt",
    )(x, w_in, conv_w, pool_w, pool_scale, w_out, g, b)


def _mixer_sample_kernel(x_ref, cs_ref, ps_ref, w_in_ref, conv_w_ref, pool_w_ref, pool_scale_ref,
                         w_out_ref, g_ref, b_ref, o_ref, conv_out_ref, pool_out_ref, m_buf):
    x = x_ref[...]
    xb = x.astype(jnp.bfloat16)
    p = _proj(xb, w_in_ref, OFF_P, D_POOL)

    for j in range(N_POOL_GROUPS):
        csl = slice(j * POOL_OUT_GROUP, (j + 1) * POOL_OUT_GROUP)
        c_g = _proj(xb, w_in_ref, OFF_C + j * POOL_OUT_GROUP, POOL_OUT_GROUP)
        h = _proj(xb, w_in_ref, OFF_H + j * POOL_OUT_GROUP, POOL_OUT_GROUP)
        u = c_g * h
        conv_out_ref[:, D_CONV + j * POOL_OUT_GROUP:D_CONV + (j + 1) * POOL_OUT_GROUP] = u
        cw = conv_w_ref[:, csl]
        u_prev2 = cs_ref[:, j * POOL_OUT_GROUP:(j + 1) * POOL_OUT_GROUP]
        u_prev1 = cs_ref[:, D_CONV + j * POOL_OUT_GROUP:D_CONV + (j + 1) * POOL_OUT_GROUP]
        conv = cw[0:1] * u_prev2 + cw[1:2] * u_prev1 + cw[2:3] * u
        b_g = _proj(xb, w_in_ref, OFF_B + j * POOL_OUT_GROUP, POOL_OUT_GROUP)
        y_a = b_g * conv

        w = POOL_WINDOWS[j]
        pj = p[:, j * POOL_GROUP:(j + 1) * POOL_GROUP]
        wsum = pj
        for k in range(1, w):
            row = POOL_STATE - k
            wsum = wsum + ps_ref[:, row * D_POOL + j * POOL_GROUP:row * D_POOL + (j + 1) * POOL_GROUP]
        q = wsum / float(w) - pj
        y_p = jnp.dot(q.astype(jnp.bfloat16), pool_w_ref[j],
                      preferred_element_type=jnp.float32) * pool_scale_ref[:, csl]

        g_a = _proj(xb, w_in_ref, OFF_GA + j * POOL_OUT_GROUP, POOL_OUT_GROUP)
        g_p = _proj(xb, w_in_ref, OFF_GP + j * POOL_OUT_GROUP, POOL_OUT_GROUP)
        m = jax.nn.sigmoid(g_a) * y_a + jax.nn.sigmoid(g_p) * y_p
        m_buf[:, csl] = m.astype(jnp.bfloat16)

    out = jnp.dot(m_buf[...], w_out_ref[...], preferred_element_type=jnp.float32)
    o_ref[...] = _layer_norm(ALPHA * x + out, g_ref[...], b_ref[...])

    conv_out_ref[:, 0:D_CONV] = cs_ref[:, D_CONV:2 * D_CONV]
    pool_out_ref[:, 0:(POOL_STATE - 1) * D_POOL] = ps_ref[:, D_POOL:POOL_STATE * D_POOL]
    pool_out_ref[:, (POOL_STATE - 1) * D_POOL:POOL_STATE * D_POOL] = p


def _mixer_sample(x2d, cs2d, ps2d, w_in, conv_w, pool_w, pool_scale, w_out, g, b):
    nb = x2d.shape[0]
    shapes = [x2d.shape, cs2d.shape, ps2d.shape, w_in.shape, conv_w.shape, pool_w.shape,
              pool_scale.shape, w_out.shape, g.shape, b.shape]
    return pl.pallas_call(
        _mixer_sample_kernel,
        out_shape=(
            jax.ShapeDtypeStruct((nb, D_MODEL), jnp.float32),
            jax.ShapeDtypeStruct(cs2d.shape, jnp.float32),
            jax.ShapeDtypeStruct(ps2d.shape, jnp.float32),
        ),
        grid=(1,),
        in_specs=[_const_spec(s) for s in shapes],
        out_specs=(
            pl.BlockSpec((nb, D_MODEL), lambda i: (0, 0)),
            pl.BlockSpec(cs2d.shape, lambda i: (0, 0)),
            pl.BlockSpec(ps2d.shape, lambda i: (0, 0)),
        ),
        scratch_shapes=[pltpu.VMEM((nb, D_MODEL), jnp.bfloat16)],
        compiler_params=pltpu.CompilerParams(
            dimension_semantics=("arbitrary",), vmem_limit_bytes=VMEM_LIMIT_BYTES),
        name="mixer_sample",
    )(x2d, cs2d, ps2d, w_in, conv_w, pool_w, pool_scale, w_out, g, b)


def kernel(x_prompt, x_sample, state_conv, state_pool, ln1_g, ln1_b, ffn1_wg, ffn1_wu, ffn1_wd,
           w_in, conv_w, pool_w, pool_scale, w_out, ln2_g, ln2_b,
           ffn2_wg, ffn2_wu, ffn2_wd, ln3_g, ln3_b):
    bsz, seq, _ = x_prompt.shape
    nb = x_sample.shape[0]
    assert x_sample.shape[1] == 1 and DEPTH == 1
    bf = jnp.bfloat16
    tm = 1024
    tm_ffn = 1024

    yp = x_prompt.reshape(bsz * seq, D_MODEL)
    ys = x_sample.reshape(nb, D_MODEL)
    l = 0
    wg1, wu1, wd1 = ffn1_wg[l].astype(bf), ffn1_wu[l].astype(bf), ffn1_wd[l].astype(bf)
    wg2, wu2, wd2 = ffn2_wg[l].astype(bf), ffn2_wu[l].astype(bf), ffn2_wd[l].astype(bf)
    w_in_b, w_out_b, pool_w_b = w_in[l].astype(bf), w_out[l].astype(bf), pool_w[l].astype(bf)
    g1, b1 = ln1_g[l][None], ln1_b[l][None]
    g2, b2 = ln2_g[l][None], ln2_b[l][None]
    g3, b3 = ln3_g[l][None], ln3_b[l][None]
    scale = pool_scale[l][None]

    yp = _ffn(yp, wg1, wu1, wd1, g1, b1, tm_ffn)
    yp, conv_p, pool_p = _mixer_prompt(yp.reshape(bsz, seq, D_MODEL), w_in_b, conv_w[l], pool_w_b,
                                       scale, w_out_b, g2, b2, tm)
    yp = _ffn(yp.reshape(bsz * seq, D_MODEL), wg2, wu2, wd2, g3, b3, tm_ffn)

    cs2d = state_conv[l].reshape(nb, (CONV_WIDTH - 1) * D_CONV)
    ps2d = state_pool[l].reshape(nb, POOL_STATE * D_POOL)
    ys = _ffn(ys, wg1, wu1, wd1, g1, b1, nb)
    ys, conv_s, pool_s = _mixer_sample(ys, cs2d, ps2d, w_in_b, conv_w[l], pool_w_b, scale,
                                       w_out_b, g2, b2)
    ys = _ffn(ys, wg2, wu2, wd2, g3, b3, nb)

    return (yp.reshape(bsz, seq, D_MODEL),
            ys.reshape(nb, 1, D_MODEL),
            conv_p[None],
            pool_p[None],
            conv_s.reshape(1, nb, CONV_WIDTH - 1, D_CONV),
            pool_s.reshape(1, nb, POOL_STATE, D_POOL))
```

```python
import functools

import jax
import jax.numpy as jnp
from jax import lax
from jax.experimental import pallas as pl
from jax.experimental.pallas import tpu as pltpu

D_MODEL = 1024
D_CONV = D_MODEL
D_POOL = D_MODEL // 2
N_POOL_GROUPS = 4
POOL_GROUP = D_POOL // N_POOL_GROUPS
POOL_OUT_GROUP = D_MODEL // N_POOL_GROUPS
POOL_WINDOWS = (2, 4, 8, 16)
POOL_STATE = max(POOL_WINDOWS) - 1
CONV_WIDTH = 3
D_FF = 2816
DEPTH = 1
ALPHA = (2.0 * DEPTH) ** 0.25
LN_EPS = 1e-5
D_IN = 3 * D_CONV + D_POOL + 2 * D_MODEL

OFF_C = 0
OFF_B = D_CONV
OFF_H = 2 * D_CONV
OFF_P = 3 * D_CONV
OFF_GA = 3 * D_CONV + D_POOL
OFF_GP = 3 * D_CONV + D_POOL + D_MODEL

SUB_ROWS = 256
FF_CHUNK = 256
POOL_HALO = 16
CONV_HALO = 8
VMEM_LIMIT_BYTES = 52 * 1024 * 1024


def _layer_norm(v, g, b):
    mu = jnp.mean(v, axis=-1, keepdims=True)
    d = v - mu
    var = jnp.mean(d * d, axis=-1, keepdims=True)
    return d * lax.rsqrt(var + LN_EPS) * g + b


def _const_spec(shape):
    zeros = (0,) * len(shape)
    return pl.BlockSpec(shape, lambda *_: zeros, pipeline_mode=pl.Buffered(1))


def _cast_specs(arrays, n_steps, step_index):
    in_specs, out_specs, out_shapes = [], [], []
    for a in arrays:
        rows, cols = a.shape
        assert rows % (n_steps * 16) == 0
        spec = pl.BlockSpec((rows // n_steps, cols), lambda *idx: (step_index(*idx), 0))
        in_specs.append(spec)
        out_specs.append(spec)
        out_shapes.append(jax.ShapeDtypeStruct(a.shape, jnp.bfloat16))
    return in_specs, out_specs, out_shapes


def _run_casts(src_refs, dst_refs):
    for src, dst in zip(src_refs, dst_refs):
        dst[...] = src[...].astype(jnp.bfloat16)


def _ffn_kernel(x_ref, wg_ref, wu_ref, wd_ref, g_ref, b_ref, *rest, sub, n_cast):
    cast_src, (o_ref, *cast_dst), h_ref = rest[:n_cast], rest[n_cast:2 * n_cast + 1], rest[-1]
    _run_casts(cast_src, cast_dst)
    for s in range(x_ref.shape[0] // sub):
        rows = slice(s * sub, (s + 1) * sub)
        x = x_ref[rows, :]
        xb = x.astype(jnp.bfloat16)
        for c in range(D_FF // FF_CHUNK):
            sl = slice(c * FF_CHUNK, (c + 1) * FF_CHUNK)
            gate = jnp.dot(xb, wg_ref[:, sl], preferred_element_type=jnp.float32)
            up = jnp.dot(xb, wu_ref[:, sl], preferred_element_type=jnp.float32)
            h_ref[rows, sl] = (gate * jax.nn.sigmoid(gate) * up).astype(jnp.bfloat16)
        y = jnp.dot(h_ref[rows, :], wd_ref[...], preferred_element_type=jnp.float32)
        o_ref[rows, :] = _layer_norm(ALPHA * x + 0.5 * y, g_ref[...], b_ref[...])


def _ffn(x2d, wg, wu, wd, g, b, tm, cast=()):
    n = x2d.shape[0]
    cast_in, cast_out, cast_shapes = _cast_specs(cast, n // tm, lambda i: i)
    outs = pl.pallas_call(
        functools.partial(_ffn_kernel, sub=min(tm, SUB_ROWS), n_cast=len(cast)),
        out_shape=[jax.ShapeDtypeStruct((n, D_MODEL), jnp.float32)] + cast_shapes,
        grid=(n // tm,),
        in_specs=[
            pl.BlockSpec((tm, D_MODEL), lambda i: (i, 0)),
            _const_spec((D_MODEL, D_FF)),
            _const_spec((D_MODEL, D_FF)),
            _const_spec((D_FF, D_MODEL)),
            _const_spec((1, D_MODEL)),
            _const_spec((1, D_MODEL)),
        ] + cast_in,
        out_specs=[pl.BlockSpec((tm, D_MODEL), lambda i: (i, 0))] + cast_out,
        scratch_shapes=[pltpu.VMEM((tm, D_FF), jnp.bfloat16)],
        compiler_params=pltpu.CompilerParams(
            dimension_semantics=("arbitrary",), vmem_limit_bytes=VMEM_LIMIT_BYTES),
        name="ffn",
    )(x2d, wg, wu, wd, g, b, *cast)
    return outs[0], outs[1:]


def _proj(xb, w_in_ref, off, width):
    return jnp.dot(xb, w_in_ref[:, off:off + width], preferred_element_type=jnp.float32)


def _shift_rows(ext, k, halo):
    return pltpu.roll(ext, k, 0)[halo:]


def _mixer_prompt_kernel(x_ref, w_in_ref, conv_w_ref, pool_w_ref, pool_scale_ref, w_out_ref,
                         g_ref, b_ref, *rest, sub, n_cast):
    cast_src = rest[:n_cast]
    o_ref, conv_out_ref, pool_out_ref, *cast_dst = rest[n_cast:2 * n_cast + 3]
    u_carry, p_carry, m_buf = rest[2 * n_cast + 3:]
    _run_casts(cast_src, cast_dst)
    t = pl.program_id(1)
    tm = x_ref.shape[1]

    @pl.when(t == 0)
    def _():
        u_carry[...] = jnp.zeros(u_carry.shape, jnp.float32)
        p_carry[...] = jnp.zeros(p_carry.shape, jnp.float32)

    u_prev = u_carry[...]
    p_prev = p_carry[...]

    for s in range(tm // sub):
        rows = slice(s * sub, (s + 1) * sub)
        x = x_ref[0, rows, :]
        xb = x.astype(jnp.bfloat16)

        p = _proj(xb, w_in_ref, OFF_P, D_POOL)
        p_ext = jnp.concatenate([p_prev, p], axis=0)
        pos = t * tm + s * sub + lax.broadcasted_iota(jnp.int32, (sub, POOL_GROUP), 0)

        u_tails = []
        for j in range(N_POOL_GROUPS):
            csl = slice(j * POOL_OUT_GROUP, (j + 1) * POOL_OUT_GROUP)
            c_g = _proj(xb, w_in_ref, OFF_C + j * POOL_OUT_GROUP, POOL_OUT_GROUP)
            h = _proj(xb, w_in_ref, OFF_H + j * POOL_OUT_GROUP, POOL_OUT_GROUP)
            u = c_g * h
            u_ext = jnp.concatenate([u_prev[:, csl], u], axis=0)
            u_tails.append(u[sub - CONV_HALO:, :])
            cw = conv_w_ref[:, csl]
            conv = (cw[0:1] * _shift_rows(u_ext, 2, CONV_HALO)
                    + cw[1:2] * _shift_rows(u_ext, 1, CONV_HALO)
                    + cw[2:3] * u)
            b_g = _proj(xb, w_in_ref, OFF_B + j * POOL_OUT_GROUP, POOL_OUT_GROUP)
            y_a = b_g * conv

            psl = slice(j * POOL_GROUP, (j + 1) * POOL_GROUP)
            w = POOL_WINDOWS[j]
            acc = p_ext[:, psl]
            span = 1
            while span < w:
                acc = acc + pltpu.roll(acc, span, 0)
                span *= 2
            inv_cnt = 1.0 / jnp.minimum(pos + 1, w).astype(jnp.float32)
            q = acc[POOL_HALO:] * inv_cnt - p[:, psl]
            y_p = jnp.dot(q.astype(jnp.bfloat16), pool_w_ref[j],
                          preferred_element_type=jnp.float32) * pool_scale_ref[:, csl]

            g_a = _proj(xb, w_in_ref, OFF_GA + j * POOL_OUT_GROUP, POOL_OUT_GROUP)
            g_p = _proj(xb, w_in_ref, OFF_GP + j * POOL_OUT_GROUP, POOL_OUT_GROUP)
            m = jax.nn.sigmoid(g_a) * y_a + jax.nn.sigmoid(g_p) * y_p
            m_buf[rows, csl] = m.astype(jnp.bfloat16)

        out = jnp.dot(m_buf[rows, :], w_out_ref[...], preferred_element_type=jnp.float32)
        o_ref[0, rows, :] = _layer_norm(ALPHA * x + out, g_ref[...], b_ref[...])
        u_prev = jnp.concatenate(u_tails, axis=1)
        p_prev = p_ext[sub:, :]

    u_carry[...] = u_prev
    p_carry[...] = p_prev
    conv_out_ref[0] = u_carry[CONV_HALO - (CONV_WIDTH - 1):CONV_HALO, :]
    pool_out_ref[0] = p_carry[POOL_HALO - POOL_STATE:POOL_HALO, :]


def _mixer_prompt(x, w_in, conv_w, pool_w, pool_scale, w_out, g, b, tm, cast=()):
    bsz, seq, _ = x.shape
    tiles = seq // tm
    cast_in, cast_out, cast_shapes = _cast_specs(cast, bsz * tiles, lambda s, t: s * tiles + t)
    kern = functools.partial(_mixer_prompt_kernel, sub=min(tm, SUB_ROWS), n_cast=len(cast))
    outs = pl.pallas_call(
        kern,
        out_shape=[
            jax.ShapeDtypeStruct((bsz, seq, D_MODEL), jnp.float32),
            jax.ShapeDtypeStruct((bsz, CONV_WIDTH - 1, D_CONV), jnp.float32),
            jax.ShapeDtypeStruct((bsz, POOL_STATE, D_POOL), jnp.float32),
        ] + cast_shapes,
        grid=(bsz, tiles),
        in_specs=[
            pl.BlockSpec((1, tm, D_MODEL), lambda s, t: (s, t, 0)),
            _const_spec((D_MODEL, D_IN)),
            _const_spec((CONV_WIDTH, D_CONV)),
            _const_spec((N_POOL_GROUPS, POOL_GROUP, POOL_OUT_GROUP)),
            _const_spec((1, D_MODEL)),
            _const_spec((D_MODEL, D_MODEL)),
            _const_spec((1, D_MODEL)),
            _const_spec((1, D_MODEL)),
        ] + cast_in,
        out_specs=[
            pl.BlockSpec((1, tm, D_MODEL), lambda s, t: (s, t, 0)),
            pl.BlockSpec((1, CONV_WIDTH - 1, D_CONV), lambda s, t: (s, 0, 0)),
            pl.BlockSpec((1, POOL_STATE, D_POOL), lambda s, t: (s, 0, 0)),
        ] + cast_out,
        scratch_shapes=[
            pltpu.VMEM((CONV_HALO, D_CONV), jnp.float32),
            pltpu.VMEM((POOL_HALO, D_POOL), jnp.float32),
            pltpu.VMEM((tm, D_MODEL), jnp.bfloat16),
        ],
        compiler_params=pltpu.CompilerParams(
            dimension_semantics=("arbitrary", "arbitrary"), vmem_limit_bytes=VMEM_LIMIT_BYTES),
        name="mixer_prompt",
    )(x, w_in, conv_w, pool_w, pool_scale, w_out, g, b, *cast)
    return outs[0], outs[1], outs[2], outs[3:]


def _mixer_sample_kernel(x_ref, cs_ref, ps_ref, w_in_ref, conv_w_ref, pool_w_ref, pool_scale_ref,
                         w_out_ref, g_ref, b_ref, o_ref, conv_out_ref, pool_out_ref, m_buf):
    x = x_ref[...]
    xb = x.astype(jnp.bfloat16)
    p = _proj(xb, w_in_ref, OFF_P, D_POOL)

    for j in range(N_POOL_GROUPS):
        csl = slice(j * POOL_OUT_GROUP, (j + 1) * POOL_OUT_GROUP)
        c_g = _proj(xb, w_in_ref, OFF_C + j * POOL_OUT_GROUP, POOL_OUT_GROUP)
        h = _proj(xb, w_in_ref, OFF_H + j * POOL_OUT_GROUP, POOL_OUT_GROUP)
        u = c_g * h
        conv_out_ref[:, D_CONV + j * POOL_OUT_GROUP:D_CONV + (j + 1) * POOL_OUT_GROUP] = u
        cw = conv_w_ref[:, csl]
        u_prev2 = cs_ref[:, j * POOL_OUT_GROUP:(j + 1) * POOL_OUT_GROUP]
        u_prev1 = cs_ref[:, D_CONV + j * POOL_OUT_GROUP:D_CONV + (j + 1) * POOL_OUT_GROUP]
        conv = cw[0:1] * u_prev2 + cw[1:2] * u_prev1 + cw[2:3] * u
        b_g = _proj(xb, w_in_ref, OFF_B + j * POOL_OUT_GROUP, POOL_OUT_GROUP)
        y_a = b_g * conv

        w = POOL_WINDOWS[j]
        pj = p[:, j * POOL_GROUP:(j + 1) * POOL_GROUP]
        wsum = pj
        for k in range(1, w):
            row = POOL_STATE - k
            wsum = wsum + ps_ref[:, row * D_POOL + j * POOL_GROUP:row * D_POOL + (j + 1) * POOL_GROUP]
        q = wsum / float(w) - pj
        y_p = jnp.dot(q.astype(jnp.bfloat16), pool_w_ref[j],
                      preferred_element_type=jnp.float32) * pool_scale_ref[:, csl]

        g_a = _proj(xb, w_in_ref, OFF_GA + j * POOL_OUT_GROUP, POOL_OUT_GROUP)
        g_p = _proj(xb, w_in_ref, OFF_GP + j * POOL_OUT_GROUP, POOL_OUT_GROUP)
        m = jax.nn.sigmoid(g_a) * y_a + jax.nn.sigmoid(g_p) * y_p
        m_buf[:, csl] = m.astype(jnp.bfloat16)

    out = jnp.dot(m_buf[...], w_out_ref[...], preferred_element_type=jnp.float32)
    o_ref[...] = _layer_norm(ALPHA * x + out, g_ref[...], b_ref[...])

    conv_out_ref[:, 0:D_CONV] = cs_ref[:, D_CONV:2 * D_CONV]
    pool_out_ref[:, 0:(POOL_STATE - 1) * D_POOL] = ps_ref[:, D_POOL:POOL_STATE * D_POOL]
    pool_out_ref[:, (POOL_STATE - 1) * D_POOL:POOL_STATE * D_POOL] = p


def _mixer_sample(x2d, cs2d, ps2d, w_in, conv_w, pool_w, pool_scale, w_out, g, b):
    nb = x2d.shape[0]
    shapes = [x2d.shape, cs2d.shape, ps2d.shape, w_in.shape, conv_w.shape, pool_w.shape,
              pool_scale.shape, w_out.shape, g.shape, b.shape]
    return pl.pallas_call(
        _mixer_sample_kernel,
        out_shape=(
            jax.ShapeDtypeStruct((nb, D_MODEL), jnp.float32),
            jax.ShapeDtypeStruct(cs2d.shape, jnp.float32),
            jax.ShapeDtypeStruct(ps2d.shape, jnp.float32),
        ),
        grid=(1,),
        in_specs=[_const_spec(s) for s in shapes],
        out_specs=(
            pl.BlockSpec((nb, D_MODEL), lambda i: (0, 0)),
            pl.BlockSpec(cs2d.shape, lambda i: (0, 0)),
            pl.BlockSpec(ps2d.shape, lambda i: (0, 0)),
        ),
        scratch_shapes=[pltpu.VMEM((nb, D_MODEL), jnp.bfloat16)],
        compiler_params=pltpu.CompilerParams(
            dimension_semantics=("arbitrary",), vmem_limit_bytes=VMEM_LIMIT_BYTES),
        name="mixer_sample",
    )(x2d, cs2d, ps2d, w_in, conv_w, pool_w, pool_scale, w_out, g, b)


def kernel(x_prompt, x_sample, state_conv, state_pool, ln1_g, ln1_b, ffn1_wg, ffn1_wu, ffn1_wd,
           w_in, conv_w, pool_w, pool_scale, w_out, ln2_g, ln2_b,
           ffn2_wg, ffn2_wu, ffn2_wd, ln3_g, ln3_b):
    bsz, seq, _ = x_prompt.shape
    nb = x_sample.shape[0]
    assert x_sample.shape[1] == 1 and DEPTH == 1
    bf = jnp.bfloat16
    tm = 1024
    tm_ffn = 1024

    yp = x_prompt.reshape(bsz * seq, D_MODEL)
    ys = x_sample.reshape(nb, D_MODEL)
    l = 0
    wg1, wu1, wd1 = ffn1_wg[l].astype(bf), ffn1_wu[l].astype(bf), ffn1_wd[l].astype(bf)
    g1, b1 = ln1_g[l][None], ln1_b[l][None]
    g2, b2 = ln2_g[l][None], ln2_b[l][None]
    g3, b3 = ln3_g[l][None], ln3_b[l][None]
    scale = pool_scale[l][None]

    pool_w2d = pool_w[l].reshape(N_POOL_GROUPS * POOL_GROUP, POOL_OUT_GROUP)
    yp, (w_in_b, w_out_b, pool_w_b) = _ffn(yp, wg1, wu1, wd1, g1, b1, tm_ffn,
                                           cast=(w_in[l], w_out[l], pool_w2d))
    pool_w_b = pool_w_b.reshape(N_POOL_GROUPS, POOL_GROUP, POOL_OUT_GROUP)
    yp, conv_p, pool_p, (wg2, wu2, wd2) = _mixer_prompt(
        yp.reshape(bsz, seq, D_MODEL), w_in_b, conv_w[l], pool_w_b, scale, w_out_b, g2, b2, tm,
        cast=(ffn2_wg[l], ffn2_wu[l], ffn2_wd[l]))
    yp, _ = _ffn(yp.reshape(bsz * seq, D_MODEL), wg2, wu2, wd2, g3, b3, tm_ffn)

    cs2d = state_conv[l].reshape(nb, (CONV_WIDTH - 1) * D_CONV)
    ps2d = state_pool[l].reshape(nb, POOL_STATE * D_POOL)
    ys, _ = _ffn(ys, wg1, wu1, wd1, g1, b1, nb)
    ys, conv_s, pool_s = _mixer_sample(ys, cs2d, ps2d, w_in_b, conv_w[l], pool_w_b, scale,
                                       w_out_b, g2, b2)
    ys, _ = _ffn(ys, wg2, wu2, wd2, g3, b3, nb)

    return (yp.reshape(bsz, seq, D_MODEL),
            ys.reshape(nb, 1, D_MODEL),
            conv_p[None],
            pool_p[None],
            conv_s.reshape(1, nb, CONV_WIDTH - 1, D_CONV),
            pool_s.reshape(1, nb, POOL_STATE, D_POOL))
```

```python
import functools

import jax
import jax.numpy as jnp
from jax import lax
from jax.experimental import pallas as pl
from jax.experimental.pallas import tpu as pltpu

D_MODEL = 1024
D_CONV = D_MODEL
D_POOL = D_MODEL // 2
N_POOL_GROUPS = 4
POOL_GROUP = D_POOL // N_POOL_GROUPS
POOL_OUT_GROUP = D_MODEL // N_POOL_GROUPS
POOL_WINDOWS = (2, 4, 8, 16)
POOL_STATE = max(POOL_WINDOWS) - 1
CONV_WIDTH = 3
D_FF = 2816
DEPTH = 1
ALPHA = (2.0 * DEPTH) ** 0.25
LN_EPS = 1e-5
D_IN = 3 * D_CONV + D_POOL + 2 * D_MODEL

OFF_C = 0
OFF_B = D_CONV
OFF_H = 2 * D_CONV
OFF_P = 3 * D_CONV
OFF_GA = 3 * D_CONV + D_POOL
OFF_GP = 3 * D_CONV + D_POOL + D_MODEL

SUB_ROWS = 256
FF_CHUNK = 256
POOL_HALO = 16
CONV_HALO = 8
VMEM_LIMIT_BYTES = 52 * 1024 * 1024


def _layer_norm(v, g, b):
    mu = jnp.mean(v, axis=-1, keepdims=True)
    d = v - mu
    var = jnp.mean(d * d, axis=-1, keepdims=True)
    return d * lax.rsqrt(var + LN_EPS) * g + b


def _const_spec(shape):
    zeros = (0,) * len(shape)
    return pl.BlockSpec(shape, lambda *_: zeros, pipeline_mode=pl.Buffered(1))


def _cast_specs(arrays, n_steps, step_index):
    in_specs, out_specs, out_shapes = [], [], []
    for a in arrays:
        rows, cols = a.shape
        assert rows % (n_steps * 16) == 0
        spec = pl.BlockSpec((rows // n_steps, cols), lambda *idx: (step_index(*idx), 0))
        in_specs.append(spec)
        out_specs.append(spec)
        out_shapes.append(jax.ShapeDtypeStruct(a.shape, jnp.bfloat16))
    return in_specs, out_specs, out_shapes


def _run_casts(src_refs, dst_refs):
    for src, dst in zip(src_refs, dst_refs):
        dst[...] = src[...].astype(jnp.bfloat16)


def _ffn_rows(x, wg_ref, wu_ref, wd_ref, g_ref, b_ref, h_view):
    xb = x.astype(jnp.bfloat16)
    for c in range(D_FF // FF_CHUNK):
        sl = slice(c * FF_CHUNK, (c + 1) * FF_CHUNK)
        gate = jnp.dot(xb, wg_ref[:, sl], preferred_element_type=jnp.float32)
        up = jnp.dot(xb, wu_ref[:, sl], preferred_element_type=jnp.float32)
        h_view[:, sl] = (gate * jax.nn.sigmoid(gate) * up).astype(jnp.bfloat16)
    y = jnp.dot(h_view[...], wd_ref[...], preferred_element_type=jnp.float32)
    return _layer_norm(ALPHA * x + 0.5 * y, g_ref[...], b_ref[...])


def _ffn_kernel(xp_ref, xs_ref, wg_ref, wu_ref, wd_ref, g_ref, b_ref, *rest, sub, n_cast):
    cast_src, (op_ref, os_ref, *cast_dst), h_ref = rest[:n_cast], rest[n_cast:2 * n_cast + 2], rest[-1]
    i = pl.program_id(0)
    last = pl.num_programs(0) - 1
    weights = (wg_ref, wu_ref, wd_ref, g_ref, b_ref)

    @pl.when(i < last)
    def _():
        _run_casts(cast_src, cast_dst)
        for s in range(xp_ref.shape[0] // sub):
            rows = pl.ds(s * sub, sub)
            op_ref[rows, :] = _ffn_rows(xp_ref[rows, :], *weights, h_ref.at[rows, :])

    @pl.when(i == last)
    def _():
        ns = xs_ref.shape[0]
        os_ref[...] = _ffn_rows(xs_ref[...], *weights, h_ref.at[pl.ds(0, ns), :])


def _ffn(xp, xs, wg, wu, wd, g, b, tm, cast=()):
    n, ns = xp.shape[0], xs.shape[0]
    nt = n // tm
    assert ns <= tm
    blk = lambda i: (jnp.minimum(i, nt - 1), 0)
    cast_in, cast_out, cast_shapes = _cast_specs(cast, nt, lambda i: jnp.minimum(i, nt - 1))
    outs = pl.pallas_call(
        functools.partial(_ffn_kernel, sub=min(tm, SUB_ROWS), n_cast=len(cast)),
        out_shape=[jax.ShapeDtypeStruct((n, D_MODEL), jnp.float32),
                   jax.ShapeDtypeStruct((ns, D_MODEL), jnp.float32)] + cast_shapes,
        grid=(nt + 1,),
        in_specs=[
            pl.BlockSpec((tm, D_MODEL), blk),
            _const_spec((ns, D_MODEL)),
            _const_spec((D_MODEL, D_FF)),
            _const_spec((D_MODEL, D_FF)),
            _const_spec((D_FF, D_MODEL)),
            _const_spec((1, D_MODEL)),
            _const_spec((1, D_MODEL)),
        ] + cast_in,
        out_specs=[pl.BlockSpec((tm, D_MODEL), blk),
                   pl.BlockSpec((ns, D_MODEL), lambda i: (0, 0))] + cast_out,
        scratch_shapes=[pltpu.VMEM((tm, D_FF), jnp.bfloat16)],
        compiler_params=pltpu.CompilerParams(
            dimension_semantics=("arbitrary",), vmem_limit_bytes=VMEM_LIMIT_BYTES),
        name="ffn",
    )(xp, xs, wg, wu, wd, g, b, *cast)
    return outs[0], outs[1], outs[2:]


def _proj(xb, w_in_ref, off, width):
    return jnp.dot(xb, w_in_ref[:, off:off + width], preferred_element_type=jnp.float32)


def _shift_rows(ext, k, halo):
    return pltpu.roll(ext, k, 0)[halo:]


def _mixer_prompt_kernel(x_ref, w_in_ref, conv_w_ref, pool_w_ref, pool_scale_ref, w_out_ref,
                         g_ref, b_ref, *rest, sub, n_cast):
    cast_src = rest[:n_cast]
    o_ref, conv_out_ref, pool_out_ref, *cast_dst = rest[n_cast:2 * n_cast + 3]
    u_carry, p_carry, m_buf = rest[2 * n_cast + 3:]
    _run_casts(cast_src, cast_dst)
    t = pl.program_id(1)
    tm = x_ref.shape[1]

    @pl.when(t == 0)
    def _():
        u_carry[...] = jnp.zeros(u_carry.shape, jnp.float32)
        p_carry[...] = jnp.zeros(p_carry.shape, jnp.float32)

    u_prev = u_carry[...]
    p_prev = p_carry[...]

    for s in range(tm // sub):
        rows = slice(s * sub, (s + 1) * sub)
        x = x_ref[0, rows, :]
        xb = x.astype(jnp.bfloat16)

        p = _proj(xb, w_in_ref, OFF_P, D_POOL)
        p_ext = jnp.concatenate([p_prev, p], axis=0)
        pos = t * tm + s * sub + lax.broadcasted_iota(jnp.int32, (sub, POOL_GROUP), 0)

        u_tails = []
        for j in range(N_POOL_GROUPS):
            csl = slice(j * POOL_OUT_GROUP, (j + 1) * POOL_OUT_GROUP)
            c_g = _proj(xb, w_in_ref, OFF_C + j * POOL_OUT_GROUP, POOL_OUT_GROUP)
            h = _proj(xb, w_in_ref, OFF_H + j * POOL_OUT_GROUP, POOL_OUT_GROUP)
            u = c_g * h
            u_ext = jnp.concatenate([u_prev[:, csl], u], axis=0)
            u_tails.append(u[sub - CONV_HALO:, :])
            cw = conv_w_ref[:, csl]
            conv = (cw[0:1] * _shift_rows(u_ext, 2, CONV_HALO)
                    + cw[1:2] * _shift_rows(u_ext, 1, CONV_HALO)
                    + cw[2:3] * u)
            b_g = _proj(xb, w_in_ref, OFF_B + j * POOL_OUT_GROUP, POOL_OUT_GROUP)
            y_a = b_g * conv

            psl = slice(j * POOL_GROUP, (j + 1) * POOL_GROUP)
            w = POOL_WINDOWS[j]
            acc = p_ext[:, psl]
            span = 1
            while span < w:
                acc = acc + pltpu.roll(acc, span, 0)
                span *= 2
            inv_cnt = 1.0 / jnp.minimum(pos + 1, w).astype(jnp.float32)
            q = acc[POOL_HALO:] * inv_cnt - p[:, psl]
            y_p = jnp.dot(q.astype(jnp.bfloat16), pool_w_ref[j],
                          preferred_element_type=jnp.float32) * pool_scale_ref[:, csl]

            g_a = _proj(xb, w_in_ref, OFF_GA + j * POOL_OUT_GROUP, POOL_OUT_GROUP)
            g_p = _proj(xb, w_in_ref, OFF_GP + j * POOL_OUT_GROUP, POOL_OUT_GROUP)
            m = jax.nn.sigmoid(g_a) * y_a + jax.nn.sigmoid(g_p) * y_p
            m_buf[rows, csl] = m.astype(jnp.bfloat16)

        out = jnp.dot(m_buf[rows, :], w_out_ref[...], preferred_element_type=jnp.float32)
        o_ref[0, rows, :] = _layer_norm(ALPHA * x + out, g_ref[...], b_ref[...])
        u_prev = jnp.concatenate(u_tails, axis=1)
        p_prev = p_ext[sub:, :]

    u_carry[...] = u_prev
    p_carry[...] = p_prev
    conv_out_ref[0] = u_carry[CONV_HALO - (CONV_WIDTH - 1):CONV_HALO, :]
    pool_out_ref[0] = p_carry[POOL_HALO - POOL_STATE:POOL_HALO, :]


def _mixer_prompt(x, w_in, conv_w, pool_w, pool_scale, w_out, g, b, tm, cast=()):
    bsz, seq, _ = x.shape
    tiles = seq // tm
    cast_in, cast_out, cast_shapes = _cast_specs(cast, bsz * tiles, lambda s, t: s * tiles + t)
    kern = functools.partial(_mixer_prompt_kernel, sub=min(tm, SUB_ROWS), n_cast=len(cast))
    outs = pl.pallas_call(
        kern,
        out_shape=[
            jax.ShapeDtypeStruct((bsz, seq, D_MODEL), jnp.float32),
            jax.ShapeDtypeStruct((bsz, CONV_WIDTH - 1, D_CONV), jnp.float32),
            jax.ShapeDtypeStruct((bsz, POOL_STATE, D_POOL), jnp.float32),
        ] + cast_shapes,
        grid=(bsz, tiles),
        in_specs=[
            pl.BlockSpec((1, tm, D_MODEL), lambda s, t: (s, t, 0)),
            _const_spec((D_MODEL, D_IN)),
            _const_spec((CONV_WIDTH, D_CONV)),
            _const_spec((N_POOL_GROUPS, POOL_GROUP, POOL_OUT_GROUP)),
            _const_spec((1, D_MODEL)),
            _const_spec((D_MODEL, D_MODEL)),
            _const_spec((1, D_MODEL)),
            _const_spec((1, D_MODEL)),
        ] + cast_in,
        out_specs=[
            pl.BlockSpec((1, tm, D_MODEL), lambda s, t: (s, t, 0)),
            pl.BlockSpec((1, CONV_WIDTH - 1, D_CONV), lambda s, t: (s, 0, 0)),
            pl.BlockSpec((1, POOL_STATE, D_POOL), lambda s, t: (s, 0, 0)),
        ] + cast_out,
        scratch_shapes=[
            pltpu.VMEM((CONV_HALO, D_CONV), jnp.float32),
            pltpu.VMEM((POOL_HALO, D_POOL), jnp.float32),
            pltpu.VMEM((tm, D_MODEL), jnp.bfloat16),
        ],
        compiler_params=pltpu.CompilerParams(
            dimension_semantics=("arbitrary", "arbitrary"), vmem_limit_bytes=VMEM_LIMIT_BYTES),
        name="mixer_prompt",
    )(x, w_in, conv_w, pool_w, pool_scale, w_out, g, b, *cast)
    return outs[0], outs[1], outs[2], outs[3:]


def _mixer_sample_kernel(x_ref, cs_ref, ps_ref, w_in_ref, conv_w_ref, pool_w_ref, pool_scale_ref,
                         w_out_ref, g_ref, b_ref, o_ref, conv_out_ref, pool_out_ref, m_buf):
    x = x_ref[...]
    xb = x.astype(jnp.bfloat16)
    p = _proj(xb, w_in_ref, OFF_P, D_POOL)

    for j in range(N_POOL_GROUPS):
        csl = slice(j * POOL_OUT_GROUP, (j + 1) * POOL_OUT_GROUP)
        c_g = _proj(xb, w_in_ref, OFF_C + j * POOL_OUT_GROUP, POOL_OUT_GROUP)
        h = _proj(xb, w_in_ref, OFF_H + j * POOL_OUT_GROUP, POOL_OUT_GROUP)
        u = c_g * h
        conv_out_ref[:, 1:2, csl] = u[:, None, :]
        cw = conv_w_ref[:, csl]
        conv = cw[0:1] * cs_ref[:, 0, csl] + cw[1:2] * cs_ref[:, 1, csl] + cw[2:3] * u
        b_g = _proj(xb, w_in_ref, OFF_B + j * POOL_OUT_GROUP, POOL_OUT_GROUP)
        y_a = b_g * conv

        w = POOL_WINDOWS[j]
        psl = slice(j * POOL_GROUP, (j + 1) * POOL_GROUP)
        pj = p[:, psl]
        wsum = pj + jnp.sum(ps_ref[:, POOL_STATE - (w - 1):POOL_STATE, psl], axis=1)
        q = wsum * (1.0 / w) - pj
        y_p = jnp.dot(q.astype(jnp.bfloat16), pool_w_ref[j],
                      preferred_element_type=jnp.float32) * pool_scale_ref[:, csl]

        g_a = _proj(xb, w_in_ref, OFF_GA + j * POOL_OUT_GROUP, POOL_OUT_GROUP)
        g_p = _proj(xb, w_in_ref, OFF_GP + j * POOL_OUT_GROUP, POOL_OUT_GROUP)
        m = jax.nn.sigmoid(g_a) * y_a + jax.nn.sigmoid(g_p) * y_p
        m_buf[:, csl] = m.astype(jnp.bfloat16)

    out = jnp.dot(m_buf[...], w_out_ref[...], preferred_element_type=jnp.float32)
    o_ref[...] = _layer_norm(ALPHA * x + out, g_ref[...], b_ref[...])

    conv_out_ref[:, 0:CONV_WIDTH - 2, :] = cs_ref[:, 1:CONV_WIDTH - 1, :]
    pool_out_ref[:, 0:POOL_STATE - 1, :] = ps_ref[:, 1:POOL_STATE, :]
    pool_out_ref[:, POOL_STATE - 1:POOL_STATE, :] = p[:, None, :]


def _mixer_sample(x2d, cs, ps, w_in, conv_w, pool_w, pool_scale, w_out, g, b):
    nb = x2d.shape[0]
    shapes = [x2d.shape, cs.shape, ps.shape, w_in.shape, conv_w.shape, pool_w.shape,
              pool_scale.shape, w_out.shape, g.shape, b.shape]
    return pl.pallas_call(
        _mixer_sample_kernel,
        out_shape=(
            jax.ShapeDtypeStruct((nb, D_MODEL), jnp.float32),
            jax.ShapeDtypeStruct(cs.shape, jnp.float32),
            jax.ShapeDtypeStruct(ps.shape, jnp.float32),
        ),
        grid=(1,),
        in_specs=[_const_spec(s) for s in shapes],
        out_specs=(
            pl.BlockSpec((nb, D_MODEL), lambda i: (0, 0)),
            pl.BlockSpec(cs.shape, lambda i: (0, 0, 0)),
            pl.BlockSpec(ps.shape, lambda i: (0, 0, 0)),
        ),
        scratch_shapes=[pltpu.VMEM((nb, D_MODEL), jnp.bfloat16)],
        compiler_params=pltpu.CompilerParams(
            dimension_semantics=("arbitrary",), vmem_limit_bytes=VMEM_LIMIT_BYTES),
        name="mixer_sample",
    )(x2d, cs, ps, w_in, conv_w, pool_w, pool_scale, w_out, g, b)


def kernel(x_prompt, x_sample, state_conv, state_pool, ln1_g, ln1_b, ffn1_wg, ffn1_wu, ffn1_wd,
           w_in, conv_w, pool_w, pool_scale, w_out, ln2_g, ln2_b,
           ffn2_wg, ffn2_wu, ffn2_wd, ln3_g, ln3_b):
    bsz, seq, _ = x_prompt.shape
    nb = x_sample.shape[0]
    assert x_sample.shape[1] == 1 and DEPTH == 1
    bf = jnp.bfloat16
    tm = 1024
    tm_ffn = 1024

    yp = x_prompt.reshape(bsz * seq, D_MODEL)
    ys = x_sample.reshape(nb, D_MODEL)
    l = 0
    wg1, wu1, wd1 = ffn1_wg[l].astype(bf), ffn1_wu[l].astype(bf), ffn1_wd[l].astype(bf)
    g1, b1 = ln1_g[l][None], ln1_b[l][None]
    g2, b2 = ln2_g[l][None], ln2_b[l][None]
    g3, b3 = ln3_g[l][None], ln3_b[l][None]
    scale = pool_scale[l][None]

    pool_w2d = pool_w[l].reshape(N_POOL_GROUPS * POOL_GROUP, POOL_OUT_GROUP)
    yp, ys, (w_in_b, w_out_b, pool_w_b) = _ffn(yp, ys, wg1, wu1, wd1, g1, b1, tm_ffn,
                                               cast=(w_in[l], w_out[l], pool_w2d))
    pool_w_b = pool_w_b.reshape(N_POOL_GROUPS, POOL_GROUP, POOL_OUT_GROUP)
    yp, conv_p, pool_p, (wg2, wu2, wd2) = _mixer_prompt(
        yp.reshape(bsz, seq, D_MODEL), w_in_b, conv_w[l], pool_w_b, scale, w_out_b, g2, b2, tm,
        cast=(ffn2_wg[l], ffn2_wu[l], ffn2_wd[l]))
    ys, conv_s, pool_s = _mixer_sample(ys, state_conv[l], state_pool[l], w_in_b, conv_w[l],
                                       pool_w_b, scale, w_out_b, g2, b2)
    yp, ys, _ = _ffn(yp.reshape(bsz * seq, D_MODEL), ys, wg2, wu2, wd2, g3, b3, tm_ffn)

    return (yp.reshape(bsz, seq, D_MODEL),
            ys.reshape(nb, 1, D_MODEL),
            conv_p[None],
            pool_p[None],
            conv_s[None],
            pool_s[None])
```

```python
import functools

import jax
import jax.numpy as jnp
from jax import lax
from jax.experimental import pallas as pl
from jax.experimental.pallas import tpu as pltpu

D_MODEL = 1024
D_CONV = D_MODEL
D_POOL = D_MODEL // 2
N_POOL_GROUPS = 4
POOL_GROUP = D_POOL // N_POOL_GROUPS
POOL_OUT_GROUP = D_MODEL // N_POOL_GROUPS
POOL_WINDOWS = (2, 4, 8, 16)
POOL_STATE = max(POOL_WINDOWS) - 1
CONV_WIDTH = 3
D_FF = 2816
DEPTH = 1
ALPHA = (2.0 * DEPTH) ** 0.25
LN_EPS = 1e-5
D_IN = 3 * D_CONV + D_POOL + 2 * D_MODEL

OFF_C = 0
OFF_B = D_CONV
OFF_H = 2 * D_CONV
OFF_P = 3 * D_CONV
OFF_GA = 3 * D_CONV + D_POOL
OFF_GP = 3 * D_CONV + D_POOL + D_MODEL

SUB_ROWS = 256
FF_CHUNK = 256
POOL_HALO = 16
CONV_HALO = 8
VMEM_LIMIT_BYTES = 52 * 1024 * 1024


def _layer_norm(v, g, b):
    mu = jnp.mean(v, axis=-1, keepdims=True)
    d = v - mu
    var = jnp.mean(d * d, axis=-1, keepdims=True)
    return d * lax.rsqrt(var + LN_EPS) * g + b


def _const_spec(shape):
    zeros = (0,) * len(shape)
    return pl.BlockSpec(shape, lambda *_: zeros, pipeline_mode=pl.Buffered(1))


def _cast_specs(arrays, n_steps, step_index):
    in_specs, out_specs, out_shapes = [], [], []
    for a in arrays:
        rows, cols = a.shape
        assert rows % (n_steps * 16) == 0
        spec = pl.BlockSpec((rows // n_steps, cols), lambda *idx: (step_index(*idx), 0))
        in_specs.append(spec)
        out_specs.append(spec)
        out_shapes.append(jax.ShapeDtypeStruct(a.shape, jnp.bfloat16))
    return in_specs, out_specs, out_shapes


def _run_casts(src_refs, dst_refs):
    for src, dst in zip(src_refs, dst_refs):
        dst[...] = src[...].astype(jnp.bfloat16)


def _ffn_rows(x, wg_ref, wu_ref, wd_ref, g_ref, b_ref, h_view):
    xb = x.astype(jnp.bfloat16)
    for c in range(D_FF // FF_CHUNK):
        sl = slice(c * FF_CHUNK, (c + 1) * FF_CHUNK)
        gate = jnp.dot(xb, wg_ref[:, sl], preferred_element_type=jnp.float32)
        up = jnp.dot(xb, wu_ref[:, sl], preferred_element_type=jnp.float32)
        h_view[:, sl] = (gate * jax.nn.sigmoid(gate) * up).astype(jnp.bfloat16)
    y = jnp.dot(h_view[...], wd_ref[...], preferred_element_type=jnp.float32)
    return _layer_norm(ALPHA * x + 0.5 * y, g_ref[...], b_ref[...])


def _ffn_kernel(xp_ref, xs_ref, wg_ref, wu_ref, wd_ref, g_ref, b_ref, *rest, sub, n_cast):
    cast_src, (op_ref, os_ref, *cast_dst), h_ref = rest[:n_cast], rest[n_cast:2 * n_cast + 2], rest[-1]
    i = pl.program_id(0)
    last = pl.num_programs(0) - 1
    weights = (wg_ref, wu_ref, wd_ref, g_ref, b_ref)

    @pl.when(i < last)
    def _():
        _run_casts(cast_src, cast_dst)
        for s in range(xp_ref.shape[0] // sub):
            rows = pl.ds(s * sub, sub)
            op_ref[rows, :] = _ffn_rows(xp_ref[rows, :], *weights, h_ref.at[rows, :])

    @pl.when(i == last)
    def _():
        ns = xs_ref.shape[0]
        xs = xs_ref[...] if len(xs_ref.shape) == 2 else xs_ref[:, 0, :]
        ys = _ffn_rows(xs, *weights, h_ref.at[pl.ds(0, ns), :])
        os_ref[...] = ys if len(os_ref.shape) == 2 else ys[:, None, :]


def _ffn(xp, xs, wg, wu, wd, g, b, tm, cast=(), sample_out_shape=None):
    n, ns = xp.shape[0], xs.shape[0]
    nt = n // tm
    assert ns <= tm
    sample_out_shape = sample_out_shape or (ns, D_MODEL)
    sample_zeros = (0,) * len(sample_out_shape)
    blk = lambda i: (jnp.minimum(i, nt - 1), 0)
    cast_in, cast_out, cast_shapes = _cast_specs(cast, nt, lambda i: jnp.minimum(i, nt - 1))
    outs = pl.pallas_call(
        functools.partial(_ffn_kernel, sub=min(tm, SUB_ROWS), n_cast=len(cast)),
        out_shape=[jax.ShapeDtypeStruct((n, D_MODEL), jnp.float32),
                   jax.ShapeDtypeStruct(sample_out_shape, jnp.float32)] + cast_shapes,
        grid=(nt + 1,),
        in_specs=[
            pl.BlockSpec((tm, D_MODEL), blk),
            _const_spec(xs.shape),
            _const_spec((D_MODEL, D_FF)),
            _const_spec((D_MODEL, D_FF)),
            _const_spec((D_FF, D_MODEL)),
            _const_spec((1, D_MODEL)),
            _const_spec((1, D_MODEL)),
        ] + cast_in,
        out_specs=[pl.BlockSpec((tm, D_MODEL), blk),
                   pl.BlockSpec(sample_out_shape, lambda i: sample_zeros)] + cast_out,
        scratch_shapes=[pltpu.VMEM((tm, D_FF), jnp.bfloat16)],
        compiler_params=pltpu.CompilerParams(
            dimension_semantics=("arbitrary",), vmem_limit_bytes=VMEM_LIMIT_BYTES),
        name="ffn",
    )(xp, xs, wg, wu, wd, g, b, *cast)
    return outs[0], outs[1], outs[2:]


def _proj(xb, w_in_ref, off, width):
    return jnp.dot(xb, w_in_ref[:, off:off + width], preferred_element_type=jnp.float32)


def _shift_rows(ext, k, halo):
    return pltpu.roll(ext, k, 0)[halo:]


def _mixer_prompt_kernel(x_ref, w_in_ref, conv_w_ref, pool_w_ref, pool_scale_ref, w_out_ref,
                         g_ref, b_ref, *rest, sub, n_cast):
    cast_src = rest[:n_cast]
    o_ref, conv_out_ref, pool_out_ref, *cast_dst = rest[n_cast:2 * n_cast + 3]
    u_carry, p_carry, m_buf = rest[2 * n_cast + 3:]
    _run_casts(cast_src, cast_dst)
    t = pl.program_id(1)
    tm = x_ref.shape[1]

    @pl.when(t == 0)
    def _():
        u_carry[...] = jnp.zeros(u_carry.shape, jnp.float32)
        p_carry[...] = jnp.zeros(p_carry.shape, jnp.float32)

    u_prev = u_carry[...]
    p_prev = p_carry[...]

    for s in range(tm // sub):
        rows = slice(s * sub, (s + 1) * sub)
        x = x_ref[0, rows, :]
        xb = x.astype(jnp.bfloat16)

        p = _proj(xb, w_in_ref, OFF_P, D_POOL)
        p_ext = jnp.concatenate([p_prev, p], axis=0)
        pos = t * tm + s * sub + lax.broadcasted_iota(jnp.int32, (sub, POOL_GROUP), 0)

        u_tails = []
        for j in range(N_POOL_GROUPS):
            csl = slice(j * POOL_OUT_GROUP, (j + 1) * POOL_OUT_GROUP)
            c_g = _proj(xb, w_in_ref, OFF_C + j * POOL_OUT_GROUP, POOL_OUT_GROUP)
            h = _proj(xb, w_in_ref, OFF_H + j * POOL_OUT_GROUP, POOL_OUT_GROUP)
            u = c_g * h
            u_ext = jnp.concatenate([u_prev[:, csl], u], axis=0)
            u_tails.append(u[sub - CONV_HALO:, :])
            cw = conv_w_ref[:, csl]
            conv = (cw[0:1] * _shift_rows(u_ext, 2, CONV_HALO)
                    + cw[1:2] * _shift_rows(u_ext, 1, CONV_HALO)
                    + cw[2:3] * u)
            b_g = _proj(xb, w_in_ref, OFF_B + j * POOL_OUT_GROUP, POOL_OUT_GROUP)
            y_a = b_g * conv

            psl = slice(j * POOL_GROUP, (j + 1) * POOL_GROUP)
            w = POOL_WINDOWS[j]
            acc = p_ext[:, psl]
            span = 1
            while span < w:
                acc = acc + pltpu.roll(acc, span, 0)
                span *= 2
            inv_cnt = 1.0 / jnp.minimum(pos + 1, w).astype(jnp.float32)
            q = acc[POOL_HALO:] * inv_cnt - p[:, psl]
            y_p = jnp.dot(q.astype(jnp.bfloat16), pool_w_ref[j],
                          preferred_element_type=jnp.float32) * pool_scale_ref[:, csl]

            g_a = _proj(xb, w_in_ref, OFF_GA + j * POOL_OUT_GROUP, POOL_OUT_GROUP)
            g_p = _proj(xb, w_in_ref, OFF_GP + j * POOL_OUT_GROUP, POOL_OUT_GROUP)
            m = jax.nn.sigmoid(g_a) * y_a + jax.nn.sigmoid(g_p) * y_p
            m_buf[rows, csl] = m.astype(jnp.bfloat16)

        out = jnp.dot(m_buf[rows, :], w_out_ref[...], preferred_element_type=jnp.float32)
        o_ref[0, rows, :] = _layer_norm(ALPHA * x + out, g_ref[...], b_ref[...])
        u_prev = jnp.concatenate(u_tails, axis=1)
        p_prev = p_ext[sub:, :]

    u_carry[...] = u_prev
    p_carry[...] = p_prev
    conv_out_ref[0] = u_carry[CONV_HALO - (CONV_WIDTH - 1):CONV_HALO, :]
    pool_out_ref[0] = p_carry[POOL_HALO - POOL_STATE:POOL_HALO, :]


def _mixer_prompt(x, w_in, conv_w, pool_w, pool_scale, w_out, g, b, tm, cast=()):
    bsz, seq, _ = x.shape
    tiles = seq // tm
    cast_in, cast_out, cast_shapes = _cast_specs(cast, bsz * tiles, lambda s, t: s * tiles + t)
    kern = functools.partial(_mixer_prompt_kernel, sub=min(tm, SUB_ROWS), n_cast=len(cast))
    outs = pl.pallas_call(
        kern,
        out_shape=[
            jax.ShapeDtypeStruct((bsz, seq, D_MODEL), jnp.float32),
            jax.ShapeDtypeStruct((bsz, CONV_WIDTH - 1, D_CONV), jnp.float32),
            jax.ShapeDtypeStruct((bsz, POOL_STATE, D_POOL), jnp.float32),
        ] + cast_shapes,
        grid=(bsz, tiles),
        in_specs=[
            pl.BlockSpec((1, tm, D_MODEL), lambda s, t: (s, t, 0)),
            _const_spec((D_MODEL, D_IN)),
            _const_spec((CONV_WIDTH, D_CONV)),
            _const_spec((N_POOL_GROUPS, POOL_GROUP, POOL_OUT_GROUP)),
            _const_spec((1, D_MODEL)),
            _const_spec((D_MODEL, D_MODEL)),
            _const_spec((1, D_MODEL)),
            _const_spec((1, D_MODEL)),
        ] + cast_in,
        out_specs=[
            pl.BlockSpec((1, tm, D_MODEL), lambda s, t: (s, t, 0)),
            pl.BlockSpec((1, CONV_WIDTH - 1, D_CONV), lambda s, t: (s, 0, 0)),
            pl.BlockSpec((1, POOL_STATE, D_POOL), lambda s, t: (s, 0, 0)),
        ] + cast_out,
        scratch_shapes=[
            pltpu.VMEM((CONV_HALO, D_CONV), jnp.float32),
            pltpu.VMEM((POOL_HALO, D_POOL), jnp.float32),
            pltpu.VMEM((tm, D_MODEL), jnp.bfloat16),
        ],
        compiler_params=pltpu.CompilerParams(
            dimension_semantics=("arbitrary", "arbitrary"), vmem_limit_bytes=VMEM_LIMIT_BYTES),
        name="mixer_prompt",
    )(x, w_in, conv_w, pool_w, pool_scale, w_out, g, b, *cast)
    return outs[0], outs[1], outs[2], outs[3:]


def _mixer_sample_kernel(x_ref, cs_ref, ps_ref, w_in_ref, conv_w_ref, pool_w_ref, pool_scale_ref,
                         w_out_ref, g_ref, b_ref, o_ref, conv_out_ref, pool_out_ref, m_buf):
    x = x_ref[...]
    xb = x.astype(jnp.bfloat16)
    p = _proj(xb, w_in_ref, OFF_P, D_POOL)

    for j in range(N_POOL_GROUPS):
        csl = slice(j * POOL_OUT_GROUP, (j + 1) * POOL_OUT_GROUP)
        c_g = _proj(xb, w_in_ref, OFF_C + j * POOL_OUT_GROUP, POOL_OUT_GROUP)
        h = _proj(xb, w_in_ref, OFF_H + j * POOL_OUT_GROUP, POOL_OUT_GROUP)
        u = c_g * h
        conv_out_ref[:, CONV_WIDTH - 2:CONV_WIDTH - 1, csl] = u[:, None, :]
        cw = conv_w_ref[:, csl]
        conv = cw[0:1] * cs_ref[:, 0, csl] + cw[1:2] * cs_ref[:, 1, csl] + cw[2:3] * u
        b_g = _proj(xb, w_in_ref, OFF_B + j * POOL_OUT_GROUP, POOL_OUT_GROUP)
        y_a = b_g * conv

        w = POOL_WINDOWS[j]
        psl = slice(j * POOL_GROUP, (j + 1) * POOL_GROUP)
        pj = p[:, psl]
        wsum = pj
        for k in range(POOL_STATE - (w - 1), POOL_STATE):
            wsum = wsum + ps_ref[k, :, psl]
        q = wsum * (1.0 / w) - pj
        y_p = jnp.dot(q.astype(jnp.bfloat16), pool_w_ref[j],
                      preferred_element_type=jnp.float32) * pool_scale_ref[:, csl]

        g_a = _proj(xb, w_in_ref, OFF_GA + j * POOL_OUT_GROUP, POOL_OUT_GROUP)
        g_p = _proj(xb, w_in_ref, OFF_GP + j * POOL_OUT_GROUP, POOL_OUT_GROUP)
        m = jax.nn.sigmoid(g_a) * y_a + jax.nn.sigmoid(g_p) * y_p
        m_buf[:, csl] = m.astype(jnp.bfloat16)

    out = jnp.dot(m_buf[...], w_out_ref[...], preferred_element_type=jnp.float32)
    o_ref[...] = _layer_norm(ALPHA * x + out, g_ref[...], b_ref[...])

    conv_out_ref[:, 0:CONV_WIDTH - 2, :] = cs_ref[:, 1:CONV_WIDTH - 1, :]
    pool_out_ref[0:POOL_STATE - 1] = ps_ref[1:POOL_STATE]
    pool_out_ref[POOL_STATE - 1] = p


def _mixer_sample(x2d, cs, ps, w_in, conv_w, pool_w, pool_scale, w_out, g, b):
    nb = x2d.shape[0]
    shapes = [x2d.shape, cs.shape, ps.shape, w_in.shape, conv_w.shape, pool_w.shape,
              pool_scale.shape, w_out.shape, g.shape, b.shape]
    return pl.pallas_call(
        _mixer_sample_kernel,
        out_shape=(
            jax.ShapeDtypeStruct((nb, D_MODEL), jnp.float32),
            jax.ShapeDtypeStruct(cs.shape, jnp.float32),
            jax.ShapeDtypeStruct(ps.shape, jnp.float32),
        ),
        grid=(1,),
        in_specs=[_const_spec(s) for s in shapes],
        out_specs=(
            pl.BlockSpec((nb, D_MODEL), lambda i: (0, 0)),
            pl.BlockSpec(cs.shape, lambda i: (0, 0, 0)),
            pl.BlockSpec(ps.shape, lambda i: (0, 0, 0)),
        ),
        scratch_shapes=[pltpu.VMEM((nb, D_MODEL), jnp.bfloat16)],
        compiler_params=pltpu.CompilerParams(
            dimension_semantics=("arbitrary",), vmem_limit_bytes=VMEM_LIMIT_BYTES),
        name="mixer_sample",
    )(x2d, cs, ps, w_in, conv_w, pool_w, pool_scale, w_out, g, b)


def kernel(x_prompt, x_sample, state_conv, state_pool, ln1_g, ln1_b, ffn1_wg, ffn1_wu, ffn1_wd,
           w_in, conv_w, pool_w, pool_scale, w_out, ln2_g, ln2_b,
           ffn2_wg, ffn2_wu, ffn2_wd, ln3_g, ln3_b):
    bsz, seq, _ = x_prompt.shape
    nb = x_sample.shape[0]
    assert x_sample.shape[1] == 1 and DEPTH == 1
    bf = jnp.bfloat16
    tm = 1024
    tm_ffn = 1024

    yp = x_prompt.reshape(bsz * seq, D_MODEL)
    l = 0
    wg1, wu1, wd1 = ffn1_wg[l].astype(bf), ffn1_wu[l].astype(bf), ffn1_wd[l].astype(bf)
    g1, b1 = ln1_g[l][None], ln1_b[l][None]
    g2, b2 = ln2_g[l][None], ln2_b[l][None]
    g3, b3 = ln3_g[l][None], ln3_b[l][None]
    scale = pool_scale[l][None]

    pool_w2d = pool_w[l].reshape(N_POOL_GROUPS * POOL_GROUP, POOL_OUT_GROUP)
    yp, ys, (w_in_b, w_out_b, pool_w_b) = _ffn(yp, x_sample, wg1, wu1, wd1, g1, b1, tm_ffn,
                                               cast=(w_in[l], w_out[l], pool_w2d))
    pool_w_b = pool_w_b.reshape(N_POOL_GROUPS, POOL_GROUP, POOL_OUT_GROUP)
    yp, conv_p, pool_p, (wg2, wu2, wd2) = _mixer_prompt(
        yp.reshape(bsz, seq, D_MODEL), w_in_b, conv_w[l], pool_w_b, scale, w_out_b, g2, b2, tm,
        cast=(ffn2_wg[l], ffn2_wu[l], ffn2_wd[l]))
    ps_rows = jnp.transpose(state_pool[l], (1, 0, 2))
    ys, conv_s, pool_s = _mixer_sample(ys, state_conv[l], ps_rows, w_in_b, conv_w[l],
                                       pool_w_b, scale, w_out_b, g2, b2)
    yp, ys, _ = _ffn(yp.reshape(bsz * seq, D_MODEL), ys, wg2, wu2, wd2, g3, b3, tm_ffn,
                     sample_out_shape=x_sample.shape)

    return (yp.reshape(bsz, seq, D_MODEL),
            ys,
            conv_p[None],
            pool_p[None],
            conv_s[None],
            jnp.transpose(pool_s, (1, 0, 2))[None])
```

```python
import functools

import jax
import jax.numpy as jnp
from jax import lax
from jax.experimental import pallas as pl
from jax.experimental.pallas import tpu as pltpu

D_MODEL = 1024
D_CONV = D_MODEL
D_POOL = D_MODEL // 2
N_POOL_GROUPS = 4
POOL_GROUP = D_POOL // N_POOL_GROUPS
POOL_OUT_GROUP = D_MODEL // N_POOL_GROUPS
POOL_WINDOWS = (2, 4, 8, 16)
POOL_STATE = max(POOL_WINDOWS) - 1
CONV_WIDTH = 3
D_FF = 2816
DEPTH = 1
ALPHA = (2.0 * DEPTH) ** 0.25
LN_EPS = 1e-5
D_IN = 3 * D_CONV + D_POOL + 2 * D_MODEL

OFF_C = 0
OFF_B = D_CONV
OFF_H = 2 * D_CONV
OFF_P = 3 * D_CONV
OFF_GA = 3 * D_CONV + D_POOL
OFF_GP = 3 * D_CONV + D_POOL + D_MODEL

SUB_ROWS = 256
FF_CHUNK = 256
POOL_HALO = 16
CONV_HALO = 8
VMEM_LIMIT_BYTES = 52 * 1024 * 1024


def _layer_norm(v, g, b):
    mu = jnp.mean(v, axis=-1, keepdims=True)
    d = v - mu
    var = jnp.mean(d * d, axis=-1, keepdims=True)
    return d * lax.rsqrt(var + LN_EPS) * g + b


def _const_spec(shape):
    zeros = (0,) * len(shape)
    return pl.BlockSpec(shape, lambda *_: zeros, pipeline_mode=pl.Buffered(1))


def _cast_specs(arrays, n_steps, step_index):
    in_specs, out_specs, out_shapes = [], [], []
    for a in arrays:
        rows, cols = a.shape
        assert rows % (n_steps * 16) == 0
        spec = pl.BlockSpec((rows // n_steps, cols), lambda *idx: (step_index(*idx), 0))
        in_specs.append(spec)
        out_specs.append(spec)
        out_shapes.append(jax.ShapeDtypeStruct(a.shape, jnp.bfloat16))
    return in_specs, out_specs, out_shapes


def _run_casts(src_refs, dst_refs):
    for src, dst in zip(src_refs, dst_refs):
        dst[...] = src[...].astype(jnp.bfloat16)


def _ffn_rows(x, wg_ref, wu_ref, wd_ref, g_ref, b_ref, h_view):
    xb = x.astype(jnp.bfloat16)
    for c in range(D_FF // FF_CHUNK):
        sl = slice(c * FF_CHUNK, (c + 1) * FF_CHUNK)
        gate = jnp.dot(xb, wg_ref[:, sl], preferred_element_type=jnp.float32)
        up = jnp.dot(xb, wu_ref[:, sl], preferred_element_type=jnp.float32)
        h_view[:, sl] = (gate * jax.nn.sigmoid(gate) * up).astype(jnp.bfloat16)
    y = jnp.dot(h_view[...], wd_ref[...], preferred_element_type=jnp.float32)
    return _layer_norm(ALPHA * x + 0.5 * y, g_ref[...], b_ref[...])


def _ffn_kernel(xp_ref, xs_ref, wg_ref, wu_ref, wd_ref, g_ref, b_ref, *rest, sub, n_cast):
    cast_src, (op_ref, os_ref, *cast_dst), h_ref = rest[:n_cast], rest[n_cast:2 * n_cast + 2], rest[-1]
    i = pl.program_id(0)
    last = pl.num_programs(0) - 1
    weights = (wg_ref, wu_ref, wd_ref, g_ref, b_ref)

    @pl.when(i < last)
    def _():
        for s in range(xp_ref.shape[0] // sub):
            rows = pl.ds(s * sub, sub)
            op_ref[rows, :] = _ffn_rows(xp_ref[rows, :], *weights, h_ref.at[rows, :])
        _run_casts(cast_src, cast_dst)

    @pl.when(i == last)
    def _():
        ns = xs_ref.shape[0]
        xs = xs_ref[...] if len(xs_ref.shape) == 2 else xs_ref[:, 0, :]
        ys = _ffn_rows(xs, *weights, h_ref.at[pl.ds(0, ns), :])
        os_ref[...] = ys if len(os_ref.shape) == 2 else ys[:, None, :]


def _ffn(xp, xs, wg, wu, wd, g, b, tm, cast=(), sample_out_shape=None):
    n, ns = xp.shape[0], xs.shape[0]
    nt = n // tm
    assert ns <= tm
    sample_out_shape = sample_out_shape or (ns, D_MODEL)
    sample_zeros = (0,) * len(sample_out_shape)
    blk = lambda i: (jnp.minimum(i, nt - 1), 0)
    cast_in, cast_out, cast_shapes = _cast_specs(cast, nt, lambda i: jnp.minimum(i, nt - 1))
    outs = pl.pallas_call(
        functools.partial(_ffn_kernel, sub=min(tm, SUB_ROWS), n_cast=len(cast)),
        out_shape=[jax.ShapeDtypeStruct((n, D_MODEL), jnp.float32),
                   jax.ShapeDtypeStruct(sample_out_shape, jnp.float32)] + cast_shapes,
        grid=(nt + 1,),
        in_specs=[
            pl.BlockSpec((tm, D_MODEL), blk),
            _const_spec(xs.shape),
            _const_spec((D_MODEL, D_FF)),
            _const_spec((D_MODEL, D_FF)),
            _const_spec((D_FF, D_MODEL)),
            _const_spec((1, D_MODEL)),
            _const_spec((1, D_MODEL)),
        ] + cast_in,
        out_specs=[pl.BlockSpec((tm, D_MODEL), blk),
                   pl.BlockSpec(sample_out_shape, lambda i: sample_zeros)] + cast_out,
        scratch_shapes=[pltpu.VMEM((tm, D_FF), jnp.bfloat16)],
        compiler_params=pltpu.CompilerParams(
            dimension_semantics=("arbitrary",), vmem_limit_bytes=VMEM_LIMIT_BYTES),
        name="ffn",
    )(xp, xs, wg, wu, wd, g, b, *cast)
    return outs[0], outs[1], outs[2:]


def _proj(xb, w_in_ref, off, width):
    return jnp.dot(xb, w_in_ref[:, off:off + width], preferred_element_type=jnp.float32)


def _shift_rows(ext, k, halo):
    return pltpu.roll(ext, k, 0)[halo:]


def _mixer_prompt_kernel(x_ref, w_in_ref, conv_w_ref, pool_w_ref, pool_scale_ref, w_out_ref,
                         g_ref, b_ref, *rest, sub, n_cast):
    cast_src = rest[:n_cast]
    o_ref, conv_out_ref, pool_out_ref, *cast_dst = rest[n_cast:2 * n_cast + 3]
    u_carry, p_carry, m_buf = rest[2 * n_cast + 3:]
    t = pl.program_id(1)
    tm = x_ref.shape[1]

    @pl.when(t == 0)
    def _():
        u_carry[...] = jnp.zeros(u_carry.shape, jnp.float32)
        p_carry[...] = jnp.zeros(p_carry.shape, jnp.float32)

    u_prev = u_carry[...]
    p_prev = p_carry[...]

    def project_out_and_norm(rows):
        out = jnp.dot(m_buf[rows, :], w_out_ref[...], preferred_element_type=jnp.float32)
        o_ref[0, rows, :] = _layer_norm(ALPHA * x_ref[0, rows, :] + out, g_ref[...], b_ref[...])

    pending_rows = None
    for s in range(tm // sub):
        rows = slice(s * sub, (s + 1) * sub)
        x = x_ref[0, rows, :]
        xb = x.astype(jnp.bfloat16)

        p = _proj(xb, w_in_ref, OFF_P, D_POOL)
        p_ext = jnp.concatenate([p_prev, p], axis=0)
        pos = t * tm + s * sub + lax.broadcasted_iota(jnp.int32, (sub, POOL_GROUP), 0)

        u_tails = []
        for j in range(N_POOL_GROUPS):
            csl = slice(j * POOL_OUT_GROUP, (j + 1) * POOL_OUT_GROUP)
            c_g = _proj(xb, w_in_ref, OFF_C + j * POOL_OUT_GROUP, POOL_OUT_GROUP)
            h = _proj(xb, w_in_ref, OFF_H + j * POOL_OUT_GROUP, POOL_OUT_GROUP)
            u = c_g * h
            u_ext = jnp.concatenate([u_prev[:, csl], u], axis=0)
            u_tails.append(u[sub - CONV_HALO:, :])
            cw = conv_w_ref[:, csl]
            conv = (cw[0:1] * _shift_rows(u_ext, 2, CONV_HALO)
                    + cw[1:2] * _shift_rows(u_ext, 1, CONV_HALO)
                    + cw[2:3] * u)
            b_g = _proj(xb, w_in_ref, OFF_B + j * POOL_OUT_GROUP, POOL_OUT_GROUP)
            y_a = b_g * conv

            psl = slice(j * POOL_GROUP, (j + 1) * POOL_GROUP)
            w = POOL_WINDOWS[j]
            acc = p_ext[:, psl]
            span = 1
            while span < w:
                acc = acc + pltpu.roll(acc, span, 0)
                span *= 2
            inv_cnt = 1.0 / jnp.minimum(pos + 1, w).astype(jnp.float32)
            q = acc[POOL_HALO:] * inv_cnt - p[:, psl]
            y_p = jnp.dot(q.astype(jnp.bfloat16), pool_w_ref[j],
                          preferred_element_type=jnp.float32) * pool_scale_ref[:, csl]

            g_a = _proj(xb, w_in_ref, OFF_GA + j * POOL_OUT_GROUP, POOL_OUT_GROUP)
            g_p = _proj(xb, w_in_ref, OFF_GP + j * POOL_OUT_GROUP, POOL_OUT_GROUP)
            m = jax.nn.sigmoid(g_a) * y_a + jax.nn.sigmoid(g_p) * y_p
            m_buf[rows, csl] = m.astype(jnp.bfloat16)

            if j == 0 and pending_rows is not None:
                project_out_and_norm(pending_rows)

        pending_rows = rows
        u_prev = jnp.concatenate(u_tails, axis=1)
        p_prev = p_ext[sub:, :]

    project_out_and_norm(pending_rows)
    _run_casts(cast_src, cast_dst)

    u_carry[...] = u_prev
    p_carry[...] = p_prev
    conv_out_ref[0] = u_carry[CONV_HALO - (CONV_WIDTH - 1):CONV_HALO, :]
    pool_out_ref[0] = p_carry[POOL_HALO - POOL_STATE:POOL_HALO, :]


def _mixer_prompt(x, w_in, conv_w, pool_w, pool_scale, w_out, g, b, tm, cast=()):
    bsz, seq, _ = x.shape
    tiles = seq // tm
    cast_in, cast_out, cast_shapes = _cast_specs(cast, bsz * tiles, lambda s, t: s * tiles + t)
    kern = functools.partial(_mixer_prompt_kernel, sub=min(tm, SUB_ROWS), n_cast=len(cast))
    outs = pl.pallas_call(
        kern,
        out_shape=[
            jax.ShapeDtypeStruct((bsz, seq, D_MODEL), jnp.float32),
            jax.ShapeDtypeStruct((bsz, CONV_WIDTH - 1, D_CONV), jnp.float32),
            jax.ShapeDtypeStruct((bsz, POOL_STATE, D_POOL), jnp.float32),
        ] + cast_shapes,
        grid=(bsz, tiles),
        in_specs=[
            pl.BlockSpec((1, tm, D_MODEL), lambda s, t: (s, t, 0)),
            _const_spec((D_MODEL, D_IN)),
            _const_spec((CONV_WIDTH, D_CONV)),
            _const_spec((N_POOL_GROUPS, POOL_GROUP, POOL_OUT_GROUP)),
            _const_spec((1, D_MODEL)),
            _const_spec((D_MODEL, D_MODEL)),
            _const_spec((1, D_MODEL)),
            _const_spec((1, D_MODEL)),
        ] + cast_in,
        out_specs=[
            pl.BlockSpec((1, tm, D_MODEL), lambda s, t: (s, t, 0)),
            pl.BlockSpec((1, CONV_WIDTH - 1, D_CONV), lambda s, t: (s, 0, 0)),
            pl.BlockSpec((1, POOL_STATE, D_POOL), lambda s, t: (s, 0, 0)),
        ] + cast_out,
        scratch_shapes=[
            pltpu.VMEM((CONV_HALO, D_CONV), jnp.float32),
            pltpu.VMEM((POOL_HALO, D_POOL), jnp.float32),
            pltpu.VMEM((tm, D_MODEL), jnp.bfloat16),
        ],
        compiler_params=pltpu.CompilerParams(
            dimension_semantics=("arbitrary", "arbitrary"), vmem_limit_bytes=VMEM_LIMIT_BYTES),
        name="mixer_prompt",
    )(x, w_in, conv_w, pool_w, pool_scale, w_out, g, b, *cast)
    return outs[0], outs[1], outs[2], outs[3:]


def _mixer_sample_kernel(x_ref, cs_ref, ps_ref, w_in_ref, conv_w_ref, pool_w_ref, pool_scale_ref,
                         w_out_ref, g_ref, b_ref, o_ref, conv_out_ref, pool_out_ref, m_buf):
    x = x_ref[...]
    xb = x.astype(jnp.bfloat16)
    p = _proj(xb, w_in_ref, OFF_P, D_POOL)

    for j in range(N_POOL_GROUPS):
        csl = slice(j * POOL_OUT_GROUP, (j + 1) * POOL_OUT_GROUP)
        c_g = _proj(xb, w_in_ref, OFF_C + j * POOL_OUT_GROUP, POOL_OUT_GROUP)
        h = _proj(xb, w_in_ref, OFF_H + j * POOL_OUT_GROUP, POOL_OUT_GROUP)
        u = c_g * h
        conv_out_ref[:, CONV_WIDTH - 2:CONV_WIDTH - 1, csl] = u[:, None, :]
        cw = conv_w_ref[:, csl]
        conv = cw[0:1] * cs_ref[:, 0, csl] + cw[1:2] * cs_ref[:, 1, csl] + cw[2:3] * u
        b_g = _proj(xb, w_in_ref, OFF_B + j * POOL_OUT_GROUP, POOL_OUT_GROUP)
        y_a = b_g * conv

        w = POOL_WINDOWS[j]
        psl = slice(j * POOL_GROUP, (j + 1) * POOL_GROUP)
        pj = p[:, psl]
        wsum = pj
        for k in range(POOL_STATE - (w - 1), POOL_STATE):
            wsum = wsum + ps_ref[k, :, psl]
        q = wsum * (1.0 / w) - pj
        y_p = jnp.dot(q.astype(jnp.bfloat16), pool_w_ref[j],
                      preferred_element_type=jnp.float32) * pool_scale_ref[:, csl]

        g_a = _proj(xb, w_in_ref, OFF_GA + j * POOL_OUT_GROUP, POOL_OUT_GROUP)
        g_p = _proj(xb, w_in_ref, OFF_GP + j * POOL_OUT_GROUP, POOL_OUT_GROUP)
        m = jax.nn.sigmoid(g_a) * y_a + jax.nn.sigmoid(g_p) * y_p
        m_buf[:, csl] = m.astype(jnp.bfloat16)

    out = jnp.dot(m_buf[...], w_out_ref[...], preferred_element_type=jnp.float32)
    o_ref[...] = _layer_norm(ALPHA * x + out, g_ref[...], b_ref[...])

    conv_out_ref[:, 0:CONV_WIDTH - 2, :] = cs_ref[:, 1:CONV_WIDTH - 1, :]
    pool_out_ref[0:POOL_STATE - 1] = ps_ref[1:POOL_STATE]
    pool_out_ref[POOL_STATE - 1] = p


def _mixer_sample(x2d, cs, ps, w_in, conv_w, pool_w, pool_scale, w_out, g, b):
    nb = x2d.shape[0]
    shapes = [x2d.shape, cs.shape, ps.shape, w_in.shape, conv_w.shape, pool_w.shape,
              pool_scale.shape, w_out.shape, g.shape, b.shape]
    return pl.pallas_call(
        _mixer_sample_kernel,
        out_shape=(
            jax.ShapeDtypeStruct((nb, D_MODEL), jnp.float32),
            jax.ShapeDtypeStruct(cs.shape, jnp.float32),
            jax.ShapeDtypeStruct(ps.shape, jnp.float32),
        ),
        grid=(1,),
        in_specs=[_const_spec(s) for s in shapes],
        out_specs=(
            pl.BlockSpec((nb, D_MODEL), lambda i: (0, 0)),
            pl.BlockSpec(cs.shape, lambda i: (0, 0, 0)),
            pl.BlockSpec(ps.shape, lambda i: (0, 0, 0)),
        ),
        scratch_shapes=[pltpu.VMEM((nb, D_MODEL), jnp.bfloat16)],
        compiler_params=pltpu.CompilerParams(
            dimension_semantics=("arbitrary",), vmem_limit_bytes=VMEM_LIMIT_BYTES),
        name="mixer_sample",
    )(x2d, cs, ps, w_in, conv_w, pool_w, pool_scale, w_out, g, b)


def kernel(x_prompt, x_sample, state_conv, state_pool, ln1_g, ln1_b, ffn1_wg, ffn1_wu, ffn1_wd,
           w_in, conv_w, pool_w, pool_scale, w_out, ln2_g, ln2_b,
           ffn2_wg, ffn2_wu, ffn2_wd, ln3_g, ln3_b):
    bsz, seq, _ = x_prompt.shape
    nb = x_sample.shape[0]
    assert x_sample.shape[1] == 1 and DEPTH == 1
    bf = jnp.bfloat16
    tm = 1024
    tm_ffn = 1024

    yp = x_prompt.reshape(bsz * seq, D_MODEL)
    l = 0
    wg1, wu1, wd1 = ffn1_wg[l].astype(bf), ffn1_wu[l].astype(bf), ffn1_wd[l].astype(bf)
    g1, b1 = ln1_g[l][None], ln1_b[l][None]
    g2, b2 = ln2_g[l][None], ln2_b[l][None]
    g3, b3 = ln3_g[l][None], ln3_b[l][None]
    scale = pool_scale[l][None]

    pool_w2d = pool_w[l].reshape(N_POOL_GROUPS * POOL_GROUP, POOL_OUT_GROUP)
    yp, ys, (w_in_b, w_out_b, pool_w_b) = _ffn(yp, x_sample.reshape(nb, D_MODEL), wg1, wu1, wd1,
                                               g1, b1, tm_ffn, cast=(w_in[l], w_out[l], pool_w2d))
    pool_w_b = pool_w_b.reshape(N_POOL_GROUPS, POOL_GROUP, POOL_OUT_GROUP)
    yp, conv_p, pool_p, (wg2, wu2, wd2) = _mixer_prompt(
        yp.reshape(bsz, seq, D_MODEL), w_in_b, conv_w[l], pool_w_b, scale, w_out_b, g2, b2, tm,
        cast=(ffn2_wg[l], ffn2_wu[l], ffn2_wd[l]))
    ps_rows = jnp.transpose(state_pool[l], (1, 0, 2))
    ys, conv_s, pool_s = _mixer_sample(ys, state_conv[l], ps_rows, w_in_b, conv_w[l],
                                       pool_w_b, scale, w_out_b, g2, b2)
    yp, ys, _ = _ffn(yp.reshape(bsz * seq, D_MODEL), ys, wg2, wu2, wd2, g3, b3, tm_ffn,
                     sample_out_shape=x_sample.shape)

    return (yp.reshape(bsz, seq, D_MODEL),
            ys,
            conv_p[None],
            pool_p[None],
            conv_s[None],
            jnp.transpose(pool_s, (1, 0, 2))[None])
```

```python
import functools

import jax
import jax.numpy as jnp
from jax import lax
from jax.experimental import pallas as pl
from jax.experimental.pallas import tpu as pltpu

D_MODEL = 1024
D_CONV = D_MODEL
D_POOL = D_MODEL // 2
N_POOL_GROUPS = 4
POOL_GROUP = D_POOL // N_POOL_GROUPS
POOL_OUT_GROUP = D_MODEL // N_POOL_GROUPS
POOL_WINDOWS = (2, 4, 8, 16)
POOL_STATE = max(POOL_WINDOWS) - 1
CONV_WIDTH = 3
D_FF = 2816
DEPTH = 1
ALPHA = (2.0 * DEPTH) ** 0.25
LN_EPS = 1e-5
D_IN = 3 * D_CONV + D_POOL + 2 * D_MODEL

OFF_C = 0
OFF_B = D_CONV
OFF_H = 2 * D_CONV
OFF_P = 3 * D_CONV
OFF_GA = 3 * D_CONV + D_POOL
OFF_GP = 3 * D_CONV + D_POOL + D_MODEL

SUB_ROWS = 256
FF_CHUNK = 256
STAGE_CHUNKS = 16
POOL_HALO = 16
CONV_HALO = 8
VMEM_LIMIT_BYTES = 52 * 1024 * 1024


def _layer_norm(v, g, b):
    mu = jnp.mean(v, axis=-1, keepdims=True)
    d = v - mu
    var = jnp.mean(d * d, axis=-1, keepdims=True)
    return d * lax.rsqrt(var + LN_EPS) * g + b


def _const_spec(shape):
    zeros = (0,) * len(shape)
    return pl.BlockSpec(shape, lambda *_: zeros, pipeline_mode=pl.Buffered(1))


def _cast_specs(arrays, n_steps, step_index):
    in_specs, out_specs, out_shapes = [], [], []
    for a in arrays:
        rows, cols = a.shape
        assert rows % (n_steps * 16) == 0
        spec = pl.BlockSpec((rows // n_steps, cols), lambda *idx: (step_index(*idx), 0))
        in_specs.append(spec)
        out_specs.append(spec)
        out_shapes.append(jax.ShapeDtypeStruct(a.shape, jnp.bfloat16))
    return in_specs, out_specs, out_shapes


def _run_casts(src_refs, dst_refs):
    for src, dst in zip(src_refs, dst_refs):
        dst[...] = src[...].astype(jnp.bfloat16)


def _ffn_rows(x, wg_ref, wu_ref, wd_ref, g_ref, b_ref, h_view):
    xb = x.astype(jnp.bfloat16)
    for c in range(D_FF // FF_CHUNK):
        sl = slice(c * FF_CHUNK, (c + 1) * FF_CHUNK)
        gate = jnp.dot(xb, wg_ref[:, sl], preferred_element_type=jnp.float32)
        up = jnp.dot(xb, wu_ref[:, sl], preferred_element_type=jnp.float32)
        h_view[:, sl] = (gate * jax.nn.sigmoid(gate) * up).astype(jnp.bfloat16)
    y = jnp.dot(h_view[...], wd_ref[...], preferred_element_type=jnp.float32)
    return _layer_norm(ALPHA * x + 0.5 * y, g_ref[...], b_ref[...])


def _stage_cast(w_hbm, w_vmem, stage, sem):
    chunk = stage.shape[1]
    n = w_hbm.shape[0] // chunk

    def copy(k):
        return pltpu.make_async_copy(w_hbm.at[pl.ds(k * chunk, chunk), :], stage.at[k % 2],
                                     sem.at[k % 2])

    copy(0).start()
    for k in range(n):
        if k + 1 < n:
            copy(k + 1).start()
        copy(k).wait()
        w_vmem[pl.ds(k * chunk, chunk), :] = stage[k % 2].astype(jnp.bfloat16)


def _ffn_kernel(xp_ref, xs_ref, wg_ref, wu_ref, wd_ref, g_ref, b_ref, *rest, sub, n_cast,
                stage_weights):
    cast_src = rest[:n_cast]
    op_ref, os_ref, *cast_dst = rest[n_cast:2 * n_cast + 2]
    h_ref, *stage_scratch = rest[2 * n_cast + 2:]
    i = pl.program_id(0)
    last = pl.num_programs(0) - 1

    if stage_weights:
        wg_v, wu_v, wd_v, stage_up, stage_down, sem = stage_scratch

        @pl.when(i == 0)
        def _():
            _stage_cast(wg_ref, wg_v, stage_up, sem)
            _stage_cast(wu_ref, wu_v, stage_up, sem)
            _stage_cast(wd_ref, wd_v, stage_down, sem)

        wg_ref, wu_ref, wd_ref = wg_v, wu_v, wd_v
    weights = (wg_ref, wu_ref, wd_ref, g_ref, b_ref)

    @pl.when(i < last)
    def _():
        for s in range(xp_ref.shape[0] // sub):
            rows = pl.ds(s * sub, sub)
            op_ref[rows, :] = _ffn_rows(xp_ref[rows, :], *weights, h_ref.at[rows, :])
        _run_casts(cast_src, cast_dst)

    @pl.when(i == last)
    def _():
        ns = xs_ref.shape[0]
        xs = xs_ref[...] if len(xs_ref.shape) == 2 else xs_ref[:, 0, :]
        ys = _ffn_rows(xs, *weights, h_ref.at[pl.ds(0, ns), :])
        os_ref[...] = ys if len(os_ref.shape) == 2 else ys[:, None, :]


def _ffn(xp, xs, wg, wu, wd, g, b, tm, cast=(), sample_out_shape=None):
    n, ns = xp.shape[0], xs.shape[0]
    nt = n // tm
    assert ns <= tm
    sample_out_shape = sample_out_shape or (ns, D_MODEL)
    sample_zeros = (0,) * len(sample_out_shape)
    blk = lambda i: (jnp.minimum(i, nt - 1), 0)
    cast_in, cast_out, cast_shapes = _cast_specs(cast, nt, lambda i: jnp.minimum(i, nt - 1))
    stage_weights = wg.dtype == jnp.float32
    assert wu.dtype == wg.dtype and wd.dtype == wg.dtype
    scratch = [pltpu.VMEM((tm, D_FF), jnp.bfloat16)]
    if stage_weights:
        weight_specs = [pl.BlockSpec(memory_space=pl.ANY)] * 3
        scratch += [
            pltpu.VMEM((D_MODEL, D_FF), jnp.bfloat16),
            pltpu.VMEM((D_MODEL, D_FF), jnp.bfloat16),
            pltpu.VMEM((D_FF, D_MODEL), jnp.bfloat16),
            pltpu.VMEM((2, D_MODEL // STAGE_CHUNKS, D_FF), jnp.float32),
            pltpu.VMEM((2, D_FF // STAGE_CHUNKS, D_MODEL), jnp.float32),
            pltpu.SemaphoreType.DMA((2,)),
        ]
    else:
        weight_specs = [_const_spec((D_MODEL, D_FF)), _const_spec((D_MODEL, D_FF)),
                        _const_spec((D_FF, D_MODEL))]
    outs = pl.pallas_call(
        functools.partial(_ffn_kernel, sub=min(tm, SUB_ROWS), n_cast=len(cast),
                          stage_weights=stage_weights),
        out_shape=[jax.ShapeDtypeStruct((n, D_MODEL), jnp.float32),
                   jax.ShapeDtypeStruct(sample_out_shape, jnp.float32)] + cast_shapes,
        grid=(nt + 1,),
        in_specs=[pl.BlockSpec((tm, D_MODEL), blk), _const_spec(xs.shape)] + weight_specs
        + [_const_spec((1, D_MODEL)), _const_spec((1, D_MODEL))] + cast_in,
        out_specs=[pl.BlockSpec((tm, D_MODEL), blk),
                   pl.BlockSpec(sample_out_shape, lambda i: sample_zeros)] + cast_out,
        scratch_shapes=scratch,
        compiler_params=pltpu.CompilerParams(
            dimension_semantics=("arbitrary",), vmem_limit_bytes=VMEM_LIMIT_BYTES),
        name="ffn",
    )(xp, xs, wg, wu, wd, g, b, *cast)
    return outs[0], outs[1], outs[2:]


def _proj(xb, w_in_ref, off, width):
    return jnp.dot(xb, w_in_ref[:, off:off + width], preferred_element_type=jnp.float32)


def _shift_rows(ext, k, halo):
    return pltpu.roll(ext, k, 0)[halo:]


def _mixer_prompt_kernel(x_ref, w_in_ref, conv_w_ref, pool_w_ref, pool_scale_ref, w_out_ref,
                         g_ref, b_ref, *rest, sub, n_cast):
    cast_src = rest[:n_cast]
    o_ref, conv_out_ref, pool_out_ref, *cast_dst = rest[n_cast:2 * n_cast + 3]
    u_carry, p_carry, m_buf = rest[2 * n_cast + 3:]
    t = pl.program_id(1)
    tm = x_ref.shape[1]

    @pl.when(t == 0)
    def _():
        u_carry[...] = jnp.zeros(u_carry.shape, jnp.float32)
        p_carry[...] = jnp.zeros(p_carry.shape, jnp.float32)

    u_prev = u_carry[...]
    p_prev = p_carry[...]

    def project_out_and_norm(rows):
        out = jnp.dot(m_buf[rows, :], w_out_ref[...], preferred_element_type=jnp.float32)
        o_ref[0, rows, :] = _layer_norm(ALPHA * x_ref[0, rows, :] + out, g_ref[...], b_ref[...])

    pending_rows = None
    for s in range(tm // sub):
        rows = slice(s * sub, (s + 1) * sub)
        x = x_ref[0, rows, :]
        xb = x.astype(jnp.bfloat16)

        p = _proj(xb, w_in_ref, OFF_P, D_POOL)
        p_ext = jnp.concatenate([p_prev, p], axis=0)
        pos = t * tm + s * sub + lax.broadcasted_iota(jnp.int32, (sub, POOL_GROUP), 0)

        u_tails = []
        for j in range(N_POOL_GROUPS):
            csl = slice(j * POOL_OUT_GROUP, (j + 1) * POOL_OUT_GROUP)
            c_g = _proj(xb, w_in_ref, OFF_C + j * POOL_OUT_GROUP, POOL_OUT_GROUP)
            h = _proj(xb, w_in_ref, OFF_H + j * POOL_OUT_GROUP, POOL_OUT_GROUP)
            u = c_g * h
            u_ext = jnp.concatenate([u_prev[:, csl], u], axis=0)
            u_tails.append(u[sub - CONV_HALO:, :])
            cw = conv_w_ref[:, csl]
            conv = (cw[0:1] * _shift_rows(u_ext, 2, CONV_HALO)
                    + cw[1:2] * _shift_rows(u_ext, 1, CONV_HALO)
                    + cw[2:3] * u)
            b_g = _proj(xb, w_in_ref, OFF_B + j * POOL_OUT_GROUP, POOL_OUT_GROUP)
            y_a = b_g * conv

            psl = slice(j * POOL_GROUP, (j + 1) * POOL_GROUP)
            w = POOL_WINDOWS[j]
            acc = p_ext[:, psl]
            span = 1
            while span < w:
                acc = acc + pltpu.roll(acc, span, 0)
                span *= 2
            inv_cnt = 1.0 / jnp.minimum(pos + 1, w).astype(jnp.float32)
            q = acc[POOL_HALO:] * inv_cnt - p[:, psl]
            y_p = jnp.dot(q.astype(jnp.bfloat16), pool_w_ref[j],
                          preferred_element_type=jnp.float32) * pool_scale_ref[:, csl]

            g_a = _proj(xb, w_in_ref, OFF_GA + j * POOL_OUT_GROUP, POOL_OUT_GROUP)
            g_p = _proj(xb, w_in_ref, OFF_GP + j * POOL_OUT_GROUP, POOL_OUT_GROUP)
            m = jax.nn.sigmoid(g_a) * y_a + jax.nn.sigmoid(g_p) * y_p
            m_buf[rows, csl] = m.astype(jnp.bfloat16)

            if j == 0 and pending_rows is not None:
                project_out_and_norm(pending_rows)

        pending_rows = rows
        u_prev = jnp.concatenate(u_tails, axis=1)
        p_prev = p_ext[sub:, :]

    project_out_and_norm(pending_rows)
    _run_casts(cast_src, cast_dst)

    u_carry[...] = u_prev
    p_carry[...] = p_prev
    conv_out_ref[0] = u_carry[CONV_HALO - (CONV_WIDTH - 1):CONV_HALO, :]
    pool_out_ref[0] = p_carry[POOL_HALO - POOL_STATE:POOL_HALO, :]


def _mixer_prompt(x, w_in, conv_w, pool_w, pool_scale, w_out, g, b, tm, cast=()):
    bsz, seq, _ = x.shape
    tiles = seq // tm
    cast_in, cast_out, cast_shapes = _cast_specs(cast, bsz * tiles, lambda s, t: s * tiles + t)
    kern = functools.partial(_mixer_prompt_kernel, sub=min(tm, SUB_ROWS), n_cast=len(cast))
    outs = pl.pallas_call(
        kern,
        out_shape=[
            jax.ShapeDtypeStruct((bsz, seq, D_MODEL), jnp.float32),
            jax.ShapeDtypeStruct((bsz, CONV_WIDTH - 1, D_CONV), jnp.float32),
            jax.ShapeDtypeStruct((bsz, POOL_STATE, D_POOL), jnp.float32),
        ] + cast_shapes,
        grid=(bsz, tiles),
        in_specs=[
            pl.BlockSpec((1, tm, D_MODEL), lambda s, t: (s, t, 0)),
            _const_spec((D_MODEL, D_IN)),
            _const_spec((CONV_WIDTH, D_CONV)),
            _const_spec((N_POOL_GROUPS, POOL_GROUP, POOL_OUT_GROUP)),
            _const_spec((1, D_MODEL)),
            _const_spec((D_MODEL, D_MODEL)),
            _const_spec((1, D_MODEL)),
            _const_spec((1, D_MODEL)),
        ] + cast_in,
        out_specs=[
            pl.BlockSpec((1, tm, D_MODEL), lambda s, t: (s, t, 0)),
            pl.BlockSpec((1, CONV_WIDTH - 1, D_CONV), lambda s, t: (s, 0, 0)),
            pl.BlockSpec((1, POOL_STATE, D_POOL), lambda s, t: (s, 0, 0)),
        ] + cast_out,
        scratch_shapes=[
            pltpu.VMEM((CONV_HALO, D_CONV), jnp.float32),
            pltpu.VMEM((POOL_HALO, D_POOL), jnp.float32),
            pltpu.VMEM((tm, D_MODEL), jnp.bfloat16),
        ],
        compiler_params=pltpu.CompilerParams(
            dimension_semantics=("arbitrary", "arbitrary"), vmem_limit_bytes=VMEM_LIMIT_BYTES),
        name="mixer_prompt",
    )(x, w_in, conv_w, pool_w, pool_scale, w_out, g, b, *cast)
    return outs[0], outs[1], outs[2], outs[3:]


def _mixer_sample_kernel(x_ref, cs_ref, ps_ref, w_in_ref, conv_w_ref, pool_w_ref, pool_scale_ref,
                         w_out_ref, g_ref, b_ref, o_ref, conv_out_ref, pool_out_ref, m_buf):
    x = x_ref[...]
    xb = x.astype(jnp.bfloat16)
    p = _proj(xb, w_in_ref, OFF_P, D_POOL)

    for j in range(N_POOL_GROUPS):
        csl = slice(j * POOL_OUT_GROUP, (j + 1) * POOL_OUT_GROUP)
        c_g = _proj(xb, w_in_ref, OFF_C + j * POOL_OUT_GROUP, POOL_OUT_GROUP)
        h = _proj(xb, w_in_ref, OFF_H + j * POOL_OUT_GROUP, POOL_OUT_GROUP)
        u = c_g * h
        conv_out_ref[:, CONV_WIDTH - 2:CONV_WIDTH - 1, csl] = u[:, None, :]
        cw = conv_w_ref[:, csl]
        conv = cw[0:1] * cs_ref[:, 0, csl] + cw[1:2] * cs_ref[:, 1, csl] + cw[2:3] * u
        b_g = _proj(xb, w_in_ref, OFF_B + j * POOL_OUT_GROUP, POOL_OUT_GROUP)
        y_a = b_g * conv

        w = POOL_WINDOWS[j]
        psl = slice(j * POOL_GROUP, (j + 1) * POOL_GROUP)
        pj = p[:, psl]
        wsum = pj
        for k in range(POOL_STATE - (w - 1), POOL_STATE):
            wsum = wsum + ps_ref[k, :, psl]
        q = wsum * (1.0 / w) - pj
        y_p = jnp.dot(q.astype(jnp.bfloat16), pool_w_ref[j],
                      preferred_element_type=jnp.float32) * pool_scale_ref[:, csl]

        g_a = _proj(xb, w_in_ref, OFF_GA + j * POOL_OUT_GROUP, POOL_OUT_GROUP)
        g_p = _proj(xb, w_in_ref, OFF_GP + j * POOL_OUT_GROUP, POOL_OUT_GROUP)
        m = jax.nn.sigmoid(g_a) * y_a + jax.nn.sigmoid(g_p) * y_p
        m_buf[:, csl] = m.astype(jnp.bfloat16)

    out = jnp.dot(m_buf[...], w_out_ref[...], preferred_element_type=jnp.float32)
    o_ref[...] = _layer_norm(ALPHA * x + out, g_ref[...], b_ref[...])

    conv_out_ref[:, 0:CONV_WIDTH - 2, :] = cs_ref[:, 1:CONV_WIDTH - 1, :]
    pool_out_ref[0:POOL_STATE - 1] = ps_ref[1:POOL_STATE]
    pool_out_ref[POOL_STATE - 1] = p


def _mixer_sample(x2d, cs, ps, w_in, conv_w, pool_w, pool_scale, w_out, g, b):
    nb = x2d.shape[0]
    shapes = [x2d.shape, cs.shape, ps.shape, w_in.shape, conv_w.shape, pool_w.shape,
              pool_scale.shape, w_out.shape, g.shape, b.shape]
    return pl.pallas_call(
        _mixer_sample_kernel,
        out_shape=(
            jax.ShapeDtypeStruct((nb, D_MODEL), jnp.float32),
            jax.ShapeDtypeStruct(cs.shape, jnp.float32),
            jax.ShapeDtypeStruct(ps.shape, jnp.float32),
        ),
        grid=(1,),
        in_specs=[_const_spec(s) for s in shapes],
        out_specs=(
            pl.BlockSpec((nb, D_MODEL), lambda i: (0, 0)),
            pl.BlockSpec(cs.shape, lambda i: (0, 0, 0)),
            pl.BlockSpec(ps.shape, lambda i: (0, 0, 0)),
        ),
        scratch_shapes=[pltpu.VMEM((nb, D_MODEL), jnp.bfloat16)],
        compiler_params=pltpu.CompilerParams(
            dimension_semantics=("arbitrary",), vmem_limit_bytes=VMEM_LIMIT_BYTES),
        name="mixer_sample",
    )(x2d, cs, ps, w_in, conv_w, pool_w, pool_scale, w_out, g, b)


def kernel(x_prompt, x_sample, state_conv, state_pool, ln1_g, ln1_b, ffn1_wg, ffn1_wu, ffn1_wd,
           w_in, conv_w, pool_w, pool_scale, w_out, ln2_g, ln2_b,
           ffn2_wg, ffn2_wu, ffn2_wd, ln3_g, ln3_b):
    bsz, seq, _ = x_prompt.shape
    nb = x_sample.shape[0]
    assert x_sample.shape[1] == 1 and DEPTH == 1
    tm = 1024
    tm_ffn = 1024

    yp = x_prompt.reshape(bsz * seq, D_MODEL)
    l = 0
    wg1, wu1, wd1 = ffn1_wg[l], ffn1_wu[l], ffn1_wd[l]
    g1, b1 = ln1_g[l][None], ln1_b[l][None]
    g2, b2 = ln2_g[l][None], ln2_b[l][None]
    g3, b3 = ln3_g[l][None], ln3_b[l][None]
    scale = pool_scale[l][None]

    pool_w2d = pool_w[l].reshape(N_POOL_GROUPS * POOL_GROUP, POOL_OUT_GROUP)
    yp, ys, (w_in_b, w_out_b, pool_w_b) = _ffn(yp, x_sample.reshape(nb, D_MODEL), wg1, wu1, wd1,
                                               g1, b1, tm_ffn, cast=(w_in[l], w_out[l], pool_w2d))
    pool_w_b = pool_w_b.reshape(N_POOL_GROUPS, POOL_GROUP, POOL_OUT_GROUP)
    yp, conv_p, pool_p, (wg2, wu2, wd2) = _mixer_prompt(
        yp.reshape(bsz, seq, D_MODEL), w_in_b, conv_w[l], pool_w_b, scale, w_out_b, g2, b2, tm,
        cast=(ffn2_wg[l], ffn2_wu[l], ffn2_wd[l]))
    ps_rows = jnp.transpose(state_pool[l], (1, 0, 2))
    ys, conv_s, pool_s = _mixer_sample(ys, state_conv[l], ps_rows, w_in_b, conv_w[l],
                                       pool_w_b, scale, w_out_b, g2, b2)
    yp, ys, _ = _ffn(yp.reshape(bsz * seq, D_MODEL), ys, wg2, wu2, wd2, g3, b3, tm_ffn,
                     sample_out_shape=x_sample.shape)

    return (yp.reshape(bsz, seq, D_MODEL),
            ys,
            conv_p[None],
            pool_p[None],
            conv_s[None],
            jnp.transpose(pool_s, (1, 0, 2))[None])
```

```python
import functools

import jax
import jax.numpy as jnp
from jax import lax
from jax.experimental import pallas as pl
from jax.experimental.pallas import tpu as pltpu

D_MODEL = 1024
D_CONV = D_MODEL
D_POOL = D_MODEL // 2
N_POOL_GROUPS = 4
POOL_GROUP = D_POOL // N_POOL_GROUPS
POOL_OUT_GROUP = D_MODEL // N_POOL_GROUPS
POOL_WINDOWS = (2, 4, 8, 16)
POOL_STATE = max(POOL_WINDOWS) - 1
CONV_WIDTH = 3
D_FF = 2816
DEPTH = 1
ALPHA = (2.0 * DEPTH) ** 0.25
LN_EPS = 1e-5
D_IN = 3 * D_CONV + D_POOL + 2 * D_MODEL

OFF_C = 0
OFF_B = D_CONV
OFF_H = 2 * D_CONV
OFF_P = 3 * D_CONV
OFF_GA = 3 * D_CONV + D_POOL
OFF_GP = 3 * D_CONV + D_POOL + D_MODEL

SUB_ROWS = 256
FF_CHUNK = 256
STAGE_CHUNKS = 16
POOL_HALO = 16
CONV_HALO = 8
VMEM_LIMIT_BYTES = 52 * 1024 * 1024


def _layer_norm(v, g, b):
    mu = jnp.mean(v, axis=-1, keepdims=True)
    d = v - mu
    var = jnp.mean(d * d, axis=-1, keepdims=True)
    return d * lax.rsqrt(var + LN_EPS) * g + b


def _const_spec(shape):
    zeros = (0,) * len(shape)
    return pl.BlockSpec(shape, lambda *_: zeros, pipeline_mode=pl.Buffered(1))


def _cast_specs(arrays, n_steps, step_index):
    in_specs, out_specs, out_shapes = [], [], []
    for a in arrays:
        rows, cols = a.shape
        assert rows % (n_steps * 16) == 0
        spec = pl.BlockSpec((rows // n_steps, cols), lambda *idx: (step_index(*idx), 0))
        in_specs.append(spec)
        out_specs.append(spec)
        out_shapes.append(jax.ShapeDtypeStruct(a.shape, jnp.bfloat16))
    return in_specs, out_specs, out_shapes


def _run_casts(src_refs, dst_refs):
    for src, dst in zip(src_refs, dst_refs):
        dst[...] = src[...].astype(jnp.bfloat16)


def _ffn_rows(x, wg_ref, wu_ref, wd_ref, g_ref, b_ref, h_view):
    xb = x.astype(jnp.bfloat16)
    for c in range(D_FF // FF_CHUNK):
        sl = slice(c * FF_CHUNK, (c + 1) * FF_CHUNK)
        gate = jnp.dot(xb, wg_ref[:, sl], preferred_element_type=jnp.float32)
        up = jnp.dot(xb, wu_ref[:, sl], preferred_element_type=jnp.float32)
        h_view[:, sl] = (gate * jax.nn.sigmoid(gate) * up).astype(jnp.bfloat16)
    y = jnp.dot(h_view[...], wd_ref[...], preferred_element_type=jnp.float32)
    return _layer_norm(ALPHA * x + 0.5 * y, g_ref[...], b_ref[...])


def _stage_cast(w_hbm, w_vmem, stage, sem):
    chunk = stage.shape[1]
    n = w_hbm.shape[0] // chunk

    def copy(k):
        return pltpu.make_async_copy(w_hbm.at[pl.ds(k * chunk, chunk), :], stage.at[k % 2],
                                     sem.at[k % 2])

    copy(0).start()
    for k in range(n):
        if k + 1 < n:
            copy(k + 1).start()
        copy(k).wait()
        w_vmem[pl.ds(k * chunk, chunk), :] = stage[k % 2].astype(jnp.bfloat16)


def _ffn_kernel(xp_ref, xs_ref, wg_ref, wu_ref, wd_ref, g_ref, b_ref, *rest, sub, n_cast,
                stage_weights):
    cast_src = rest[:n_cast]
    op_ref, os_ref, *cast_dst = rest[n_cast:2 * n_cast + 2]
    h_ref, *stage_scratch = rest[2 * n_cast + 2:]
    i = pl.program_id(0)
    last = pl.num_programs(0) - 1

    if stage_weights:
        wg_v, wu_v, wd_v, stage_up, stage_down, sem = stage_scratch

        @pl.when(i == 0)
        def _():
            _stage_cast(wg_ref, wg_v, stage_up, sem)
            _stage_cast(wu_ref, wu_v, stage_up, sem)
            _stage_cast(wd_ref, wd_v, stage_down, sem)

        wg_ref, wu_ref, wd_ref = wg_v, wu_v, wd_v
    weights = (wg_ref, wu_ref, wd_ref, g_ref, b_ref)

    @pl.when(i < last)
    def _():
        for s in range(xp_ref.shape[0] // sub):
            rows = pl.ds(s * sub, sub)
            op_ref[rows, :] = _ffn_rows(xp_ref[rows, :], *weights, h_ref.at[rows, :])
        _run_casts(cast_src, cast_dst)

    @pl.when(i == last)
    def _():
        ns = xs_ref.shape[0]
        xs = xs_ref[...] if len(xs_ref.shape) == 2 else xs_ref[:, 0, :]
        ys = _ffn_rows(xs, *weights, h_ref.at[pl.ds(0, ns), :])
        os_ref[...] = ys if len(os_ref.shape) == 2 else ys[:, None, :]


def _ffn(xp, xs, wg, wu, wd, g, b, tm, cast=(), sample_out_shape=None):
    n, ns = xp.shape[0], xs.shape[0]
    nt = n // tm
    assert ns <= tm
    sample_out_shape = sample_out_shape or (ns, D_MODEL)
    sample_zeros = (0,) * len(sample_out_shape)
    blk = lambda i: (jnp.minimum(i, nt - 1), 0)
    cast_in, cast_out, cast_shapes = _cast_specs(cast, nt, lambda i: jnp.minimum(i, nt - 1))
    stage_weights = wg.dtype == jnp.float32
    assert wu.dtype == wg.dtype and wd.dtype == wg.dtype
    scratch = [pltpu.VMEM((tm, D_FF), jnp.bfloat16)]
    if stage_weights:
        weight_specs = [pl.BlockSpec(memory_space=pl.ANY)] * 3
        scratch += [
            pltpu.VMEM((D_MODEL, D_FF), jnp.bfloat16),
            pltpu.VMEM((D_MODEL, D_FF), jnp.bfloat16),
            pltpu.VMEM((D_FF, D_MODEL), jnp.bfloat16),
            pltpu.VMEM((2, D_MODEL // STAGE_CHUNKS, D_FF), jnp.float32),
            pltpu.VMEM((2, D_FF // STAGE_CHUNKS, D_MODEL), jnp.float32),
            pltpu.SemaphoreType.DMA((2,)),
        ]
    else:
        weight_specs = [_const_spec((D_MODEL, D_FF)), _const_spec((D_MODEL, D_FF)),
                        _const_spec((D_FF, D_MODEL))]
    outs = pl.pallas_call(
        functools.partial(_ffn_kernel, sub=min(tm, SUB_ROWS), n_cast=len(cast),
                          stage_weights=stage_weights),
        out_shape=[jax.ShapeDtypeStruct((n, D_MODEL), jnp.float32),
                   jax.ShapeDtypeStruct(sample_out_shape, jnp.float32)] + cast_shapes,
        grid=(nt + 1,),
        in_specs=[pl.BlockSpec((tm, D_MODEL), blk), _const_spec(xs.shape)] + weight_specs
        + [_const_spec((1, D_MODEL)), _const_spec((1, D_MODEL))] + cast_in,
        out_specs=[pl.BlockSpec((tm, D_MODEL), blk),
                   pl.BlockSpec(sample_out_shape, lambda i: sample_zeros)] + cast_out,
        scratch_shapes=scratch,
        compiler_params=pltpu.CompilerParams(
            dimension_semantics=("arbitrary",), vmem_limit_bytes=VMEM_LIMIT_BYTES),
        name="ffn",
    )(xp, xs, wg, wu, wd, g, b, *cast)
    return outs[0], outs[1], outs[2:]


def _proj(xb, w_in_ref, off, width):
    return jnp.dot(xb, w_in_ref[:, off:off + width], preferred_element_type=jnp.float32)


def _shift_rows(ext, k, halo):
    return pltpu.roll(ext, k, 0)[halo:]


def _mixer_prompt_kernel(x_ref, w_in_ref, conv_w_ref, pool_w_ref, pool_scale_ref, w_out_ref,
                         g_ref, b_ref, *rest, sub, n_cast):
    cast_src = rest[:n_cast]
    o_ref, conv_out_ref, pool_out_ref, *cast_dst = rest[n_cast:2 * n_cast + 3]
    u_carry, p_carry, m_buf = rest[2 * n_cast + 3:]
    t = pl.program_id(1)
    tm = x_ref.shape[1]

    @pl.when(t == 0)
    def _():
        u_carry[...] = jnp.zeros(u_carry.shape, jnp.float32)
        p_carry[...] = jnp.zeros(p_carry.shape, jnp.float32)

    u_prev = u_carry[...]
    p_prev = p_carry[...]

    def project_out_and_norm(rows):
        out = jnp.dot(m_buf[rows, :], w_out_ref[...], preferred_element_type=jnp.float32)
        o_ref[0, rows, :] = _layer_norm(ALPHA * x_ref[0, rows, :] + out, g_ref[...], b_ref[...])

    pending_rows = None
    for s in range(tm // sub):
        rows = slice(s * sub, (s + 1) * sub)
        x = x_ref[0, rows, :]
        xb = x.astype(jnp.bfloat16)

        p = _proj(xb, w_in_ref, OFF_P, D_POOL)
        p_ext = jnp.concatenate([p_prev, p], axis=0)
        pos = t * tm + s * sub + lax.broadcasted_iota(jnp.int32, (sub, POOL_GROUP), 0)

        u_tails = []
        for j in range(N_POOL_GROUPS):
            csl = slice(j * POOL_OUT_GROUP, (j + 1) * POOL_OUT_GROUP)
            c_g = _proj(xb, w_in_ref, OFF_C + j * POOL_OUT_GROUP, POOL_OUT_GROUP)
            h = _proj(xb, w_in_ref, OFF_H + j * POOL_OUT_GROUP, POOL_OUT_GROUP)
            u = c_g * h
            u_ext = jnp.concatenate([u_prev[:, csl], u], axis=0)
            u_tails.append(u[sub - CONV_HALO:, :])
            cw = conv_w_ref[:, csl]
            conv = (cw[0:1] * _shift_rows(u_ext, 2, CONV_HALO)
                    + cw[1:2] * _shift_rows(u_ext, 1, CONV_HALO)
                    + cw[2:3] * u)
            b_g = _proj(xb, w_in_ref, OFF_B + j * POOL_OUT_GROUP, POOL_OUT_GROUP)
            y_a = b_g * conv

            psl = slice(j * POOL_GROUP, (j + 1) * POOL_GROUP)
            w = POOL_WINDOWS[j]
            acc = p_ext[:, psl]
            span = 1
            while span < w:
                acc = acc + pltpu.roll(acc, span, 0)
                span *= 2
            inv_cnt = 1.0 / jnp.minimum(pos + 1, w).astype(jnp.float32)
            q = acc[POOL_HALO:] * inv_cnt - p[:, psl]
            y_p = jnp.dot(q.astype(jnp.bfloat16), pool_w_ref[j],
                          preferred_element_type=jnp.float32) * pool_scale_ref[:, csl]

            g_a = _proj(xb, w_in_ref, OFF_GA + j * POOL_OUT_GROUP, POOL_OUT_GROUP)
            g_p = _proj(xb, w_in_ref, OFF_GP + j * POOL_OUT_GROUP, POOL_OUT_GROUP)
            m = jax.nn.sigmoid(g_a) * y_a + jax.nn.sigmoid(g_p) * y_p
            m_buf[rows, csl] = m.astype(jnp.bfloat16)

            if j == 0 and pending_rows is not None:
                project_out_and_norm(pending_rows)

        pending_rows = rows
        u_prev = jnp.concatenate(u_tails, axis=1)
        p_prev = p_ext[sub:, :]

    project_out_and_norm(pending_rows)
    _run_casts(cast_src, cast_dst)

    u_carry[...] = u_prev
    p_carry[...] = p_prev
    conv_out_ref[0] = u_carry[CONV_HALO - (CONV_WIDTH - 1):CONV_HALO, :]
    pool_out_ref[0] = p_carry[POOL_HALO - POOL_STATE:POOL_HALO, :]


def _mixer_prompt(x, w_in, conv_w, pool_w, pool_scale, w_out, g, b, tm, cast=()):
    bsz, seq, _ = x.shape
    tiles = seq // tm
    cast_in, cast_out, cast_shapes = _cast_specs(cast, bsz * tiles, lambda s, t: s * tiles + t)
    kern = functools.partial(_mixer_prompt_kernel, sub=min(tm, SUB_ROWS), n_cast=len(cast))
    outs = pl.pallas_call(
        kern,
        out_shape=[
            jax.ShapeDtypeStruct((bsz, seq, D_MODEL), jnp.float32),
            jax.ShapeDtypeStruct((bsz, CONV_WIDTH - 1, D_CONV), jnp.float32),
            jax.ShapeDtypeStruct((bsz, POOL_STATE, D_POOL), jnp.float32),
        ] + cast_shapes,
        grid=(bsz, tiles),
        in_specs=[
            pl.BlockSpec((1, tm, D_MODEL), lambda s, t: (s, t, 0)),
            _const_spec((D_MODEL, D_IN)),
            _const_spec((CONV_WIDTH, D_CONV)),
            _const_spec((N_POOL_GROUPS, POOL_GROUP, POOL_OUT_GROUP)),
            _const_spec((1, D_MODEL)),
            _const_spec((D_MODEL, D_MODEL)),
            _const_spec((1, D_MODEL)),
            _const_spec((1, D_MODEL)),
        ] + cast_in,
        out_specs=[
            pl.BlockSpec((1, tm, D_MODEL), lambda s, t: (s, t, 0)),
            pl.BlockSpec((1, CONV_WIDTH - 1, D_CONV), lambda s, t: (s, 0, 0)),
            pl.BlockSpec((1, POOL_STATE, D_POOL), lambda s, t: (s, 0, 0)),
        ] + cast_out,
        scratch_shapes=[
            pltpu.VMEM((CONV_HALO, D_CONV), jnp.float32),
            pltpu.VMEM((POOL_HALO, D_POOL), jnp.float32),
            pltpu.VMEM((tm, D_MODEL), jnp.bfloat16),
        ],
        compiler_params=pltpu.CompilerParams(
            dimension_semantics=("arbitrary", "arbitrary"), vmem_limit_bytes=VMEM_LIMIT_BYTES),
        name="mixer_prompt",
    )(x, w_in, conv_w, pool_w, pool_scale, w_out, g, b, *cast)
    return outs[0], outs[1], outs[2], outs[3:]


def _mixer_sample_kernel(x_ref, cs_ref, ps_ref, w_in_ref, conv_w_ref, pool_w_ref, pool_scale_ref,
                         w_out_ref, g_ref, b_ref, o_ref, conv_out_ref, pool_out_ref, m_buf):
    x = x_ref[...]
    xb = x.astype(jnp.bfloat16)
    p = _proj(xb, w_in_ref, OFF_P, D_POOL)

    for j in range(N_POOL_GROUPS):
        csl = slice(j * POOL_OUT_GROUP, (j + 1) * POOL_OUT_GROUP)
        c_g = _proj(xb, w_in_ref, OFF_C + j * POOL_OUT_GROUP, POOL_OUT_GROUP)
        h = _proj(xb, w_in_ref, OFF_H + j * POOL_OUT_GROUP, POOL_OUT_GROUP)
        u = c_g * h
        conv_out_ref[:, CONV_WIDTH - 2:CONV_WIDTH - 1, csl] = u[:, None, :]
        cw = conv_w_ref[:, csl]
        conv = cw[0:1] * cs_ref[:, 0, csl] + cw[1:2] * cs_ref[:, 1, csl] + cw[2:3] * u
        b_g = _proj(xb, w_in_ref, OFF_B + j * POOL_OUT_GROUP, POOL_OUT_GROUP)
        y_a = b_g * conv

        w = POOL_WINDOWS[j]
        psl = slice(j * POOL_GROUP, (j + 1) * POOL_GROUP)
        pj = p[:, psl]
        wsum = pj
        for k in range(POOL_STATE - (w - 1), POOL_STATE):
            wsum = wsum + ps_ref[k, :, psl]
        q = wsum * (1.0 / w) - pj
        y_p = jnp.dot(q.astype(jnp.bfloat16), pool_w_ref[j],
                      preferred_element_type=jnp.float32) * pool_scale_ref[:, csl]

        g_a = _proj(xb, w_in_ref, OFF_GA + j * POOL_OUT_GROUP, POOL_OUT_GROUP)
        g_p = _proj(xb, w_in_ref, OFF_GP + j * POOL_OUT_GROUP, POOL_OUT_GROUP)
        m = jax.nn.sigmoid(g_a) * y_a + jax.nn.sigmoid(g_p) * y_p
        m_buf[:, csl] = m.astype(jnp.bfloat16)

    out = jnp.dot(m_buf[...], w_out_ref[...], preferred_element_type=jnp.float32)
    o_ref[...] = _layer_norm(ALPHA * x + out, g_ref[...], b_ref[...])

    conv_out_ref[:, 0:CONV_WIDTH - 2, :] = cs_ref[:, 1:CONV_WIDTH - 1, :]
    pool_out_ref[0:POOL_STATE - 1] = ps_ref[1:POOL_STATE]
    pool_out_ref[POOL_STATE - 1] = p


def _mixer_sample(x2d, cs, ps, w_in, conv_w, pool_w, pool_scale, w_out, g, b):
    nb = x2d.shape[0]
    shapes = [x2d.shape, cs.shape, ps.shape, w_in.shape, conv_w.shape, pool_w.shape,
              pool_scale.shape, w_out.shape, g.shape, b.shape]
    return pl.pallas_call(
        _mixer_sample_kernel,
        out_shape=(
            jax.ShapeDtypeStruct((nb, D_MODEL), jnp.float32),
            jax.ShapeDtypeStruct(cs.shape, jnp.float32),
            jax.ShapeDtypeStruct(ps.shape, jnp.float32),
        ),
        grid=(1,),
        in_specs=[_const_spec(s) for s in shapes],
        out_specs=(
            pl.BlockSpec((nb, D_MODEL), lambda i: (0, 0)),
            pl.BlockSpec(cs.shape, lambda i: (0, 0, 0)),
            pl.BlockSpec(ps.shape, lambda i: (0, 0, 0)),
        ),
        scratch_shapes=[pltpu.VMEM((nb, D_MODEL), jnp.bfloat16)],
        compiler_params=pltpu.CompilerParams(
            dimension_semantics=("arbitrary",), vmem_limit_bytes=VMEM_LIMIT_BYTES),
        name="mixer_sample",
    )(x2d, cs, ps, w_in, conv_w, pool_w, pool_scale, w_out, g, b)


def kernel(x_prompt, x_sample, state_conv, state_pool, ln1_g, ln1_b, ffn1_wg, ffn1_wu, ffn1_wd,
           w_in, conv_w, pool_w, pool_scale, w_out, ln2_g, ln2_b,
           ffn2_wg, ffn2_wu, ffn2_wd, ln3_g, ln3_b):
    bsz, seq, _ = x_prompt.shape
    nb = x_sample.shape[0]
    assert x_sample.shape[1] == 1 and DEPTH == 1
    tm = 1024
    tm_ffn = 512

    yp = x_prompt.reshape(bsz * seq, D_MODEL)
    l = 0
    bf = jnp.bfloat16
    wg1, wu1, wd1 = ffn1_wg[l].astype(bf), ffn1_wu[l].astype(bf), ffn1_wd[l].astype(bf)
    g1, b1 = ln1_g[l][None], ln1_b[l][None]
    g2, b2 = ln2_g[l][None], ln2_b[l][None]
    g3, b3 = ln3_g[l][None], ln3_b[l][None]
    scale = pool_scale[l][None]

    pool_w2d = pool_w[l].reshape(N_POOL_GROUPS * POOL_GROUP, POOL_OUT_GROUP)
    yp, ys, (w_in_b, w_out_b, pool_w_b) = _ffn(yp, x_sample.reshape(nb, D_MODEL), wg1, wu1, wd1,
                                               g1, b1, tm_ffn, cast=(w_in[l], w_out[l], pool_w2d))
    pool_w_b = pool_w_b.reshape(N_POOL_GROUPS, POOL_GROUP, POOL_OUT_GROUP)
    yp, conv_p, pool_p, (wg2, wu2, wd2) = _mixer_prompt(
        yp.reshape(bsz, seq, D_MODEL), w_in_b, conv_w[l], pool_w_b, scale, w_out_b, g2, b2, tm,
        cast=(ffn2_wg[l], ffn2_wu[l], ffn2_wd[l]))
    ps_rows = jnp.transpose(state_pool[l], (1, 0, 2))
    ys, conv_s, pool_s = _mixer_sample(ys, state_conv[l], ps_rows, w_in_b, conv_w[l],
                                       pool_w_b, scale, w_out_b, g2, b2)
    yp, ys, _ = _ffn(yp.reshape(bsz * seq, D_MODEL), ys, wg2, wu2, wd2, g3, b3, tm_ffn,
                     sample_out_shape=x_sample.shape)

    return (yp.reshape(bsz, seq, D_MODEL),
            ys,
            conv_p[None],
            pool_p[None],
            conv_s[None],
            jnp.transpose(pool_s, (1, 0, 2))[None])
```

```python
import functools

import jax
import jax.numpy as jnp
from jax import lax
from jax.experimental import pallas as pl
from jax.experimental.pallas import tpu as pltpu

D_MODEL = 1024
D_CONV = D_MODEL
D_POOL = D_MODEL // 2
N_POOL_GROUPS = 4
POOL_GROUP = D_POOL // N_POOL_GROUPS
POOL_OUT_GROUP = D_MODEL // N_POOL_GROUPS
POOL_WINDOWS = (2, 4, 8, 16)
POOL_STATE = max(POOL_WINDOWS) - 1
CONV_WIDTH = 3
D_FF = 2816
DEPTH = 1
ALPHA = (2.0 * DEPTH) ** 0.25
LN_EPS = 1e-5
D_IN = 3 * D_CONV + D_POOL + 2 * D_MODEL

OFF_C = 0
OFF_B = D_CONV
OFF_H = 2 * D_CONV
OFF_P = 3 * D_CONV
OFF_GA = 3 * D_CONV + D_POOL
OFF_GP = 3 * D_CONV + D_POOL + D_MODEL

SUB_ROWS = 256
FFN_SUB_ROWS = 512
FF_CHUNK = 256
STAGE_CHUNKS = 16
POOL_HALO = 16
CONV_HALO = 8
VMEM_LIMIT_BYTES = 52 * 1024 * 1024


def _layer_norm(v, g, b):
    mu = jnp.mean(v, axis=-1, keepdims=True)
    d = v - mu
    var = jnp.mean(d * d, axis=-1, keepdims=True)
    return d * lax.rsqrt(var + LN_EPS) * g + b


def _const_spec(shape):
    zeros = (0,) * len(shape)
    return pl.BlockSpec(shape, lambda *_: zeros, pipeline_mode=pl.Buffered(1))


def _cast_specs(arrays, n_steps, step_index):
    in_specs, out_specs, out_shapes = [], [], []
    for a in arrays:
        rows, cols = a.shape
        assert rows % (n_steps * 16) == 0
        spec = pl.BlockSpec((rows // n_steps, cols), lambda *idx: (step_index(*idx), 0))
        in_specs.append(spec)
        out_specs.append(spec)
        out_shapes.append(jax.ShapeDtypeStruct(a.shape, jnp.bfloat16))
    return in_specs, out_specs, out_shapes


def _run_casts(src_refs, dst_refs):
    for src, dst in zip(src_refs, dst_refs):
        dst[...] = src[...].astype(jnp.bfloat16)


def _ffn_rows(x, wg_ref, wu_ref, wd_ref, g_ref, b_ref, h_view):
    xb = x.astype(jnp.bfloat16)
    for c in range(D_FF // FF_CHUNK):
        sl = slice(c * FF_CHUNK, (c + 1) * FF_CHUNK)
        gate = jnp.dot(xb, wg_ref[:, sl], preferred_element_type=jnp.float32)
        up = jnp.dot(xb, wu_ref[:, sl], preferred_element_type=jnp.float32)
        h_view[:, sl] = (gate * jax.nn.sigmoid(gate) * up).astype(jnp.bfloat16)
    y = jnp.dot(h_view[...], wd_ref[...], preferred_element_type=jnp.float32)
    return _layer_norm(ALPHA * x + 0.5 * y, g_ref[...], b_ref[...])


def _stage_cast(w_hbm, w_vmem, stage, sem):
    chunk = stage.shape[1]
    n = w_hbm.shape[0] // chunk

    def copy(k):
        return pltpu.make_async_copy(w_hbm.at[pl.ds(k * chunk, chunk), :], stage.at[k % 2],
                                     sem.at[k % 2])

    copy(0).start()
    for k in range(n):
        if k + 1 < n:
            copy(k + 1).start()
        copy(k).wait()
        w_vmem[pl.ds(k * chunk, chunk), :] = stage[k % 2].astype(jnp.bfloat16)


def _ffn_kernel(xp_ref, xs_ref, wg_ref, wu_ref, wd_ref, g_ref, b_ref, *rest, sub, n_cast,
                stage_weights):
    cast_src = rest[:n_cast]
    op_ref, os_ref, *cast_dst = rest[n_cast:2 * n_cast + 2]
    h_ref, *stage_scratch = rest[2 * n_cast + 2:]
    i = pl.program_id(0)
    last = pl.num_programs(0) - 1

    if stage_weights:
        wg_v, wu_v, wd_v, stage_up, stage_down, sem = stage_scratch

        @pl.when(i == 0)
        def _():
            _stage_cast(wg_ref, wg_v, stage_up, sem)
            _stage_cast(wu_ref, wu_v, stage_up, sem)
            _stage_cast(wd_ref, wd_v, stage_down, sem)

        wg_ref, wu_ref, wd_ref = wg_v, wu_v, wd_v
    weights = (wg_ref, wu_ref, wd_ref, g_ref, b_ref)

    @pl.when(i < last)
    def _():
        for s in range(xp_ref.shape[0] // sub):
            rows = pl.ds(s * sub, sub)
            op_ref[rows, :] = _ffn_rows(xp_ref[rows, :], *weights, h_ref.at[rows, :])
        _run_casts(cast_src, cast_dst)

    @pl.when(i == last)
    def _():
        ns = xs_ref.shape[0]
        xs = xs_ref[...] if len(xs_ref.shape) == 2 else xs_ref[:, 0, :]
        ys = _ffn_rows(xs, *weights, h_ref.at[pl.ds(0, ns), :])
        os_ref[...] = ys if len(os_ref.shape) == 2 else ys[:, None, :]


def _ffn(xp, xs, wg, wu, wd, g, b, tm, cast=(), sample_out_shape=None):
    n, ns = xp.shape[0], xs.shape[0]
    nt = n // tm
    assert ns <= tm
    sample_out_shape = sample_out_shape or (ns, D_MODEL)
    sample_zeros = (0,) * len(sample_out_shape)
    blk = lambda i: (jnp.minimum(i, nt - 1), 0)
    cast_in, cast_out, cast_shapes = _cast_specs(cast, nt, lambda i: jnp.minimum(i, nt - 1))
    stage_weights = wg.dtype == jnp.float32
    assert wu.dtype == wg.dtype and wd.dtype == wg.dtype
    scratch = [pltpu.VMEM((tm, D_FF), jnp.bfloat16)]
    if stage_weights:
        weight_specs = [pl.BlockSpec(memory_space=pl.ANY)] * 3
        scratch += [
            pltpu.VMEM((D_MODEL, D_FF), jnp.bfloat16),
            pltpu.VMEM((D_MODEL, D_FF), jnp.bfloat16),
            pltpu.VMEM((D_FF, D_MODEL), jnp.bfloat16),
            pltpu.VMEM((2, D_MODEL // STAGE_CHUNKS, D_FF), jnp.float32),
            pltpu.VMEM((2, D_FF // STAGE_CHUNKS, D_MODEL), jnp.float32),
            pltpu.SemaphoreType.DMA((2,)),
        ]
    else:
        weight_specs = [_const_spec((D_MODEL, D_FF)), _const_spec((D_MODEL, D_FF)),
                        _const_spec((D_FF, D_MODEL))]
    outs = pl.pallas_call(
        functools.partial(_ffn_kernel, sub=min(tm, FFN_SUB_ROWS), n_cast=len(cast),
                          stage_weights=stage_weights),
        out_shape=[jax.ShapeDtypeStruct((n, D_MODEL), jnp.float32),
                   jax.ShapeDtypeStruct(sample_out_shape, jnp.float32)] + cast_shapes,
        grid=(nt + 1,),
        in_specs=[pl.BlockSpec((tm, D_MODEL), blk), _const_spec(xs.shape)] + weight_specs
        + [_const_spec((1, D_MODEL)), _const_spec((1, D_MODEL))] + cast_in,
        out_specs=[pl.BlockSpec((tm, D_MODEL), blk),
                   pl.BlockSpec(sample_out_shape, lambda i: sample_zeros)] + cast_out,
        scratch_shapes=scratch,
        compiler_params=pltpu.CompilerParams(
            dimension_semantics=("arbitrary",), vmem_limit_bytes=VMEM_LIMIT_BYTES),
        name="ffn",
    )(xp, xs, wg, wu, wd, g, b, *cast)
    return outs[0], outs[1], outs[2:]


def _proj(xb, w_in_ref, off, width):
    return jnp.dot(xb, w_in_ref[:, off:off + width], preferred_element_type=jnp.float32)


def _shift_rows(ext, k, halo):
    return pltpu.roll(ext, k, 0)[halo:]


def _mixer_prompt_kernel(x_ref, w_in_ref, conv_w_ref, pool_w_ref, pool_scale_ref, w_out_ref,
                         g_ref, b_ref, *rest, sub, n_cast):
    cast_src = rest[:n_cast]
    o_ref, conv_out_ref, pool_out_ref, *cast_dst = rest[n_cast:2 * n_cast + 3]
    u_carry, p_carry, m_buf = rest[2 * n_cast + 3:]
    t = pl.program_id(1)
    tm = x_ref.shape[1]

    @pl.when(t == 0)
    def _():
        u_carry[...] = jnp.zeros(u_carry.shape, jnp.float32)
        p_carry[...] = jnp.zeros(p_carry.shape, jnp.float32)

    u_prev = u_carry[...]
    p_prev = p_carry[...]

    def project_out_and_norm(rows):
        out = jnp.dot(m_buf[rows, :], w_out_ref[...], preferred_element_type=jnp.float32)
        o_ref[0, rows, :] = _layer_norm(ALPHA * x_ref[0, rows, :] + out, g_ref[...], b_ref[...])

    pending_rows = None
    for s in range(tm // sub):
        rows = slice(s * sub, (s + 1) * sub)
        x = x_ref[0, rows, :]
        xb = x.astype(jnp.bfloat16)

        p = _proj(xb, w_in_ref, OFF_P, D_POOL)
        p_ext = jnp.concatenate([p_prev, p], axis=0)
        pos = t * tm + s * sub + lax.broadcasted_iota(jnp.int32, (sub, POOL_GROUP), 0)

        u_tails = []
        for j in range(N_POOL_GROUPS):
            csl = slice(j * POOL_OUT_GROUP, (j + 1) * POOL_OUT_GROUP)
            c_g = _proj(xb, w_in_ref, OFF_C + j * POOL_OUT_GROUP, POOL_OUT_GROUP)
            h = _proj(xb, w_in_ref, OFF_H + j * POOL_OUT_GROUP, POOL_OUT_GROUP)
            u = c_g * h
            u_ext = jnp.concatenate([u_prev[:, csl], u], axis=0)
            u_tails.append(u[sub - CONV_HALO:, :])
            cw = conv_w_ref[:, csl]
            conv = (cw[0:1] * _shift_rows(u_ext, 2, CONV_HALO)
                    + cw[1:2] * _shift_rows(u_ext, 1, CONV_HALO)
                    + cw[2:3] * u)
            b_g = _proj(xb, w_in_ref, OFF_B + j * POOL_OUT_GROUP, POOL_OUT_GROUP)
            y_a = b_g * conv

            psl = slice(j * POOL_GROUP, (j + 1) * POOL_GROUP)
            w = POOL_WINDOWS[j]
            acc = p_ext[:, psl]
            span = 1
            while span < w:
                acc = acc + pltpu.roll(acc, span, 0)
                span *= 2
            inv_cnt = 1.0 / jnp.minimum(pos + 1, w).astype(jnp.float32)
            q = acc[POOL_HALO:] * inv_cnt - p[:, psl]
            y_p = jnp.dot(q.astype(jnp.bfloat16), pool_w_ref[j],
                          preferred_element_type=jnp.float32) * pool_scale_ref[:, csl]

            g_a = _proj(xb, w_in_ref, OFF_GA + j * POOL_OUT_GROUP, POOL_OUT_GROUP)
            g_p = _proj(xb, w_in_ref, OFF_GP + j * POOL_OUT_GROUP, POOL_OUT_GROUP)
            m = jax.nn.sigmoid(g_a) * y_a + jax.nn.sigmoid(g_p) * y_p
            m_buf[rows, csl] = m.astype(jnp.bfloat16)

            if j == 0 and pending_rows is not None:
                project_out_and_norm(pending_rows)

        pending_rows = rows
        u_prev = jnp.concatenate(u_tails, axis=1)
        p_prev = p_ext[sub:, :]

    project_out_and_norm(pending_rows)
    _run_casts(cast_src, cast_dst)

    u_carry[...] = u_prev
    p_carry[...] = p_prev
    conv_out_ref[0] = u_carry[CONV_HALO - (CONV_WIDTH - 1):CONV_HALO, :]
    pool_out_ref[0] = p_carry[POOL_HALO - POOL_STATE:POOL_HALO, :]


def _mixer_prompt(x, w_in, conv_w, pool_w, pool_scale, w_out, g, b, tm, cast=()):
    bsz, seq, _ = x.shape
    tiles = seq // tm
    cast_in, cast_out, cast_shapes = _cast_specs(cast, bsz * tiles, lambda s, t: s * tiles + t)
    kern = functools.partial(_mixer_prompt_kernel, sub=min(tm, SUB_ROWS), n_cast=len(cast))
    outs = pl.pallas_call(
        kern,
        out_shape=[
            jax.ShapeDtypeStruct((bsz, seq, D_MODEL), jnp.float32),
            jax.ShapeDtypeStruct((bsz, CONV_WIDTH - 1, D_CONV), jnp.float32),
            jax.ShapeDtypeStruct((bsz, POOL_STATE, D_POOL), jnp.float32),
        ] + cast_shapes,
        grid=(bsz, tiles),
        in_specs=[
            pl.BlockSpec((1, tm, D_MODEL), lambda s, t: (s, t, 0)),
            _const_spec((D_MODEL, D_IN)),
            _const_spec((CONV_WIDTH, D_CONV)),
            _const_spec((N_POOL_GROUPS, POOL_GROUP, POOL_OUT_GROUP)),
            _const_spec((1, D_MODEL)),
            _const_spec((D_MODEL, D_MODEL)),
            _const_spec((1, D_MODEL)),
            _const_spec((1, D_MODEL)),
        ] + cast_in,
        out_specs=[
            pl.BlockSpec((1, tm, D_MODEL), lambda s, t: (s, t, 0)),
            pl.BlockSpec((1, CONV_WIDTH - 1, D_CONV), lambda s, t: (s, 0, 0)),
            pl.BlockSpec((1, POOL_STATE, D_POOL), lambda s, t: (s, 0, 0)),
        ] + cast_out,
        scratch_shapes=[
            pltpu.VMEM((CONV_HALO, D_CONV), jnp.float32),
            pltpu.VMEM((POOL_HALO, D_POOL), jnp.float32),
            pltpu.VMEM((tm, D_MODEL), jnp.bfloat16),
        ],
        compiler_params=pltpu.CompilerParams(
            dimension_semantics=("arbitrary", "arbitrary"), vmem_limit_bytes=VMEM_LIMIT_BYTES),
        name="mixer_prompt",
    )(x, w_in, conv_w, pool_w, pool_scale, w_out, g, b, *cast)
    return outs[0], outs[1], outs[2], outs[3:]


def _mixer_sample_kernel(x_ref, cs_ref, ps_ref, w_in_ref, conv_w_ref, pool_w_ref, pool_scale_ref,
                         w_out_ref, g_ref, b_ref, o_ref, conv_out_ref, pool_out_ref, m_buf):
    x = x_ref[...]
    xb = x.astype(jnp.bfloat16)
    p = _proj(xb, w_in_ref, OFF_P, D_POOL)

    for j in range(N_POOL_GROUPS):
        csl = slice(j * POOL_OUT_GROUP, (j + 1) * POOL_OUT_GROUP)
        c_g = _proj(xb, w_in_ref, OFF_C + j * POOL_OUT_GROUP, POOL_OUT_GROUP)
        h = _proj(xb, w_in_ref, OFF_H + j * POOL_OUT_GROUP, POOL_OUT_GROUP)
        u = c_g * h
        conv_out_ref[:, CONV_WIDTH - 2:CONV_WIDTH - 1, csl] = u[:, None, :]
        cw = conv_w_ref[:, csl]
        conv = cw[0:1] * cs_ref[:, 0, csl] + cw[1:2] * cs_ref[:, 1, csl] + cw[2:3] * u
        b_g = _proj(xb, w_in_ref, OFF_B + j * POOL_OUT_GROUP, POOL_OUT_GROUP)
        y_a = b_g * conv

        w = POOL_WINDOWS[j]
        psl = slice(j * POOL_GROUP, (j + 1) * POOL_GROUP)
        pj = p[:, psl]
        wsum = pj
        for k in range(POOL_STATE - (w - 1), POOL_STATE):
            wsum = wsum + ps_ref[k, :, psl]
        q = wsum * (1.0 / w) - pj
        y_p = jnp.dot(q.astype(jnp.bfloat16), pool_w_ref[j],
                      preferred_element_type=jnp.float32) * pool_scale_ref[:, csl]

        g_a = _proj(xb, w_in_ref, OFF_GA + j * POOL_OUT_GROUP, POOL_OUT_GROUP)
        g_p = _proj(xb, w_in_ref, OFF_GP + j * POOL_OUT_GROUP, POOL_OUT_GROUP)
        m = jax.nn.sigmoid(g_a) * y_a + jax.nn.sigmoid(g_p) * y_p
        m_buf[:, csl] = m.astype(jnp.bfloat16)

    out = jnp.dot(m_buf[...], w_out_ref[...], preferred_element_type=jnp.float32)
    o_ref[...] = _layer_norm(ALPHA * x + out, g_ref[...], b_ref[...])

    conv_out_ref[:, 0:CONV_WIDTH - 2, :] = cs_ref[:, 1:CONV_WIDTH - 1, :]
    pool_out_ref[0:POOL_STATE - 1] = ps_ref[1:POOL_STATE]
    pool_out_ref[POOL_STATE - 1] = p


def _mixer_sample(x2d, cs, ps, w_in, conv_w, pool_w, pool_scale, w_out, g, b):
    nb = x2d.shape[0]
    shapes = [x2d.shape, cs.shape, ps.shape, w_in.shape, conv_w.shape, pool_w.shape,
              pool_scale.shape, w_out.shape, g.shape, b.shape]
    return pl.pallas_call(
        _mixer_sample_kernel,
        out_shape=(
            jax.ShapeDtypeStruct((nb, D_MODEL), jnp.float32),
            jax.ShapeDtypeStruct(cs.shape, jnp.float32),
            jax.ShapeDtypeStruct(ps.shape, jnp.float32),
        ),
        grid=(1,),
        in_specs=[_const_spec(s) for s in shapes],
        out_specs=(
            pl.BlockSpec((nb, D_MODEL), lambda i: (0, 0)),
            pl.BlockSpec(cs.shape, lambda i: (0, 0, 0)),
            pl.BlockSpec(ps.shape, lambda i: (0, 0, 0)),
        ),
        scratch_shapes=[pltpu.VMEM((nb, D_MODEL), jnp.bfloat16)],
        compiler_params=pltpu.CompilerParams(
            dimension_semantics=("arbitrary",), vmem_limit_bytes=VMEM_LIMIT_BYTES),
        name="mixer_sample",
    )(x2d, cs, ps, w_in, conv_w, pool_w, pool_scale, w_out, g, b)


def kernel(x_prompt, x_sample, state_conv, state_pool, ln1_g, ln1_b, ffn1_wg, ffn1_wu, ffn1_wd,
           w_in, conv_w, pool_w, pool_scale, w_out, ln2_g, ln2_b,
           ffn2_wg, ffn2_wu, ffn2_wd, ln3_g, ln3_b):
    bsz, seq, _ = x_prompt.shape
    nb = x_sample.shape[0]
    assert x_sample.shape[1] == 1 and DEPTH == 1
    tm = 1024
    tm_ffn = 1024

    yp = x_prompt.reshape(bsz * seq, D_MODEL)
    l = 0
    bf = jnp.bfloat16
    wg1, wu1, wd1 = ffn1_wg[l].astype(bf), ffn1_wu[l].astype(bf), ffn1_wd[l].astype(bf)
    g1, b1 = ln1_g[l][None], ln1_b[l][None]
    g2, b2 = ln2_g[l][None], ln2_b[l][None]
    g3, b3 = ln3_g[l][None], ln3_b[l][None]
    scale = pool_scale[l][None]

    pool_w2d = pool_w[l].reshape(N_POOL_GROUPS * POOL_GROUP, POOL_OUT_GROUP)
    yp, ys, (w_in_b, w_out_b, pool_w_b) = _ffn(yp, x_sample.reshape(nb, D_MODEL), wg1, wu1, wd1,
                                               g1, b1, tm_ffn, cast=(w_in[l], w_out[l], pool_w2d))
    pool_w_b = pool_w_b.reshape(N_POOL_GROUPS, POOL_GROUP, POOL_OUT_GROUP)
    yp, conv_p, pool_p, (wg2, wu2, wd2) = _mixer_prompt(
        yp.reshape(bsz, seq, D_MODEL), w_in_b, conv_w[l], pool_w_b, scale, w_out_b, g2, b2, tm,
        cast=(ffn2_wg[l], ffn2_wu[l], ffn2_wd[l]))
    ps_rows = jnp.transpose(state_pool[l], (1, 0, 2))
    ys, conv_s, pool_s = _mixer_sample(ys, state_conv[l], ps_rows, w_in_b, conv_w[l],
                                       pool_w_b, scale, w_out_b, g2, b2)
    yp, ys, _ = _ffn(yp.reshape(bsz * seq, D_MODEL), ys, wg2, wu2, wd2, g3, b3, tm_ffn,
                     sample_out_shape=x_sample.shape)

    return (yp.reshape(bsz, seq, D_MODEL),
            ys,
            conv_p[None],
            pool_p[None],
            conv_s[None],
            jnp.transpose(pool_s, (1, 0, 2))[None])
```

```python
import functools

import jax
import jax.numpy as jnp
from jax import lax
from jax.experimental import pallas as pl
from jax.experimental.pallas import tpu as pltpu

D_MODEL = 1024
D_CONV = D_MODEL
D_POOL = D_MODEL // 2
N_POOL_GROUPS = 4
POOL_GROUP = D_POOL // N_POOL_GROUPS
POOL_OUT_GROUP = D_MODEL // N_POOL_GROUPS
POOL_WINDOWS = (2, 4, 8, 16)
POOL_STATE = max(POOL_WINDOWS) - 1
CONV_WIDTH = 3
D_FF = 2816
DEPTH = 1
ALPHA = (2.0 * DEPTH) ** 0.25
LN_EPS = 1e-5
D_IN = 3 * D_CONV + D_POOL + 2 * D_MODEL

OFF_C = 0
OFF_B = D_CONV
OFF_H = 2 * D_CONV
OFF_P = 3 * D_CONV
OFF_GA = 3 * D_CONV + D_POOL
OFF_GP = 3 * D_CONV + D_POOL + D_MODEL

SUB_ROWS = 256
FFN_SUB_ROWS = 256
FF_CHUNK = 256
STAGE_CHUNKS = 16
POOL_HALO = 16
CONV_HALO = 8
VMEM_LIMIT_BYTES = 52 * 1024 * 1024


def _layer_norm(v, g, b):
    mu = jnp.mean(v, axis=-1, keepdims=True)
    d = v - mu
    var = jnp.mean(d * d, axis=-1, keepdims=True)
    return d * lax.rsqrt(var + LN_EPS) * g + b


def _const_spec(shape):
    zeros = (0,) * len(shape)
    return pl.BlockSpec(shape, lambda *_: zeros, pipeline_mode=pl.Buffered(1))


def _cast_specs(arrays, n_steps, step_index):
    in_specs, out_specs, out_shapes = [], [], []
    for a in arrays:
        rows, cols = a.shape
        assert rows % (n_steps * 16) == 0
        spec = pl.BlockSpec((rows // n_steps, cols), lambda *idx: (step_index(*idx), 0))
        in_specs.append(spec)
        out_specs.append(spec)
        out_shapes.append(jax.ShapeDtypeStruct(a.shape, jnp.bfloat16))
    return in_specs, out_specs, out_shapes


def _run_casts(src_refs, dst_refs):
    for src, dst in zip(src_refs, dst_refs):
        dst[...] = src[...].astype(jnp.bfloat16)


def _ffn_rows(x, wg_ref, wu_ref, wd_ref, g_ref, b_ref, h_view):
    xb = x.astype(jnp.bfloat16)
    for c in range(D_FF // FF_CHUNK):
        sl = slice(c * FF_CHUNK, (c + 1) * FF_CHUNK)
        gate = jnp.dot(xb, wg_ref[:, sl], preferred_element_type=jnp.float32)
        up = jnp.dot(xb, wu_ref[:, sl], preferred_element_type=jnp.float32)
        h_view[:, sl] = (gate * jax.nn.sigmoid(gate) * up).astype(jnp.bfloat16)
    y = jnp.dot(h_view[...], wd_ref[...], preferred_element_type=jnp.float32)
    return _layer_norm(ALPHA * x + 0.5 * y, g_ref[...], b_ref[...])


def _stage_cast(w_hbm, w_vmem, stage, sem):
    chunk = stage.shape[1]
    n = w_hbm.shape[0] // chunk

    def rows(k):
        return pl.ds(pl.multiple_of(k * chunk, chunk), chunk)

    def copy(k):
        return pltpu.make_async_copy(w_hbm.at[rows(k), :], stage.at[k % 2], sem.at[k % 2])

    copy(0).start()

    def body(k, carry):
        @pl.when(k + 1 < n)
        def _():
            copy(k + 1).start()

        copy(k).wait()
        w_vmem[rows(k), :] = stage[k % 2].astype(jnp.bfloat16)
        return carry

    lax.fori_loop(0, n, body, 0)


def _ffn_kernel(xp_ref, xs_ref, wg_ref, wu_ref, wd_ref, g_ref, b_ref, *rest, sub, n_cast,
                stage_weights):
    cast_src = rest[:n_cast]
    op_ref, os_ref, *cast_dst = rest[n_cast:2 * n_cast + 2]
    h_ref, *stage_scratch = rest[2 * n_cast + 2:]
    i = pl.program_id(0)
    last = pl.num_programs(0) - 1

    if stage_weights:
        wg_v, wu_v, wd_v, stage_up, stage_down, sem = stage_scratch

        @pl.when(i == 0)
        def _():
            _stage_cast(wg_ref, wg_v, stage_up, sem)
            _stage_cast(wu_ref, wu_v, stage_up, sem)
            _stage_cast(wd_ref, wd_v, stage_down, sem)

        wg_ref, wu_ref, wd_ref = wg_v, wu_v, wd_v
    weights = (wg_ref, wu_ref, wd_ref, g_ref, b_ref)

    @pl.when(i < last)
    def _():
        for s in range(xp_ref.shape[0] // sub):
            rows = pl.ds(s * sub, sub)
            op_ref[rows, :] = _ffn_rows(xp_ref[rows, :], *weights, h_ref.at[rows, :])
        _run_casts(cast_src, cast_dst)

    @pl.when(i == last)
    def _():
        ns = xs_ref.shape[0]
        xs = xs_ref[...] if len(xs_ref.shape) == 2 else xs_ref[:, 0, :]
        ys = _ffn_rows(xs, *weights, h_ref.at[pl.ds(0, ns), :])
        os_ref[...] = ys if len(os_ref.shape) == 2 else ys[:, None, :]


def _ffn(xp, xs, wg, wu, wd, g, b, tm, cast=(), sample_out_shape=None):
    n, ns = xp.shape[0], xs.shape[0]
    nt = n // tm
    assert ns <= tm
    sample_out_shape = sample_out_shape or (ns, D_MODEL)
    sample_zeros = (0,) * len(sample_out_shape)
    blk = lambda i: (jnp.minimum(i, nt - 1), 0)
    cast_in, cast_out, cast_shapes = _cast_specs(cast, nt, lambda i: jnp.minimum(i, nt - 1))
    stage_weights = wg.dtype == jnp.float32
    assert wu.dtype == wg.dtype and wd.dtype == wg.dtype
    scratch = [pltpu.VMEM((tm, D_FF), jnp.bfloat16)]
    if stage_weights:
        weight_specs = [pl.BlockSpec(memory_space=pl.ANY)] * 3
        scratch += [
            pltpu.VMEM((D_MODEL, D_FF), jnp.bfloat16),
            pltpu.VMEM((D_MODEL, D_FF), jnp.bfloat16),
            pltpu.VMEM((D_FF, D_MODEL), jnp.bfloat16),
            pltpu.VMEM((2, D_MODEL // STAGE_CHUNKS, D_FF), jnp.float32),
            pltpu.VMEM((2, D_FF // STAGE_CHUNKS, D_MODEL), jnp.float32),
            pltpu.SemaphoreType.DMA((2,)),
        ]
    else:
        weight_specs = [_const_spec((D_MODEL, D_FF)), _const_spec((D_MODEL, D_FF)),
                        _const_spec((D_FF, D_MODEL))]
    outs = pl.pallas_call(
        functools.partial(_ffn_kernel, sub=min(tm, FFN_SUB_ROWS), n_cast=len(cast),
                          stage_weights=stage_weights),
        out_shape=[jax.ShapeDtypeStruct((n, D_MODEL), jnp.float32),
                   jax.ShapeDtypeStruct(sample_out_shape, jnp.float32)] + cast_shapes,
        grid=(nt + 1,),
        in_specs=[pl.BlockSpec((tm, D_MODEL), blk), _const_spec(xs.shape)] + weight_specs
        + [_const_spec((1, D_MODEL)), _const_spec((1, D_MODEL))] + cast_in,
        out_specs=[pl.BlockSpec((tm, D_MODEL), blk),
                   pl.BlockSpec(sample_out_shape, lambda i: sample_zeros)] + cast_out,
        scratch_shapes=scratch,
        compiler_params=pltpu.CompilerParams(
            dimension_semantics=("arbitrary",), vmem_limit_bytes=VMEM_LIMIT_BYTES),
        name="ffn",
    )(xp, xs, wg, wu, wd, g, b, *cast)
    return outs[0], outs[1], outs[2:]


def _proj(xb, w_in_ref, off, width):
    return jnp.dot(xb, w_in_ref[:, off:off + width], preferred_element_type=jnp.float32)


def _shift_rows(ext, k, halo):
    return pltpu.roll(ext, k, 0)[halo:]


def _mixer_prompt_kernel(x_ref, w_in_ref, conv_w_ref, pool_w_ref, pool_scale_ref, w_out_ref,
                         g_ref, b_ref, *rest, sub, n_cast):
    cast_src = rest[:n_cast]
    o_ref, conv_out_ref, pool_out_ref, *cast_dst = rest[n_cast:2 * n_cast + 3]
    u_carry, p_carry, m_buf = rest[2 * n_cast + 3:]
    t = pl.program_id(1)
    tm = x_ref.shape[1]

    @pl.when(t == 0)
    def _():
        u_carry[...] = jnp.zeros(u_carry.shape, jnp.float32)
        p_carry[...] = jnp.zeros(p_carry.shape, jnp.float32)

    u_prev = u_carry[...]
    p_prev = p_carry[...]

    def project_out_and_norm(rows):
        out = jnp.dot(m_buf[rows, :], w_out_ref[...], preferred_element_type=jnp.float32)
        o_ref[0, rows, :] = _layer_norm(ALPHA * x_ref[0, rows, :] + out, g_ref[...], b_ref[...])

    pending_rows = None
    for s in range(tm // sub):
        rows = slice(s * sub, (s + 1) * sub)
        x = x_ref[0, rows, :]
        xb = x.astype(jnp.bfloat16)

        p = _proj(xb, w_in_ref, OFF_P, D_POOL)
        p_ext = jnp.concatenate([p_prev, p], axis=0)
        pos = t * tm + s * sub + lax.broadcasted_iota(jnp.int32, (sub, POOL_GROUP), 0)

        u_tails = []
        for j in range(N_POOL_GROUPS):
            csl = slice(j * POOL_OUT_GROUP, (j + 1) * POOL_OUT_GROUP)
            c_g = _proj(xb, w_in_ref, OFF_C + j * POOL_OUT_GROUP, POOL_OUT_GROUP)
            h = _proj(xb, w_in_ref, OFF_H + j * POOL_OUT_GROUP, POOL_OUT_GROUP)
            u = c_g * h
            u_ext = jnp.concatenate([u_prev[:, csl], u], axis=0)
            u_tails.append(u[sub - CONV_HALO:, :])
            cw = conv_w_ref[:, csl]
            conv = (cw[0:1] * _shift_rows(u_ext, 2, CONV_HALO)
                    + cw[1:2] * _shift_rows(u_ext, 1, CONV_HALO)
                    + cw[2:3] * u)
            b_g = _proj(xb, w_in_ref, OFF_B + j * POOL_OUT_GROUP, POOL_OUT_GROUP)
            y_a = b_g * conv

            psl = slice(j * POOL_GROUP, (j + 1) * POOL_GROUP)
            w = POOL_WINDOWS[j]
            acc = p_ext[:, psl]
            span = 1
            while span < w:
                acc = acc + pltpu.roll(acc, span, 0)
                span *= 2
            inv_cnt = 1.0 / jnp.minimum(pos + 1, w).astype(jnp.float32)
            q = acc[POOL_HALO:] * inv_cnt - p[:, psl]
            y_p = jnp.dot(q.astype(jnp.bfloat16), pool_w_ref[j],
                          preferred_element_type=jnp.float32) * pool_scale_ref[:, csl]

            g_a = _proj(xb, w_in_ref, OFF_GA + j * POOL_OUT_GROUP, POOL_OUT_GROUP)
            g_p = _proj(xb, w_in_ref, OFF_GP + j * POOL_OUT_GROUP, POOL_OUT_GROUP)
            m = jax.nn.sigmoid(g_a) * y_a + jax.nn.sigmoid(g_p) * y_p
            m_buf[rows, csl] = m.astype(jnp.bfloat16)

            if j == 0 and pending_rows is not None:
                project_out_and_norm(pending_rows)

        pending_rows = rows
        u_prev = jnp.concatenate(u_tails, axis=1)
        p_prev = p_ext[sub:, :]

    project_out_and_norm(pending_rows)
    _run_casts(cast_src, cast_dst)

    u_carry[...] = u_prev
    p_carry[...] = p_prev
    conv_out_ref[0] = u_carry[CONV_HALO - (CONV_WIDTH - 1):CONV_HALO, :]
    pool_out_ref[0] = p_carry[POOL_HALO - POOL_STATE:POOL_HALO, :]


def _mixer_prompt(x, w_in, conv_w, pool_w, pool_scale, w_out, g, b, tm, cast=()):
    bsz, seq, _ = x.shape
    tiles = seq // tm
    cast_in, cast_out, cast_shapes = _cast_specs(cast, bsz * tiles, lambda s, t: s * tiles + t)
    kern = functools.partial(_mixer_prompt_kernel, sub=min(tm, SUB_ROWS), n_cast=len(cast))
    outs = pl.pallas_call(
        kern,
        out_shape=[
            jax.ShapeDtypeStruct((bsz, seq, D_MODEL), jnp.float32),
            jax.ShapeDtypeStruct((bsz, CONV_WIDTH - 1, D_CONV), jnp.float32),
            jax.ShapeDtypeStruct((bsz, POOL_STATE, D_POOL), jnp.float32),
        ] + cast_shapes,
        grid=(bsz, tiles),
        in_specs=[
            pl.BlockSpec((1, tm, D_MODEL), lambda s, t: (s, t, 0)),
            _const_spec((D_MODEL, D_IN)),
            _const_spec((CONV_WIDTH, D_CONV)),
            _const_spec((N_POOL_GROUPS, POOL_GROUP, POOL_OUT_GROUP)),
            _const_spec((1, D_MODEL)),
            _const_spec((D_MODEL, D_MODEL)),
            _const_spec((1, D_MODEL)),
            _const_spec((1, D_MODEL)),
        ] + cast_in,
        out_specs=[
            pl.BlockSpec((1, tm, D_MODEL), lambda s, t: (s, t, 0)),
            pl.BlockSpec((1, CONV_WIDTH - 1, D_CONV), lambda s, t: (s, 0, 0)),
            pl.BlockSpec((1, POOL_STATE, D_POOL), lambda s, t: (s, 0, 0)),
        ] + cast_out,
        scratch_shapes=[
            pltpu.VMEM((CONV_HALO, D_CONV), jnp.float32),
            pltpu.VMEM((POOL_HALO, D_POOL), jnp.float32),
            pltpu.VMEM((tm, D_MODEL), jnp.bfloat16),
        ],
        compiler_params=pltpu.CompilerParams(
            dimension_semantics=("arbitrary", "arbitrary"), vmem_limit_bytes=VMEM_LIMIT_BYTES),
        name="mixer_prompt",
    )(x, w_in, conv_w, pool_w, pool_scale, w_out, g, b, *cast)
    return outs[0], outs[1], outs[2], outs[3:]


def _mixer_sample_kernel(x_ref, cs_ref, ps_ref, w_in_ref, conv_w_ref, pool_w_ref, pool_scale_ref,
                         w_out_ref, g_ref, b_ref, o_ref, conv_out_ref, pool_out_ref, m_buf):
    x = x_ref[...]
    xb = x.astype(jnp.bfloat16)
    p = _proj(xb, w_in_ref, OFF_P, D_POOL)

    for j in range(N_POOL_GROUPS):
        csl = slice(j * POOL_OUT_GROUP, (j + 1) * POOL_OUT_GROUP)
        c_g = _proj(xb, w_in_ref, OFF_C + j * POOL_OUT_GROUP, POOL_OUT_GROUP)
        h = _proj(xb, w_in_ref, OFF_H + j * POOL_OUT_GROUP, POOL_OUT_GROUP)
        u = c_g * h
        conv_out_ref[:, CONV_WIDTH - 2:CONV_WIDTH - 1, csl] = u[:, None, :]
        cw = conv_w_ref[:, csl]
        conv = cw[0:1] * cs_ref[:, 0, csl] + cw[1:2] * cs_ref[:, 1, csl] + cw[2:3] * u
        b_g = _proj(xb, w_in_ref, OFF_B + j * POOL_OUT_GROUP, POOL_OUT_GROUP)
        y_a = b_g * conv

        w = POOL_WINDOWS[j]
        psl = slice(j * POOL_GROUP, (j + 1) * POOL_GROUP)
        pj = p[:, psl]
        wsum = pj
        for k in range(POOL_STATE - (w - 1), POOL_STATE):
            wsum = wsum + ps_ref[k, :, psl]
        q = wsum * (1.0 / w) - pj
        y_p = jnp.dot(q.astype(jnp.bfloat16), pool_w_ref[j],
                      preferred_element_type=jnp.float32) * pool_scale_ref[:, csl]

        g_a = _proj(xb, w_in_ref, OFF_GA + j * POOL_OUT_GROUP, POOL_OUT_GROUP)
        g_p = _proj(xb, w_in_ref, OFF_GP + j * POOL_OUT_GROUP, POOL_OUT_GROUP)
        m = jax.nn.sigmoid(g_a) * y_a + jax.nn.sigmoid(g_p) * y_p
        m_buf[:, csl] = m.astype(jnp.bfloat16)

    out = jnp.dot(m_buf[...], w_out_ref[...], preferred_element_type=jnp.float32)
    o_ref[...] = _layer_norm(ALPHA * x + out, g_ref[...], b_ref[...])

    conv_out_ref[:, 0:CONV_WIDTH - 2, :] = cs_ref[:, 1:CONV_WIDTH - 1, :]
    pool_out_ref[0:POOL_STATE - 1] = ps_ref[1:POOL_STATE]
    pool_out_ref[POOL_STATE - 1] = p


def _mixer_sample(x2d, cs, ps, w_in, conv_w, pool_w, pool_scale, w_out, g, b):
    nb = x2d.shape[0]
    shapes = [x2d.shape, cs.shape, ps.shape, w_in.shape, conv_w.shape, pool_w.shape,
              pool_scale.shape, w_out.shape, g.shape, b.shape]
    return pl.pallas_call(
        _mixer_sample_kernel,
        out_shape=(
            jax.ShapeDtypeStruct((nb, D_MODEL), jnp.float32),
            jax.ShapeDtypeStruct(cs.shape, jnp.float32),
            jax.ShapeDtypeStruct(ps.shape, jnp.float32),
        ),
        grid=(1,),
        in_specs=[_const_spec(s) for s in shapes],
        out_specs=(
            pl.BlockSpec((nb, D_MODEL), lambda i: (0, 0)),
            pl.BlockSpec(cs.shape, lambda i: (0, 0, 0)),
            pl.BlockSpec(ps.shape, lambda i: (0, 0, 0)),
        ),
        scratch_shapes=[pltpu.VMEM((nb, D_MODEL), jnp.bfloat16)],
        compiler_params=pltpu.CompilerParams(
            dimension_semantics=("arbitrary",), vmem_limit_bytes=VMEM_LIMIT_BYTES),
        name="mixer_sample",
    )(x2d, cs, ps, w_in, conv_w, pool_w, pool_scale, w_out, g, b)


def kernel(x_prompt, x_sample, state_conv, state_pool, ln1_g, ln1_b, ffn1_wg, ffn1_wu, ffn1_wd,
           w_in, conv_w, pool_w, pool_scale, w_out, ln2_g, ln2_b,
           ffn2_wg, ffn2_wu, ffn2_wd, ln3_g, ln3_b):
    bsz, seq, _ = x_prompt.shape
    nb = x_sample.shape[0]
    assert x_sample.shape[1] == 1 and DEPTH == 1
    tm = 1024
    tm_ffn = 1024

    yp = x_prompt.reshape(bsz * seq, D_MODEL)
    l = 0
    wg1, wu1, wd1 = ffn1_wg[l], ffn1_wu[l], ffn1_wd[l]
    g1, b1 = ln1_g[l][None], ln1_b[l][None]
    g2, b2 = ln2_g[l][None], ln2_b[l][None]
    g3, b3 = ln3_g[l][None], ln3_b[l][None]
    scale = pool_scale[l][None]

    pool_w2d = pool_w[l].reshape(N_POOL_GROUPS * POOL_GROUP, POOL_OUT_GROUP)
    yp, ys, (w_in_b, w_out_b, pool_w_b) = _ffn(yp, x_sample.reshape(nb, D_MODEL), wg1, wu1, wd1,
                                               g1, b1, tm_ffn, cast=(w_in[l], w_out[l], pool_w2d))
    pool_w_b = pool_w_b.reshape(N_POOL_GROUPS, POOL_GROUP, POOL_OUT_GROUP)
    yp, conv_p, pool_p, (wg2, wu2, wd2) = _mixer_prompt(
        yp.reshape(bsz, seq, D_MODEL), w_in_b, conv_w[l], pool_w_b, scale, w_out_b, g2, b2, tm,
        cast=(ffn2_wg[l], ffn2_wu[l], ffn2_wd[l]))
    ps_rows = jnp.transpose(state_pool[l], (1, 0, 2))
    ys, conv_s, pool_s = _mixer_sample(ys, state_conv[l], ps_rows, w_in_b, conv_w[l],
                                       pool_w_b, scale, w_out_b, g2, b2)
    yp, ys, _ = _ffn(yp.reshape(bsz * seq, D_MODEL), ys, wg2, wu2, wd2, g3, b3, tm_ffn,
                     sample_out_shape=x_sample.shape)

    return (yp.reshape(bsz, seq, D_MODEL),
            ys,
            conv_p[None],
            pool_p[None],
            conv_s[None],
            jnp.transpose(pool_s, (1, 0, 2))[None])
```

```python
import functools

import jax
import jax.numpy as jnp
from jax import lax
from jax.experimental import pallas as pl
from jax.experimental.pallas import tpu as pltpu

D_MODEL = 1024
D_CONV = D_MODEL
D_POOL = D_MODEL // 2
N_POOL_GROUPS = 4
POOL_GROUP = D_POOL // N_POOL_GROUPS
POOL_OUT_GROUP = D_MODEL // N_POOL_GROUPS
POOL_WINDOWS = (2, 4, 8, 16)
POOL_STATE = max(POOL_WINDOWS) - 1
CONV_WIDTH = 3
D_FF = 2816
DEPTH = 1
ALPHA = (2.0 * DEPTH) ** 0.25
LN_EPS = 1e-5
D_IN = 3 * D_CONV + D_POOL + 2 * D_MODEL

OFF_C = 0
OFF_B = D_CONV
OFF_H = 2 * D_CONV
OFF_P = 3 * D_CONV
OFF_GA = 3 * D_CONV + D_POOL
OFF_GP = 3 * D_CONV + D_POOL + D_MODEL

SUB_ROWS = 256
FFN_SUB_ROWS = 256
FF_CHUNK = 256
STAGE_CHUNKS = 16
STAGE_SLOTS = 4
POOL_HALO = 16
CONV_HALO = 8
VMEM_LIMIT_BYTES = 52 * 1024 * 1024


def _layer_norm(v, g, b):
    mu = jnp.mean(v, axis=-1, keepdims=True)
    d = v - mu
    var = jnp.mean(d * d, axis=-1, keepdims=True)
    return d * lax.rsqrt(var + LN_EPS) * g + b


def _const_spec(shape):
    zeros = (0,) * len(shape)
    return pl.BlockSpec(shape, lambda *_: zeros, pipeline_mode=pl.Buffered(1))


def _cast_specs(arrays, n_steps, step_index):
    in_specs, out_specs, out_shapes = [], [], []
    for a in arrays:
        rows, cols = a.shape
        assert rows % (n_steps * 16) == 0
        spec = pl.BlockSpec((rows // n_steps, cols), lambda *idx: (step_index(*idx), 0))
        in_specs.append(spec)
        out_specs.append(spec)
        out_shapes.append(jax.ShapeDtypeStruct(a.shape, jnp.bfloat16))
    return in_specs, out_specs, out_shapes


def _run_casts(src_refs, dst_refs):
    for src, dst in zip(src_refs, dst_refs):
        dst[...] = src[...].astype(jnp.bfloat16)


def _ffn_rows(x, wg_ref, wu_ref, wd_ref, g_ref, b_ref, h_view):
    xb = x.astype(jnp.bfloat16)
    for c in range(D_FF // FF_CHUNK):
        sl = slice(c * FF_CHUNK, (c + 1) * FF_CHUNK)
        gate = jnp.dot(xb, wg_ref[:, sl], preferred_element_type=jnp.float32)
        up = jnp.dot(xb, wu_ref[:, sl], preferred_element_type=jnp.float32)
        h_view[:, sl] = (gate * jax.nn.sigmoid(gate) * up).astype(jnp.bfloat16)
    y = jnp.dot(h_view[...], wd_ref[...], preferred_element_type=jnp.float32)
    return _layer_norm(ALPHA * x + 0.5 * y, g_ref[...], b_ref[...])


def _stage_cast(w_hbm, w_vmem, stage, sem):
    slots, chunk = stage.shape[0], stage.shape[1]
    n = w_hbm.shape[0] // chunk
    ahead = slots - 1

    def rows(k):
        return pl.ds(pl.multiple_of(k * chunk, chunk), chunk)

    def copy(k):
        return pltpu.make_async_copy(w_hbm.at[rows(k), :], stage.at[k % slots], sem.at[k % slots])

    for k in range(min(ahead, n)):
        copy(k).start()

    def body(k, carry):
        @pl.when(k + ahead < n)
        def _():
            copy(k + ahead).start()

        copy(k).wait()
        w_vmem[rows(k), :] = stage[k % slots].astype(jnp.bfloat16)
        return carry

    lax.fori_loop(0, n, body, 0)


def _ffn_kernel(xp_ref, xs_ref, wg_ref, wu_ref, wd_ref, g_ref, b_ref, *rest, sub, n_cast,
                stage_weights):
    cast_src = rest[:n_cast]
    op_ref, os_ref, *cast_dst = rest[n_cast:2 * n_cast + 2]
    h_ref, *stage_scratch = rest[2 * n_cast + 2:]
    i = pl.program_id(0)
    last = pl.num_programs(0) - 1

    if stage_weights:
        wg_v, wu_v, wd_v, stage_up, stage_down, sem = stage_scratch

        @pl.when(i == 0)
        def _():
            _stage_cast(wg_ref, wg_v, stage_up, sem)
            _stage_cast(wu_ref, wu_v, stage_up, sem)
            _stage_cast(wd_ref, wd_v, stage_down, sem)

        wg_ref, wu_ref, wd_ref = wg_v, wu_v, wd_v
    weights = (wg_ref, wu_ref, wd_ref, g_ref, b_ref)

    @pl.when(i < last)
    def _():
        for s in range(xp_ref.shape[0] // sub):
            rows = pl.ds(s * sub, sub)
            op_ref[rows, :] = _ffn_rows(xp_ref[rows, :], *weights, h_ref.at[rows, :])
        _run_casts(cast_src, cast_dst)

    @pl.when(i == last)
    def _():
        ns = xs_ref.shape[0]
        xs = xs_ref[...] if len(xs_ref.shape) == 2 else xs_ref[:, 0, :]
        ys = _ffn_rows(xs, *weights, h_ref.at[pl.ds(0, ns), :])
        os_ref[...] = ys if len(os_ref.shape) == 2 else ys[:, None, :]


def _ffn(xp, xs, wg, wu, wd, g, b, tm, cast=(), sample_out_shape=None):
    n, ns = xp.shape[0], xs.shape[0]
    nt = n // tm
    assert ns <= tm
    sample_out_shape = sample_out_shape or (ns, D_MODEL)
    sample_zeros = (0,) * len(sample_out_shape)
    blk = lambda i: (jnp.minimum(i, nt - 1), 0)
    cast_in, cast_out, cast_shapes = _cast_specs(cast, nt, lambda i: jnp.minimum(i, nt - 1))
    stage_weights = wg.dtype == jnp.float32
    assert wu.dtype == wg.dtype and wd.dtype == wg.dtype
    scratch = [pltpu.VMEM((tm, D_FF), jnp.bfloat16)]
    if stage_weights:
        weight_specs = [pl.BlockSpec(memory_space=pl.ANY)] * 3
        scratch += [
            pltpu.VMEM((D_MODEL, D_FF), jnp.bfloat16),
            pltpu.VMEM((D_MODEL, D_FF), jnp.bfloat16),
            pltpu.VMEM((D_FF, D_MODEL), jnp.bfloat16),
            pltpu.VMEM((STAGE_SLOTS, D_MODEL // STAGE_CHUNKS, D_FF), jnp.float32),
            pltpu.VMEM((STAGE_SLOTS, D_FF // STAGE_CHUNKS, D_MODEL), jnp.float32),
            pltpu.SemaphoreType.DMA((STAGE_SLOTS,)),
        ]
    else:
        weight_specs = [_const_spec((D_MODEL, D_FF)), _const_spec((D_MODEL, D_FF)),
                        _const_spec((D_FF, D_MODEL))]
    outs = pl.pallas_call(
        functools.partial(_ffn_kernel, sub=min(tm, FFN_SUB_ROWS), n_cast=len(cast),
                          stage_weights=stage_weights),
        out_shape=[jax.ShapeDtypeStruct((n, D_MODEL), jnp.float32),
                   jax.ShapeDtypeStruct(sample_out_shape, jnp.float32)] + cast_shapes,
        grid=(nt + 1,),
        in_specs=[pl.BlockSpec((tm, D_MODEL), blk), _const_spec(xs.shape)] + weight_specs
        + [_const_spec((1, D_MODEL)), _const_spec((1, D_MODEL))] + cast_in,
        out_specs=[pl.BlockSpec((tm, D_MODEL), blk),
                   pl.BlockSpec(sample_out_shape, lambda i: sample_zeros)] + cast_out,
        scratch_shapes=scratch,
        compiler_params=pltpu.CompilerParams(
            dimension_semantics=("arbitrary",), vmem_limit_bytes=VMEM_LIMIT_BYTES),
        name="ffn",
    )(xp, xs, wg, wu, wd, g, b, *cast)
    return outs[0], outs[1], outs[2:]


def _proj(xb, w_in_ref, off, width):
    return jnp.dot(xb, w_in_ref[:, off:off + width], preferred_element_type=jnp.float32)


def _shift_rows(ext, k, halo):
    return pltpu.roll(ext, k, 0)[halo:]


def _mixer_prompt_kernel(x_ref, w_in_ref, conv_w_ref, pool_w_ref, pool_scale_ref, w_out_ref,
                         g_ref, b_ref, *rest, sub, n_cast):
    cast_src = rest[:n_cast]
    o_ref, conv_out_ref, pool_out_ref, *cast_dst = rest[n_cast:2 * n_cast + 3]
    u_carry, p_carry, m_buf = rest[2 * n_cast + 3:]
    t = pl.program_id(1)
    tm = x_ref.shape[1]

    @pl.when(t == 0)
    def _():
        u_carry[...] = jnp.zeros(u_carry.shape, jnp.float32)
        p_carry[...] = jnp.zeros(p_carry.shape, jnp.float32)

    u_prev = u_carry[...]
    p_prev = p_carry[...]

    def project_out_and_norm(rows):
        out = jnp.dot(m_buf[rows, :], w_out_ref[...], preferred_element_type=jnp.float32)
        o_ref[0, rows, :] = _layer_norm(ALPHA * x_ref[0, rows, :] + out, g_ref[...], b_ref[...])

    pending_rows = None
    for s in range(tm // sub):
        rows = slice(s * sub, (s + 1) * sub)
        x = x_ref[0, rows, :]
        xb = x.astype(jnp.bfloat16)

        p = _proj(xb, w_in_ref, OFF_P, D_POOL)
        p_ext = jnp.concatenate([p_prev, p], axis=0)
        pos = t * tm + s * sub + lax.broadcasted_iota(jnp.int32, (sub, POOL_GROUP), 0)

        u_tails = []
        for j in range(N_POOL_GROUPS):
            csl = slice(j * POOL_OUT_GROUP, (j + 1) * POOL_OUT_GROUP)
            c_g = _proj(xb, w_in_ref, OFF_C + j * POOL_OUT_GROUP, POOL_OUT_GROUP)
            h = _proj(xb, w_in_ref, OFF_H + j * POOL_OUT_GROUP, POOL_OUT_GROUP)
            u = c_g * h
            u_ext = jnp.concatenate([u_prev[:, csl], u], axis=0)
            u_tails.append(u[sub - CONV_HALO:, :])
            cw = conv_w_ref[:, csl]
            conv = (cw[0:1] * _shift_rows(u_ext, 2, CONV_HALO)
                    + cw[1:2] * _shift_rows(u_ext, 1, CONV_HALO)
                    + cw[2:3] * u)
            b_g = _proj(xb, w_in_ref, OFF_B + j * POOL_OUT_GROUP, POOL_OUT_GROUP)
            y_a = b_g * conv

            psl = slice(j * POOL_GROUP, (j + 1) * POOL_GROUP)
            w = POOL_WINDOWS[j]
            acc = p_ext[:, psl]
            span = 1
            while span < w:
                acc = acc + pltpu.roll(acc, span, 0)
                span *= 2
            inv_cnt = 1.0 / jnp.minimum(pos + 1, w).astype(jnp.float32)
            q = acc[POOL_HALO:] * inv_cnt - p[:, psl]
            y_p = jnp.dot(q.astype(jnp.bfloat16), pool_w_ref[j],
                          preferred_element_type=jnp.float32) * pool_scale_ref[:, csl]

            g_a = _proj(xb, w_in_ref, OFF_GA + j * POOL_OUT_GROUP, POOL_OUT_GROUP)
            g_p = _proj(xb, w_in_ref, OFF_GP + j * POOL_OUT_GROUP, POOL_OUT_GROUP)
            m = jax.nn.sigmoid(g_a) * y_a + jax.nn.sigmoid(g_p) * y_p
            m_buf[rows, csl] = m.astype(jnp.bfloat16)

            if j == 0 and pending_rows is not None:
                project_out_and_norm(pending_rows)

        pending_rows = rows
        u_prev = jnp.concatenate(u_tails, axis=1)
        p_prev = p_ext[sub:, :]

    project_out_and_norm(pending_rows)
    _run_casts(cast_src, cast_dst)

    u_carry[...] = u_prev
    p_carry[...] = p_prev
    conv_out_ref[0] = u_carry[CONV_HALO - (CONV_WIDTH - 1):CONV_HALO, :]
    pool_out_ref[0] = p_carry[POOL_HALO - POOL_STATE:POOL_HALO, :]


def _mixer_prompt(x, w_in, conv_w, pool_w, pool_scale, w_out, g, b, tm, cast=()):
    bsz, seq, _ = x.shape
    tiles = seq // tm
    cast_in, cast_out, cast_shapes = _cast_specs(cast, bsz * tiles, lambda s, t: s * tiles + t)
    kern = functools.partial(_mixer_prompt_kernel, sub=min(tm, SUB_ROWS), n_cast=len(cast))
    outs = pl.pallas_call(
        kern,
        out_shape=[
            jax.ShapeDtypeStruct((bsz, seq, D_MODEL), jnp.float32),
            jax.ShapeDtypeStruct((bsz, CONV_WIDTH - 1, D_CONV), jnp.float32),
            jax.ShapeDtypeStruct((bsz, POOL_STATE, D_POOL), jnp.float32),
        ] + cast_shapes,
        grid=(bsz, tiles),
        in_specs=[
            pl.BlockSpec((1, tm, D_MODEL), lambda s, t: (s, t, 0)),
            _const_spec((D_MODEL, D_IN)),
            _const_spec((CONV_WIDTH, D_CONV)),
            _const_spec((N_POOL_GROUPS, POOL_GROUP, POOL_OUT_GROUP)),
            _const_spec((1, D_MODEL)),
            _const_spec((D_MODEL, D_MODEL)),
            _const_spec((1, D_MODEL)),
            _const_spec((1, D_MODEL)),
        ] + cast_in,
        out_specs=[
            pl.BlockSpec((1, tm, D_MODEL), lambda s, t: (s, t, 0)),
            pl.BlockSpec((1, CONV_WIDTH - 1, D_CONV), lambda s, t: (s, 0, 0)),
            pl.BlockSpec((1, POOL_STATE, D_POOL), lambda s, t: (s, 0, 0)),
        ] + cast_out,
        scratch_shapes=[
            pltpu.VMEM((CONV_HALO, D_CONV), jnp.float32),
            pltpu.VMEM((POOL_HALO, D_POOL), jnp.float32),
            pltpu.VMEM((tm, D_MODEL), jnp.bfloat16),
        ],
        compiler_params=pltpu.CompilerParams(
            dimension_semantics=("arbitrary", "arbitrary"), vmem_limit_bytes=VMEM_LIMIT_BYTES),
        name="mixer_prompt",
    )(x, w_in, conv_w, pool_w, pool_scale, w_out, g, b, *cast)
    return outs[0], outs[1], outs[2], outs[3:]


def _mixer_sample_kernel(x_ref, cs_ref, ps_ref, w_in_ref, conv_w_ref, pool_w_ref, pool_scale_ref,
                         w_out_ref, g_ref, b_ref, o_ref, conv_out_ref, pool_out_ref, m_buf):
    x = x_ref[...]
    xb = x.astype(jnp.bfloat16)
    p = _proj(xb, w_in_ref, OFF_P, D_POOL)

    for j in range(N_POOL_GROUPS):
        csl = slice(j * POOL_OUT_GROUP, (j + 1) * POOL_OUT_GROUP)
        c_g = _proj(xb, w_in_ref, OFF_C + j * POOL_OUT_GROUP, POOL_OUT_GROUP)
        h = _proj(xb, w_in_ref, OFF_H + j * POOL_OUT_GROUP, POOL_OUT_GROUP)
        u = c_g * h
        conv_out_ref[:, CONV_WIDTH - 2:CONV_WIDTH - 1, csl] = u[:, None, :]
        cw = conv_w_ref[:, csl]
        conv = cw[0:1] * cs_ref[:, 0, csl] + cw[1:2] * cs_ref[:, 1, csl] + cw[2:3] * u
        b_g = _proj(xb, w_in_ref, OFF_B + j * POOL_OUT_GROUP, POOL_OUT_GROUP)
        y_a = b_g * conv

        w = POOL_WINDOWS[j]
        psl = slice(j * POOL_GROUP, (j + 1) * POOL_GROUP)
        pj = p[:, psl]
        wsum = pj
        for k in range(POOL_STATE - (w - 1), POOL_STATE):
            wsum = wsum + ps_ref[k, :, psl]
        q = wsum * (1.0 / w) - pj
        y_p = jnp.dot(q.astype(jnp.bfloat16), pool_w_ref[j],
                      preferred_element_type=jnp.float32) * pool_scale_ref[:, csl]

        g_a = _proj(xb, w_in_ref, OFF_GA + j * POOL_OUT_GROUP, POOL_OUT_GROUP)
        g_p = _proj(xb, w_in_ref, OFF_GP + j * POOL_OUT_GROUP, POOL_OUT_GROUP)
        m = jax.nn.sigmoid(g_a) * y_a + jax.nn.sigmoid(g_p) * y_p
        m_buf[:, csl] = m.astype(jnp.bfloat16)

    out = jnp.dot(m_buf[...], w_out_ref[...], preferred_element_type=jnp.float32)
    o_ref[...] = _layer_norm(ALPHA * x + out, g_ref[...], b_ref[...])

    conv_out_ref[:, 0:CONV_WIDTH - 2, :] = cs_ref[:, 1:CONV_WIDTH - 1, :]
    pool_out_ref[0:POOL_STATE - 1] = ps_ref[1:POOL_STATE]
    pool_out_ref[POOL_STATE - 1] = p


def _mixer_sample(x2d, cs, ps, w_in, conv_w, pool_w, pool_scale, w_out, g, b):
    nb = x2d.shape[0]
    shapes = [x2d.shape, cs.shape, ps.shape, w_in.shape, conv_w.shape, pool_w.shape,
              pool_scale.shape, w_out.shape, g.shape, b.shape]
    return pl.pallas_call(
        _mixer_sample_kernel,
        out_shape=(
            jax.ShapeDtypeStruct((nb, D_MODEL), jnp.float32),
            jax.ShapeDtypeStruct(cs.shape, jnp.float32),
            jax.ShapeDtypeStruct(ps.shape, jnp.float32),
        ),
        grid=(1,),
        in_specs=[_const_spec(s) for s in shapes],
        out_specs=(
            pl.BlockSpec((nb, D_MODEL), lambda i: (0, 0)),
            pl.BlockSpec(cs.shape, lambda i: (0, 0, 0)),
            pl.BlockSpec(ps.shape, lambda i: (0, 0, 0)),
        ),
        scratch_shapes=[pltpu.VMEM((nb, D_MODEL), jnp.bfloat16)],
        compiler_params=pltpu.CompilerParams(
            dimension_semantics=("arbitrary",), vmem_limit_bytes=VMEM_LIMIT_BYTES),
        name="mixer_sample",
    )(x2d, cs, ps, w_in, conv_w, pool_w, pool_scale, w_out, g, b)


def kernel(x_prompt, x_sample, state_conv, state_pool, ln1_g, ln1_b, ffn1_wg, ffn1_wu, ffn1_wd,
           w_in, conv_w, pool_w, pool_scale, w_out, ln2_g, ln2_b,
           ffn2_wg, ffn2_wu, ffn2_wd, ln3_g, ln3_b):
    bsz, seq, _ = x_prompt.shape
    nb = x_sample.shape[0]
    assert x_sample.shape[1] == 1 and DEPTH == 1
    tm = 1024
    tm_ffn = 1024

    yp = x_prompt.reshape(bsz * seq, D_MODEL)
    l = 0
    wg1, wu1, wd1 = ffn1_wg[l], ffn1_wu[l], ffn1_wd[l]
    g1, b1 = ln1_g[l][None], ln1_b[l][None]
    g2, b2 = ln2_g[l][None], ln2_b[l][None]
    g3, b3 = ln3_g[l][None], ln3_b[l][None]
    scale = pool_scale[l][None]

    pool_w2d = pool_w[l].reshape(N_POOL_GROUPS * POOL_GROUP, POOL_OUT_GROUP)
    yp, ys, (w_in_b, w_out_b, pool_w_b) = _ffn(yp, x_sample.reshape(nb, D_MODEL), wg1, wu1, wd1,
                                               g1, b1, tm_ffn, cast=(w_in[l], w_out[l], pool_w2d))
    pool_w_b = pool_w_b.reshape(N_POOL_GROUPS, POOL_GROUP, POOL_OUT_GROUP)
    yp, conv_p, pool_p, (wg2, wu2, wd2) = _mixer_prompt(
        yp.reshape(bsz, seq, D_MODEL), w_in_b, conv_w[l], pool_w_b, scale, w_out_b, g2, b2, tm,
        cast=(ffn2_wg[l], ffn2_wu[l], ffn2_wd[l]))
    ps_rows = jnp.transpose(state_pool[l], (1, 0, 2))
    ys, conv_s, pool_s = _mixer_sample(ys, state_conv[l], ps_rows, w_in_b, conv_w[l],
                                       pool_w_b, scale, w_out_b, g2, b2)
    yp, ys, _ = _ffn(yp.reshape(bsz * seq, D_MODEL), ys, wg2, wu2, wd2, g3, b3, tm_ffn,
                     sample_out_shape=x_sample.shape)

    return (yp.reshape(bsz, seq, D_MODEL),
            ys,
            conv_p[None],
            pool_p[None],
            conv_s[None],
            jnp.transpose(pool_s, (1, 0, 2))[None])
```

```python
import functools

import jax
import jax.numpy as jnp
from jax import lax
from jax.experimental import pallas as pl
from jax.experimental.pallas import tpu as pltpu

D_MODEL = 1024
D_CONV = D_MODEL
D_POOL = D_MODEL // 2
N_POOL_GROUPS = 4
POOL_GROUP = D_POOL // N_POOL_GROUPS
POOL_OUT_GROUP = D_MODEL // N_POOL_GROUPS
POOL_WINDOWS = (2, 4, 8, 16)
POOL_STATE = max(POOL_WINDOWS) - 1
CONV_WIDTH = 3
D_FF = 2816
DEPTH = 1
ALPHA = (2.0 * DEPTH) ** 0.25
LN_EPS = 1e-5
D_IN = 3 * D_CONV + D_POOL + 2 * D_MODEL

OFF_C = 0
OFF_B = D_CONV
OFF_H = 2 * D_CONV
OFF_P = 3 * D_CONV
OFF_GA = 3 * D_CONV + D_POOL
OFF_GP = 3 * D_CONV + D_POOL + D_MODEL

ROW_BLOCK = 1024
SUB_ROWS = 256
FF_CHUNK = 256
STAGE_CHUNKS = 16
STAGE_SLOTS = 4
POOL_HALO = 16
CONV_HALO = 8
VMEM_LIMIT_BYTES = 52 * 1024 * 1024


def _layer_norm(v, g, b):
    mu = jnp.mean(v, axis=-1, keepdims=True)
    d = v - mu
    var = jnp.mean(d * d, axis=-1, keepdims=True)
    return d * lax.rsqrt(var + LN_EPS) * g + b


def _const_spec(shape):
    zeros = (0,) * len(shape)
    return pl.BlockSpec(shape, lambda *_: zeros, pipeline_mode=pl.Buffered(1))


def _cast_specs(arrays, n_steps, step_index):
    in_specs, out_specs, out_shapes = [], [], []
    for a in arrays:
        rows, cols = a.shape
        assert rows % (n_steps * 16) == 0
        spec = pl.BlockSpec((rows // n_steps, cols), lambda *idx: (step_index(*idx), 0))
        in_specs.append(spec)
        out_specs.append(spec)
        out_shapes.append(jax.ShapeDtypeStruct(a.shape, jnp.bfloat16))
    return in_specs, out_specs, out_shapes


def _run_casts(src_refs, dst_refs):
    for src, dst in zip(src_refs, dst_refs):
        dst[...] = src[...].astype(jnp.bfloat16)


def _ffn_rows(x, wg_ref, wu_ref, wd_ref, g_ref, b_ref, h_view):
    xb = x.astype(jnp.bfloat16)
    for c in range(D_FF // FF_CHUNK):
        sl = slice(c * FF_CHUNK, (c + 1) * FF_CHUNK)
        gate = jnp.dot(xb, wg_ref[:, sl], preferred_element_type=jnp.float32)
        up = jnp.dot(xb, wu_ref[:, sl], preferred_element_type=jnp.float32)
        h_view[:, sl] = (gate * jax.nn.sigmoid(gate) * up).astype(jnp.bfloat16)
    y = jnp.dot(h_view[...], wd_ref[...], preferred_element_type=jnp.float32)
    return _layer_norm(ALPHA * x + 0.5 * y, g_ref[...], b_ref[...])


def _stage_cast(w_hbm, w_vmem, stage, sem):
    slots, chunk = stage.shape[0], stage.shape[1]
    n = w_hbm.shape[0] // chunk
    ahead = slots - 1

    def rows(k):
        return pl.ds(pl.multiple_of(k * chunk, chunk), chunk)

    def copy(k):
        return pltpu.make_async_copy(w_hbm.at[rows(k), :], stage.at[k % slots], sem.at[k % slots])

    for k in range(min(ahead, n)):
        copy(k).start()

    def body(k, carry):
        @pl.when(k + ahead < n)
        def _():
            copy(k + ahead).start()

        copy(k).wait()
        w_vmem[rows(k), :] = stage[k % slots].astype(jnp.bfloat16)
        return carry

    lax.fori_loop(0, n, body, 0)


def _ffn_kernel(xp_ref, xs_ref, wg_ref, wu_ref, wd_ref, g_ref, b_ref, *rest, sub, n_cast,
                stage_weights):
    cast_src = rest[:n_cast]
    op_ref, os_ref, *cast_dst = rest[n_cast:2 * n_cast + 2]
    h_ref, *stage_scratch = rest[2 * n_cast + 2:]
    i = pl.program_id(0)
    last = pl.num_programs(0) - 1

    if stage_weights:
        wg_v, wu_v, wd_v, stage_up, stage_down, sem = stage_scratch

        @pl.when(i == 0)
        def _():
            _stage_cast(wg_ref, wg_v, stage_up, sem)
            _stage_cast(wu_ref, wu_v, stage_up, sem)
            _stage_cast(wd_ref, wd_v, stage_down, sem)

        wg_ref, wu_ref, wd_ref = wg_v, wu_v, wd_v
    weights = (wg_ref, wu_ref, wd_ref, g_ref, b_ref)

    @pl.when(i < last)
    def _():
        for s in range(xp_ref.shape[0] // sub):
            rows = pl.ds(s * sub, sub)
            op_ref[rows, :] = _ffn_rows(xp_ref[rows, :], *weights, h_ref.at[rows, :])
        _run_casts(cast_src, cast_dst)

    @pl.when(i == last)
    def _():
        ns = xs_ref.shape[0]
        xs = xs_ref[...] if len(xs_ref.shape) == 2 else xs_ref[:, 0, :]
        ys = _ffn_rows(xs, *weights, h_ref.at[pl.ds(0, ns), :])
        os_ref[...] = ys if len(os_ref.shape) == 2 else ys[:, None, :]


def _ffn(xp, xs, wg, wu, wd, g, b, tm, cast=(), sample_out_shape=None):
    n, ns = xp.shape[0], xs.shape[0]
    nt = n // tm
    assert ns <= tm
    sample_out_shape = sample_out_shape or (ns, D_MODEL)
    sample_zeros = (0,) * len(sample_out_shape)
    blk = lambda i: (jnp.minimum(i, nt - 1), 0)
    cast_in, cast_out, cast_shapes = _cast_specs(cast, nt, lambda i: jnp.minimum(i, nt - 1))
    stage_weights = wg.dtype == jnp.float32
    assert wu.dtype == wg.dtype and wd.dtype == wg.dtype
    scratch = [pltpu.VMEM((tm, D_FF), jnp.bfloat16)]
    if stage_weights:
        weight_specs = [pl.BlockSpec(memory_space=pl.ANY)] * 3
        scratch += [
            pltpu.VMEM((D_MODEL, D_FF), jnp.bfloat16),
            pltpu.VMEM((D_MODEL, D_FF), jnp.bfloat16),
            pltpu.VMEM((D_FF, D_MODEL), jnp.bfloat16),
            pltpu.VMEM((STAGE_SLOTS, D_MODEL // STAGE_CHUNKS, D_FF), jnp.float32),
            pltpu.VMEM((STAGE_SLOTS, D_FF // STAGE_CHUNKS, D_MODEL), jnp.float32),
            pltpu.SemaphoreType.DMA((STAGE_SLOTS,)),
        ]
    else:
        weight_specs = [_const_spec((D_MODEL, D_FF)), _const_spec((D_MODEL, D_FF)),
                        _const_spec((D_FF, D_MODEL))]
    outs = pl.pallas_call(
        functools.partial(_ffn_kernel, sub=min(tm, SUB_ROWS), n_cast=len(cast),
                          stage_weights=stage_weights),
        out_shape=[jax.ShapeDtypeStruct((n, D_MODEL), jnp.float32),
                   jax.ShapeDtypeStruct(sample_out_shape, jnp.float32)] + cast_shapes,
        grid=(nt + 1,),
        in_specs=[pl.BlockSpec((tm, D_MODEL), blk), _const_spec(xs.shape)] + weight_specs
        + [_const_spec((1, D_MODEL)), _const_spec((1, D_MODEL))] + cast_in,
        out_specs=[pl.BlockSpec((tm, D_MODEL), blk),
                   pl.BlockSpec(sample_out_shape, lambda i: sample_zeros)] + cast_out,
        scratch_shapes=scratch,
        compiler_params=pltpu.CompilerParams(
            dimension_semantics=("arbitrary",), vmem_limit_bytes=VMEM_LIMIT_BYTES),
        name="ffn",
    )(xp, xs, wg, wu, wd, g, b, *cast)
    return outs[0], outs[1], outs[2:]


def _proj(xb, w_in_ref, off, width):
    return jnp.dot(xb, w_in_ref[:, off:off + width], preferred_element_type=jnp.float32)


def _shift_rows(ext, k, halo):
    return pltpu.roll(ext, k, 0)[halo:]


def _mixer_prompt_kernel(x_ref, w_in_ref, conv_w_ref, pool_w_ref, pool_scale_ref, w_out_ref,
                         g_ref, b_ref, *rest, sub, n_cast):
    cast_src = rest[:n_cast]
    o_ref, conv_out_ref, pool_out_ref, *cast_dst = rest[n_cast:2 * n_cast + 3]
    u_carry, p_carry, m_buf = rest[2 * n_cast + 3:]
    t = pl.program_id(1)
    tm = x_ref.shape[1]

    @pl.when(t == 0)
    def _():
        u_carry[...] = jnp.zeros(u_carry.shape, jnp.float32)
        p_carry[...] = jnp.zeros(p_carry.shape, jnp.float32)

    u_prev = u_carry[...]
    p_prev = p_carry[...]

    def project_out_and_norm(rows):
        out = jnp.dot(m_buf[rows, :], w_out_ref[...], preferred_element_type=jnp.float32)
        o_ref[0, rows, :] = _layer_norm(ALPHA * x_ref[0, rows, :] + out, g_ref[...], b_ref[...])

    pending_rows = None
    for s in range(tm // sub):
        rows = slice(s * sub, (s + 1) * sub)
        x = x_ref[0, rows, :]
        xb = x.astype(jnp.bfloat16)

        p = _proj(xb, w_in_ref, OFF_P, D_POOL)
        p_ext = jnp.concatenate([p_prev, p], axis=0)
        pos = t * tm + s * sub + lax.broadcasted_iota(jnp.int32, (sub, POOL_GROUP), 0)

        u_tails = []
        for j in range(N_POOL_GROUPS):
            csl = slice(j * POOL_OUT_GROUP, (j + 1) * POOL_OUT_GROUP)
            c_g = _proj(xb, w_in_ref, OFF_C + j * POOL_OUT_GROUP, POOL_OUT_GROUP)
            h = _proj(xb, w_in_ref, OFF_H + j * POOL_OUT_GROUP, POOL_OUT_GROUP)
            u = c_g * h
            u_ext = jnp.concatenate([u_prev[:, csl], u], axis=0)
            u_tails.append(u[sub - CONV_HALO:, :])
            cw = conv_w_ref[:, csl]
            conv = (cw[0:1] * _shift_rows(u_ext, 2, CONV_HALO)
                    + cw[1:2] * _shift_rows(u_ext, 1, CONV_HALO)
                    + cw[2:3] * u)
            b_g = _proj(xb, w_in_ref, OFF_B + j * POOL_OUT_GROUP, POOL_OUT_GROUP)
            y_a = b_g * conv

            psl = slice(j * POOL_GROUP, (j + 1) * POOL_GROUP)
            w = POOL_WINDOWS[j]
            acc = p_ext[:, psl]
            span = 1
            while span < w:
                acc = acc + pltpu.roll(acc, span, 0)
                span *= 2
            inv_cnt = 1.0 / jnp.minimum(pos + 1, w).astype(jnp.float32)
            q = acc[POOL_HALO:] * inv_cnt - p[:, psl]
            y_p = jnp.dot(q.astype(jnp.bfloat16), pool_w_ref[j],
                          preferred_element_type=jnp.float32) * pool_scale_ref[:, csl]

            g_a = _proj(xb, w_in_ref, OFF_GA + j * POOL_OUT_GROUP, POOL_OUT_GROUP)
            g_p = _proj(xb, w_in_ref, OFF_GP + j * POOL_OUT_GROUP, POOL_OUT_GROUP)
            m = jax.nn.sigmoid(g_a) * y_a + jax.nn.sigmoid(g_p) * y_p
            m_buf[rows, csl] = m.astype(jnp.bfloat16)

            if j == 0 and pending_rows is not None:
                project_out_and_norm(pending_rows)

        pending_rows = rows
        u_prev = jnp.concatenate(u_tails, axis=1)
        p_prev = p_ext[sub:, :]

    project_out_and_norm(pending_rows)
    _run_casts(cast_src, cast_dst)

    u_carry[...] = u_prev
    p_carry[...] = p_prev
    conv_out_ref[0] = u_carry[CONV_HALO - (CONV_WIDTH - 1):CONV_HALO, :]
    pool_out_ref[0] = p_carry[POOL_HALO - POOL_STATE:POOL_HALO, :]


def _mixer_prompt(x, w_in, conv_w, pool_w, pool_scale, w_out, g, b, tm, cast=()):
    bsz, seq, _ = x.shape
    tiles = seq // tm
    cast_in, cast_out, cast_shapes = _cast_specs(cast, bsz * tiles, lambda s, t: s * tiles + t)
    kern = functools.partial(_mixer_prompt_kernel, sub=min(tm, SUB_ROWS), n_cast=len(cast))
    outs = pl.pallas_call(
        kern,
        out_shape=[
            jax.ShapeDtypeStruct((bsz, seq, D_MODEL), jnp.float32),
            jax.ShapeDtypeStruct((bsz, CONV_WIDTH - 1, D_CONV), jnp.float32),
            jax.ShapeDtypeStruct((bsz, POOL_STATE, D_POOL), jnp.float32),
        ] + cast_shapes,
        grid=(bsz, tiles),
        in_specs=[
            pl.BlockSpec((1, tm, D_MODEL), lambda s, t: (s, t, 0)),
            _const_spec((D_MODEL, D_IN)),
            _const_spec((CONV_WIDTH, D_CONV)),
            _const_spec((N_POOL_GROUPS, POOL_GROUP, POOL_OUT_GROUP)),
            _const_spec((1, D_MODEL)),
            _const_spec((D_MODEL, D_MODEL)),
            _const_spec((1, D_MODEL)),
            _const_spec((1, D_MODEL)),
        ] + cast_in,
        out_specs=[
            pl.BlockSpec((1, tm, D_MODEL), lambda s, t: (s, t, 0)),
            pl.BlockSpec((1, CONV_WIDTH - 1, D_CONV), lambda s, t: (s, 0, 0)),
            pl.BlockSpec((1, POOL_STATE, D_POOL), lambda s, t: (s, 0, 0)),
        ] + cast_out,
        scratch_shapes=[
            pltpu.VMEM((CONV_HALO, D_CONV), jnp.float32),
            pltpu.VMEM((POOL_HALO, D_POOL), jnp.float32),
            pltpu.VMEM((tm, D_MODEL), jnp.bfloat16),
        ],
        compiler_params=pltpu.CompilerParams(
            dimension_semantics=("arbitrary", "arbitrary"), vmem_limit_bytes=VMEM_LIMIT_BYTES),
        name="mixer_prompt",
    )(x, w_in, conv_w, pool_w, pool_scale, w_out, g, b, *cast)
    return outs[0], outs[1], outs[2], outs[3:]


def _mixer_sample_kernel(x_ref, cs_ref, ps_ref, w_in_ref, conv_w_ref, pool_w_ref, pool_scale_ref,
                         w_out_ref, g_ref, b_ref, o_ref, conv_out_ref, pool_out_ref, m_buf):
    x = x_ref[...]
    xb = x.astype(jnp.bfloat16)
    p = _proj(xb, w_in_ref, OFF_P, D_POOL)

    for j in range(N_POOL_GROUPS):
        csl = slice(j * POOL_OUT_GROUP, (j + 1) * POOL_OUT_GROUP)
        c_g = _proj(xb, w_in_ref, OFF_C + j * POOL_OUT_GROUP, POOL_OUT_GROUP)
        h = _proj(xb, w_in_ref, OFF_H + j * POOL_OUT_GROUP, POOL_OUT_GROUP)
        u = c_g * h
        conv_out_ref[:, CONV_WIDTH - 2:CONV_WIDTH - 1, csl] = u[:, None, :]
        cw = conv_w_ref[:, csl]
        conv = cw[0:1] * cs_ref[:, 0, csl] + cw[1:2] * cs_ref[:, 1, csl] + cw[2:3] * u
        b_g = _proj(xb, w_in_ref, OFF_B + j * POOL_OUT_GROUP, POOL_OUT_GROUP)
        y_a = b_g * conv

        w = POOL_WINDOWS[j]
        psl = slice(j * POOL_GROUP, (j + 1) * POOL_GROUP)
        pj = p[:, psl]
        wsum = pj
        for k in range(POOL_STATE - (w - 1), POOL_STATE):
            wsum = wsum + ps_ref[k, :, psl]
        q = wsum * (1.0 / w) - pj
        y_p = jnp.dot(q.astype(jnp.bfloat16), pool_w_ref[j],
                      preferred_element_type=jnp.float32) * pool_scale_ref[:, csl]

        g_a = _proj(xb, w_in_ref, OFF_GA + j * POOL_OUT_GROUP, POOL_OUT_GROUP)
        g_p = _proj(xb, w_in_ref, OFF_GP + j * POOL_OUT_GROUP, POOL_OUT_GROUP)
        m = jax.nn.sigmoid(g_a) * y_a + jax.nn.sigmoid(g_p) * y_p
        m_buf[:, csl] = m.astype(jnp.bfloat16)

    out = jnp.dot(m_buf[...], w_out_ref[...], preferred_element_type=jnp.float32)
    o_ref[...] = _layer_norm(ALPHA * x + out, g_ref[...], b_ref[...])

    conv_out_ref[:, 0:CONV_WIDTH - 2, :] = cs_ref[:, 1:CONV_WIDTH - 1, :]
    pool_out_ref[0:POOL_STATE - 1] = ps_ref[1:POOL_STATE]
    pool_out_ref[POOL_STATE - 1] = p


def _mixer_sample(x2d, cs, ps, w_in, conv_w, pool_w, pool_scale, w_out, g, b):
    nb = x2d.shape[0]
    shapes = [x2d.shape, cs.shape, ps.shape, w_in.shape, conv_w.shape, pool_w.shape,
              pool_scale.shape, w_out.shape, g.shape, b.shape]
    return pl.pallas_call(
        _mixer_sample_kernel,
        out_shape=(
            jax.ShapeDtypeStruct((nb, D_MODEL), jnp.float32),
            jax.ShapeDtypeStruct(cs.shape, jnp.float32),
            jax.ShapeDtypeStruct(ps.shape, jnp.float32),
        ),
        grid=(1,),
        in_specs=[_const_spec(s) for s in shapes],
        out_specs=(
            pl.BlockSpec((nb, D_MODEL), lambda i: (0, 0)),
            pl.BlockSpec(cs.shape, lambda i: (0, 0, 0)),
            pl.BlockSpec(ps.shape, lambda i: (0, 0, 0)),
        ),
        scratch_shapes=[pltpu.VMEM((nb, D_MODEL), jnp.bfloat16)],
        compiler_params=pltpu.CompilerParams(
            dimension_semantics=("arbitrary",), vmem_limit_bytes=VMEM_LIMIT_BYTES),
        name="mixer_sample",
    )(x2d, cs, ps, w_in, conv_w, pool_w, pool_scale, w_out, g, b)


def kernel(x_prompt, x_sample, state_conv, state_pool, ln1_g, ln1_b, ffn1_wg, ffn1_wu, ffn1_wd,
           w_in, conv_w, pool_w, pool_scale, w_out, ln2_g, ln2_b,
           ffn2_wg, ffn2_wu, ffn2_wd, ln3_g, ln3_b):
    bsz, seq, _ = x_prompt.shape
    nb = x_sample.shape[0]
    assert x_sample.shape[1] == 1 and DEPTH == 1
    assert seq % ROW_BLOCK == 0

    yp = x_prompt.reshape(bsz * seq, D_MODEL)
    l = 0
    wg1, wu1, wd1 = ffn1_wg[l], ffn1_wu[l], ffn1_wd[l]
    g1, b1 = ln1_g[l][None], ln1_b[l][None]
    g2, b2 = ln2_g[l][None], ln2_b[l][None]
    g3, b3 = ln3_g[l][None], ln3_b[l][None]
    scale = pool_scale[l][None]

    pool_w2d = pool_w[l].reshape(N_POOL_GROUPS * POOL_GROUP, POOL_OUT_GROUP)
    yp, ys, (w_in_b, w_out_b, pool_w_b) = _ffn(yp, x_sample.reshape(nb, D_MODEL), wg1, wu1, wd1,
                                               g1, b1, ROW_BLOCK, cast=(w_in[l], w_out[l], pool_w2d))
    pool_w_b = pool_w_b.reshape(N_POOL_GROUPS, POOL_GROUP, POOL_OUT_GROUP)
    yp, conv_p, pool_p, (wg2, wu2, wd2) = _mixer_prompt(
        yp.reshape(bsz, seq, D_MODEL), w_in_b, conv_w[l], pool_w_b, scale, w_out_b, g2, b2,
        ROW_BLOCK, cast=(ffn2_wg[l], ffn2_wu[l], ffn2_wd[l]))
    ps_rows = jnp.transpose(state_pool[l], (1, 0, 2))
    ys, conv_s, pool_s = _mixer_sample(ys, state_conv[l], ps_rows, w_in_b, conv_w[l],
                                       pool_w_b, scale, w_out_b, g2, b2)
    yp, ys, _ = _ffn(yp.reshape(bsz * seq, D_MODEL), ys, wg2, wu2, wd2, g3, b3, ROW_BLOCK,
                     sample_out_shape=x_sample.shape)

    return (yp.reshape(bsz, seq, D_MODEL),
            ys,
            conv_p[None],
            pool_p[None],
            conv_s[None],
            jnp.transpose(pool_s, (1, 0, 2))[None])
```

```python
import functools

import jax
import jax.numpy as jnp
from jax import lax
from jax.experimental import pallas as pl
from jax.experimental.pallas import tpu as pltpu

D_MODEL = 1024
D_CONV = D_MODEL
D_POOL = D_MODEL // 2
N_POOL_GROUPS = 4
POOL_GROUP = D_POOL // N_POOL_GROUPS
POOL_OUT_GROUP = D_MODEL // N_POOL_GROUPS
POOL_WINDOWS = (2, 4, 8, 16)
POOL_STATE = max(POOL_WINDOWS) - 1
CONV_WIDTH = 3
D_FF = 2816
DEPTH = 1
ALPHA = (2.0 * DEPTH) ** 0.25
LN_EPS = 1e-5
D_IN = 3 * D_CONV + D_POOL + 2 * D_MODEL

OFF_C = 0
OFF_B = D_CONV
OFF_H = 2 * D_CONV
OFF_P = 3 * D_CONV
OFF_GA = 3 * D_CONV + D_POOL
OFF_GP = 3 * D_CONV + D_POOL + D_MODEL

ROW_BLOCK = 1024
SUB_ROWS = 256
FF_CHUNK = 256
STAGE_CHUNKS = 16
STAGE_SLOTS = 6
POOL_HALO = 16
CONV_HALO = 8
VMEM_LIMIT_BYTES = 52 * 1024 * 1024


def _layer_norm(v, g, b):
    mu = jnp.mean(v, axis=-1, keepdims=True)
    d = v - mu
    var = jnp.mean(d * d, axis=-1, keepdims=True)
    return d * lax.rsqrt(var + LN_EPS) * g + b


def _const_spec(shape):
    zeros = (0,) * len(shape)
    return pl.BlockSpec(shape, lambda *_: zeros, pipeline_mode=pl.Buffered(1))


def _cast_specs(arrays, n_steps, step_index):
    in_specs, out_specs, out_shapes = [], [], []
    for a in arrays:
        rows, cols = a.shape
        assert rows % (n_steps * 16) == 0
        spec = pl.BlockSpec((rows // n_steps, cols), lambda *idx: (step_index(*idx), 0))
        in_specs.append(spec)
        out_specs.append(spec)
        out_shapes.append(jax.ShapeDtypeStruct(a.shape, jnp.bfloat16))
    return in_specs, out_specs, out_shapes


def _run_casts(src_refs, dst_refs):
    for src, dst in zip(src_refs, dst_refs):
        dst[...] = src[...].astype(jnp.bfloat16)


def _ffn_rows(x, wg_ref, wu_ref, wd_ref, g_ref, b_ref, h_view):
    xb = x.astype(jnp.bfloat16)
    for c in range(D_FF // FF_CHUNK):
        sl = slice(c * FF_CHUNK, (c + 1) * FF_CHUNK)
        gate = jnp.dot(xb, wg_ref[:, sl], preferred_element_type=jnp.float32)
        up = jnp.dot(xb, wu_ref[:, sl], preferred_element_type=jnp.float32)
        h_view[:, sl] = (gate * jax.nn.sigmoid(gate) * up).astype(jnp.bfloat16)
    y = jnp.dot(h_view[...], wd_ref[...], preferred_element_type=jnp.float32)
    return _layer_norm(ALPHA * x + 0.5 * y, g_ref[...], b_ref[...])


def _stage_cast(w_hbm, w_vmem, stage, sem):
    slots, chunk = stage.shape[0], stage.shape[1]
    n = w_hbm.shape[0] // chunk
    ahead = slots - 1

    def rows(k):
        return pl.ds(pl.multiple_of(k * chunk, chunk), chunk)

    def copy(k):
        return pltpu.make_async_copy(w_hbm.at[rows(k), :], stage.at[k % slots], sem.at[k % slots])

    for k in range(min(ahead, n)):
        copy(k).start()

    def body(k, carry):
        @pl.when(k + ahead < n)
        def _():
            copy(k + ahead).start()

        copy(k).wait()
        w_vmem[rows(k), :] = stage[k % slots].astype(jnp.bfloat16)
        return carry

    lax.fori_loop(0, n, body, 0)


def _ffn_kernel(xp_ref, xs_ref, wg_ref, wu_ref, wd_ref, g_ref, b_ref, *rest, sub, n_cast,
                stage_weights):
    cast_src = rest[:n_cast]
    op_ref, os_ref, *cast_dst = rest[n_cast:2 * n_cast + 2]
    h_ref, *stage_scratch = rest[2 * n_cast + 2:]
    i = pl.program_id(0)
    last = pl.num_programs(0) - 1

    if stage_weights:
        wg_v, wu_v, wd_v, stage_up, stage_down, sem = stage_scratch

        @pl.when(i == 0)
        def _():
            _stage_cast(wg_ref, wg_v, stage_up, sem)
            _stage_cast(wu_ref, wu_v, stage_up, sem)
            _stage_cast(wd_ref, wd_v, stage_down, sem)

        wg_ref, wu_ref, wd_ref = wg_v, wu_v, wd_v
    weights = (wg_ref, wu_ref, wd_ref, g_ref, b_ref)

    @pl.when(i < last)
    def _():
        for s in range(xp_ref.shape[0] // sub):
            rows = pl.ds(s * sub, sub)
            h_view = h_ref.at[pl.ds((s % 2) * sub, sub), :]
            op_ref[rows, :] = _ffn_rows(xp_ref[rows, :], *weights, h_view)
        _run_casts(cast_src, cast_dst)

    @pl.when(i == last)
    def _():
        ns = xs_ref.shape[0]
        xs = xs_ref[...] if len(xs_ref.shape) == 2 else xs_ref[:, 0, :]
        ys = _ffn_rows(xs, *weights, h_ref.at[pl.ds(0, ns), :])
        os_ref[...] = ys if len(os_ref.shape) == 2 else ys[:, None, :]


def _ffn(xp, xs, wg, wu, wd, g, b, tm, cast=(), sample_out_shape=None):
    n, ns = xp.shape[0], xs.shape[0]
    nt = n // tm
    assert ns <= tm
    sample_out_shape = sample_out_shape or (ns, D_MODEL)
    sample_zeros = (0,) * len(sample_out_shape)
    blk = lambda i: (jnp.minimum(i, nt - 1), 0)
    cast_in, cast_out, cast_shapes = _cast_specs(cast, nt, lambda i: jnp.minimum(i, nt - 1))
    stage_weights = wg.dtype == jnp.float32
    assert wu.dtype == wg.dtype and wd.dtype == wg.dtype
    sub = min(tm, SUB_ROWS)
    scratch = [pltpu.VMEM((min(tm, 2 * sub), D_FF), jnp.bfloat16)]
    if stage_weights:
        weight_specs = [pl.BlockSpec(memory_space=pl.ANY)] * 3
        scratch += [
            pltpu.VMEM((D_MODEL, D_FF), jnp.bfloat16),
            pltpu.VMEM((D_MODEL, D_FF), jnp.bfloat16),
            pltpu.VMEM((D_FF, D_MODEL), jnp.bfloat16),
            pltpu.VMEM((STAGE_SLOTS, D_MODEL // STAGE_CHUNKS, D_FF), jnp.float32),
            pltpu.VMEM((STAGE_SLOTS, D_FF // STAGE_CHUNKS, D_MODEL), jnp.float32),
            pltpu.SemaphoreType.DMA((STAGE_SLOTS,)),
        ]
    else:
        weight_specs = [_const_spec((D_MODEL, D_FF)), _const_spec((D_MODEL, D_FF)),
                        _const_spec((D_FF, D_MODEL))]
    outs = pl.pallas_call(
        functools.partial(_ffn_kernel, sub=sub, n_cast=len(cast),
                          stage_weights=stage_weights),
        out_shape=[jax.ShapeDtypeStruct((n, D_MODEL), jnp.float32),
                   jax.ShapeDtypeStruct(sample_out_shape, jnp.float32)] + cast_shapes,
        grid=(nt + 1,),
        in_specs=[pl.BlockSpec((tm, D_MODEL), blk), _const_spec(xs.shape)] + weight_specs
        + [_const_spec((1, D_MODEL)), _const_spec((1, D_MODEL))] + cast_in,
        out_specs=[pl.BlockSpec((tm, D_MODEL), blk),
                   pl.BlockSpec(sample_out_shape, lambda i: sample_zeros)] + cast_out,
        scratch_shapes=scratch,
        compiler_params=pltpu.CompilerParams(
            dimension_semantics=("arbitrary",), vmem_limit_bytes=VMEM_LIMIT_BYTES),
        name="ffn",
    )(xp, xs, wg, wu, wd, g, b, *cast)
    return outs[0], outs[1], outs[2:]


def _proj(xb, w_in_ref, off, width):
    return jnp.dot(xb, w_in_ref[:, off:off + width], preferred_element_type=jnp.float32)


def _shift_rows(ext, k, halo):
    return pltpu.roll(ext, k, 0)[halo:]


def _mixer_prompt_kernel(x_ref, w_in_ref, conv_w_ref, pool_w_ref, pool_scale_ref, w_out_ref,
                         g_ref, b_ref, *rest, sub, n_cast):
    cast_src = rest[:n_cast]
    o_ref, conv_out_ref, pool_out_ref, *cast_dst = rest[n_cast:2 * n_cast + 3]
    u_carry, p_carry, m_buf = rest[2 * n_cast + 3:]
    t = pl.program_id(1)
    tm = x_ref.shape[1]

    @pl.when(t == 0)
    def _():
        u_carry[...] = jnp.zeros(u_carry.shape, jnp.float32)
        p_carry[...] = jnp.zeros(p_carry.shape, jnp.float32)

    u_prev = u_carry[...]
    p_prev = p_carry[...]

    def project_out_and_norm(rows):
        out = jnp.dot(m_buf[rows, :], w_out_ref[...], preferred_element_type=jnp.float32)
        o_ref[0, rows, :] = _layer_norm(ALPHA * x_ref[0, rows, :] + out, g_ref[...], b_ref[...])

    pending_rows = None
    for s in range(tm // sub):
        rows = slice(s * sub, (s + 1) * sub)
        x = x_ref[0, rows, :]
        xb = x.astype(jnp.bfloat16)

        p = _proj(xb, w_in_ref, OFF_P, D_POOL)
        p_ext = jnp.concatenate([p_prev, p], axis=0)
        pos = t * tm + s * sub + lax.broadcasted_iota(jnp.int32, (sub, POOL_GROUP), 0)

        u_tails = []
        for j in range(N_POOL_GROUPS):
            csl = slice(j * POOL_OUT_GROUP, (j + 1) * POOL_OUT_GROUP)
            c_g = _proj(xb, w_in_ref, OFF_C + j * POOL_OUT_GROUP, POOL_OUT_GROUP)
            h = _proj(xb, w_in_ref, OFF_H + j * POOL_OUT_GROUP, POOL_OUT_GROUP)
            u = c_g * h
            u_ext = jnp.concatenate([u_prev[:, csl], u], axis=0)
            u_tails.append(u[sub - CONV_HALO:, :])
            cw = conv_w_ref[:, csl]
            conv = (cw[0:1] * _shift_rows(u_ext, 2, CONV_HALO)
                    + cw[1:2] * _shift_rows(u_ext, 1, CONV_HALO)
                    + cw[2:3] * u)
            b_g = _proj(xb, w_in_ref, OFF_B + j * POOL_OUT_GROUP, POOL_OUT_GROUP)
            y_a = b_g * conv

            psl = slice(j * POOL_GROUP, (j + 1) * POOL_GROUP)
            w = POOL_WINDOWS[j]
            acc = p_ext[:, psl]
            span = 1
            while span < w:
                acc = acc + pltpu.roll(acc, span, 0)
                span *= 2
            inv_cnt = 1.0 / jnp.minimum(pos + 1, w).astype(jnp.float32)
            q = acc[POOL_HALO:] * inv_cnt - p[:, psl]
            y_p = jnp.dot(q.astype(jnp.bfloat16), pool_w_ref[j],
                          preferred_element_type=jnp.float32) * pool_scale_ref[:, csl]

            g_a = _proj(xb, w_in_ref, OFF_GA + j * POOL_OUT_GROUP, POOL_OUT_GROUP)
            g_p = _proj(xb, w_in_ref, OFF_GP + j * POOL_OUT_GROUP, POOL_OUT_GROUP)
            m = jax.nn.sigmoid(g_a) * y_a + jax.nn.sigmoid(g_p) * y_p
            m_buf[rows, csl] = m.astype(jnp.bfloat16)

            if j == 0 and pending_rows is not None:
                project_out_and_norm(pending_rows)

        pending_rows = rows
        u_prev = jnp.concatenate(u_tails, axis=1)
        p_prev = p_ext[sub:, :]

    project_out_and_norm(pending_rows)
    _run_casts(cast_src, cast_dst)

    u_carry[...] = u_prev
    p_carry[...] = p_prev
    conv_out_ref[0] = u_carry[CONV_HALO - (CONV_WIDTH - 1):CONV_HALO, :]
    pool_out_ref[0] = p_carry[POOL_HALO - POOL_STATE:POOL_HALO, :]


def _mixer_prompt(x, w_in, conv_w, pool_w, pool_scale, w_out, g, b, tm, cast=()):
    bsz, seq, _ = x.shape
    tiles = seq // tm
    cast_in, cast_out, cast_shapes = _cast_specs(cast, bsz * tiles, lambda s, t: s * tiles + t)
    kern = functools.partial(_mixer_prompt_kernel, sub=min(tm, SUB_ROWS), n_cast=len(cast))
    outs = pl.pallas_call(
        kern,
        out_shape=[
            jax.ShapeDtypeStruct((bsz, seq, D_MODEL), jnp.float32),
            jax.ShapeDtypeStruct((bsz, CONV_WIDTH - 1, D_CONV), jnp.float32),
            jax.ShapeDtypeStruct((bsz, POOL_STATE, D_POOL), jnp.float32),
        ] + cast_shapes,
        grid=(bsz, tiles),
        in_specs=[
            pl.BlockSpec((1, tm, D_MODEL), lambda s, t: (s, t, 0)),
            _const_spec((D_MODEL, D_IN)),
            _const_spec((CONV_WIDTH, D_CONV)),
            _const_spec((N_POOL_GROUPS, POOL_GROUP, POOL_OUT_GROUP)),
            _const_spec((1, D_MODEL)),
            _const_spec((D_MODEL, D_MODEL)),
            _const_spec((1, D_MODEL)),
            _const_spec((1, D_MODEL)),
        ] + cast_in,
        out_specs=[
            pl.BlockSpec((1, tm, D_MODEL), lambda s, t: (s, t, 0)),
            pl.BlockSpec((1, CONV_WIDTH - 1, D_CONV), lambda s, t: (s, 0, 0)),
            pl.BlockSpec((1, POOL_STATE, D_POOL), lambda s, t: (s, 0, 0)),
        ] + cast_out,
        scratch_shapes=[
            pltpu.VMEM((CONV_HALO, D_CONV), jnp.float32),
            pltpu.VMEM((POOL_HALO, D_POOL), jnp.float32),
            pltpu.VMEM((tm, D_MODEL), jnp.bfloat16),
        ],
        compiler_params=pltpu.CompilerParams(
            dimension_semantics=("arbitrary", "arbitrary"), vmem_limit_bytes=VMEM_LIMIT_BYTES),
        name="mixer_prompt",
    )(x, w_in, conv_w, pool_w, pool_scale, w_out, g, b, *cast)
    return outs[0], outs[1], outs[2], outs[3:]


def _mixer_sample_kernel(x_ref, cs_ref, ps_ref, w_in_ref, conv_w_ref, pool_w_ref, pool_scale_ref,
                         w_out_ref, g_ref, b_ref, o_ref, conv_out_ref, pool_out_ref, m_buf):
    x = x_ref[...]
    xb = x.astype(jnp.bfloat16)
    p = _proj(xb, w_in_ref, OFF_P, D_POOL)

    for j in range(N_POOL_GROUPS):
        csl = slice(j * POOL_OUT_GROUP, (j + 1) * POOL_OUT_GROUP)
        c_g = _proj(xb, w_in_ref, OFF_C + j * POOL_OUT_GROUP, POOL_OUT_GROUP)
        h = _proj(xb, w_in_ref, OFF_H + j * POOL_OUT_GROUP, POOL_OUT_GROUP)
        u = c_g * h
        conv_out_ref[:, CONV_WIDTH - 2:CONV_WIDTH - 1, csl] = u[:, None, :]
        cw = conv_w_ref[:, csl]
        conv = cw[0:1] * cs_ref[:, 0, csl] + cw[1:2] * cs_ref[:, 1, csl] + cw[2:3] * u
        b_g = _proj(xb, w_in_ref, OFF_B + j * POOL_OUT_GROUP, POOL_OUT_GROUP)
        y_a = b_g * conv

        w = POOL_WINDOWS[j]
        psl = slice(j * POOL_GROUP, (j + 1) * POOL_GROUP)
        pj = p[:, psl]
        wsum = pj
        for k in range(POOL_STATE - (w - 1), POOL_STATE):
            wsum = wsum + ps_ref[k, :, psl]
        q = wsum * (1.0 / w) - pj
        y_p = jnp.dot(q.astype(jnp.bfloat16), pool_w_ref[j],
                      preferred_element_type=jnp.float32) * pool_scale_ref[:, csl]

        g_a = _proj(xb, w_in_ref, OFF_GA + j * POOL_OUT_GROUP, POOL_OUT_GROUP)
        g_p = _proj(xb, w_in_ref, OFF_GP + j * POOL_OUT_GROUP, POOL_OUT_GROUP)
        m = jax.nn.sigmoid(g_a) * y_a + jax.nn.sigmoid(g_p) * y_p
        m_buf[:, csl] = m.astype(jnp.bfloat16)

    out = jnp.dot(m_buf[...], w_out_ref[...], preferred_element_type=jnp.float32)
    o_ref[...] = _layer_norm(ALPHA * x + out, g_ref[...], b_ref[...])

    conv_out_ref[:, 0:CONV_WIDTH - 2, :] = cs_ref[:, 1:CONV_WIDTH - 1, :]
    pool_out_ref[0:POOL_STATE - 1] = ps_ref[1:POOL_STATE]
    pool_out_ref[POOL_STATE - 1] = p


def _mixer_sample(x2d, cs, ps, w_in, conv_w, pool_w, pool_scale, w_out, g, b):
    nb = x2d.shape[0]
    shapes = [x2d.shape, cs.shape, ps.shape, w_in.shape, conv_w.shape, pool_w.shape,
              pool_scale.shape, w_out.shape, g.shape, b.shape]
    return pl.pallas_call(
        _mixer_sample_kernel,
        out_shape=(
            jax.ShapeDtypeStruct((nb, D_MODEL), jnp.float32),
            jax.ShapeDtypeStruct(cs.shape, jnp.float32),
            jax.ShapeDtypeStruct(ps.shape, jnp.float32),
        ),
        grid=(1,),
        in_specs=[_const_spec(s) for s in shapes],
        out_specs=(
            pl.BlockSpec((nb, D_MODEL), lambda i: (0, 0)),
            pl.BlockSpec(cs.shape, lambda i: (0, 0, 0)),
            pl.BlockSpec(ps.shape, lambda i: (0, 0, 0)),
        ),
        scratch_shapes=[pltpu.VMEM((nb, D_MODEL), jnp.bfloat16)],
        compiler_params=pltpu.CompilerParams(
            dimension_semantics=("arbitrary",), vmem_limit_bytes=VMEM_LIMIT_BYTES),
        name="mixer_sample",
    )(x2d, cs, ps, w_in, conv_w, pool_w, pool_scale, w_out, g, b)


def kernel(x_prompt, x_sample, state_conv, state_pool, ln1_g, ln1_b, ffn1_wg, ffn1_wu, ffn1_wd,
           w_in, conv_w, pool_w, pool_scale, w_out, ln2_g, ln2_b,
           ffn2_wg, ffn2_wu, ffn2_wd, ln3_g, ln3_b):
    bsz, seq, _ = x_prompt.shape
    nb = x_sample.shape[0]
    assert x_sample.shape[1] == 1 and DEPTH == 1
    assert seq % ROW_BLOCK == 0

    yp = x_prompt.reshape(bsz * seq, D_MODEL)
    l = 0
    wg1, wu1, wd1 = ffn1_wg[l], ffn1_wu[l], ffn1_wd[l]
    g1, b1 = ln1_g[l][None], ln1_b[l][None]
    g2, b2 = ln2_g[l][None], ln2_b[l][None]
    g3, b3 = ln3_g[l][None], ln3_b[l][None]
    scale = pool_scale[l][None]

    pool_w2d = pool_w[l].reshape(N_POOL_GROUPS * POOL_GROUP, POOL_OUT_GROUP)
    yp, ys, (w_in_b, w_out_b, pool_w_b) = _ffn(yp, x_sample.reshape(nb, D_MODEL), wg1, wu1, wd1,
                                               g1, b1, ROW_BLOCK, cast=(w_in[l], w_out[l], pool_w2d))
    pool_w_b = pool_w_b.reshape(N_POOL_GROUPS, POOL_GROUP, POOL_OUT_GROUP)
    yp, conv_p, pool_p, (wg2, wu2, wd2) = _mixer_prompt(
        yp.reshape(bsz, seq, D_MODEL), w_in_b, conv_w[l], pool_w_b, scale, w_out_b, g2, b2,
        ROW_BLOCK, cast=(ffn2_wg[l], ffn2_wu[l], ffn2_wd[l]))
    ps_rows = jnp.transpose(state_pool[l], (1, 0, 2))
    ys, conv_s, pool_s = _mixer_sample(ys, state_conv[l], ps_rows, w_in_b, conv_w[l],
                                       pool_w_b, scale, w_out_b, g2, b2)
    yp, ys, _ = _ffn(yp.reshape(bsz * seq, D_MODEL), ys, wg2, wu2, wd2, g3, b3, ROW_BLOCK,
                     sample_out_shape=x_sample.shape)

    return (yp.reshape(bsz, seq, D_MODEL),
            ys,
            conv_p[None],
            pool_p[None],
            conv_s[None],
            jnp.transpose(pool_s, (1, 0, 2))[None])
```

```python
import functools

import jax
import jax.numpy as jnp
from jax import lax
from jax.experimental import pallas as pl
from jax.experimental.pallas import tpu as pltpu

D_MODEL = 1024
D_CONV = D_MODEL
D_POOL = D_MODEL // 2
N_POOL_GROUPS = 4
POOL_GROUP = D_POOL // N_POOL_GROUPS
POOL_OUT_GROUP = D_MODEL // N_POOL_GROUPS
POOL_WINDOWS = (2, 4, 8, 16)
POOL_STATE = max(POOL_WINDOWS) - 1
CONV_WIDTH = 3
D_FF = 2816
DEPTH = 1
ALPHA = (2.0 * DEPTH) ** 0.25
LN_EPS = 1e-5
D_IN = 3 * D_CONV + D_POOL + 2 * D_MODEL

OFF_C = 0
OFF_B = D_CONV
OFF_H = 2 * D_CONV
OFF_P = 3 * D_CONV
OFF_GA = 3 * D_CONV + D_POOL
OFF_GP = 3 * D_CONV + D_POOL + D_MODEL

ROW_BLOCK = 1024
SUB_ROWS = 256
FF_CHUNK = 256
STAGE_CHUNKS = 16
STAGE_SLOTS = 6
POOL_HALO = 16
CONV_HALO = 8
VMEM_LIMIT_BYTES = 52 * 1024 * 1024


def _layer_norm(v, g, b):
    mu = jnp.mean(v, axis=-1, keepdims=True)
    d = v - mu
    var = jnp.mean(d * d, axis=-1, keepdims=True)
    return d * lax.rsqrt(var + LN_EPS) * g + b


def _const_spec(shape):
    zeros = (0,) * len(shape)
    return pl.BlockSpec(shape, lambda *_: zeros, pipeline_mode=pl.Buffered(1))


def _cast_specs(arrays, n_steps, step_index):
    in_specs, out_specs, out_shapes = [], [], []
    for a in arrays:
        rows, cols = a.shape
        assert rows % (n_steps * 16) == 0
        spec = pl.BlockSpec((rows // n_steps, cols), lambda *idx: (step_index(*idx), 0))
        in_specs.append(spec)
        out_specs.append(spec)
        out_shapes.append(jax.ShapeDtypeStruct(a.shape, jnp.bfloat16))
    return in_specs, out_specs, out_shapes


def _run_casts(src_refs, dst_refs):
    for src, dst in zip(src_refs, dst_refs):
        dst[...] = src[...].astype(jnp.bfloat16)


def _ffn_rows(x, wg_ref, wu_ref, wd_ref, g_ref, b_ref, h_view):
    xb = x.astype(jnp.bfloat16)
    for c in range(D_FF // FF_CHUNK):
        sl = slice(c * FF_CHUNK, (c + 1) * FF_CHUNK)
        gate = jnp.dot(xb, wg_ref[:, sl], preferred_element_type=jnp.float32)
        up = jnp.dot(xb, wu_ref[:, sl], preferred_element_type=jnp.float32)
        h_view[:, sl] = (gate * jax.nn.sigmoid(gate) * up).astype(jnp.bfloat16)
    y = jnp.dot(h_view[...], wd_ref[...], preferred_element_type=jnp.float32)
    return _layer_norm(ALPHA * x + 0.5 * y, g_ref[...], b_ref[...])


def _stage_cast(w_hbm, w_vmem, stage, sem):
    slots, chunk = stage.shape[0], stage.shape[1]
    n = w_hbm.shape[0] // chunk
    ahead = slots - 1

    def rows(k):
        return pl.ds(pl.multiple_of(k * chunk, chunk), chunk)

    def copy(k):
        return pltpu.make_async_copy(w_hbm.at[rows(k), :], stage.at[k % slots], sem.at[k % slots])

    for k in range(min(ahead, n)):
        copy(k).start()

    def body(k, carry):
        @pl.when(k + ahead < n)
        def _():
            copy(k + ahead).start()

        copy(k).wait()
        w_vmem[rows(k), :] = stage[k % slots].astype(jnp.bfloat16)
        return carry

    lax.fori_loop(0, n, body, 0)


def _ffn_kernel(xp_ref, xs_ref, wg_ref, wu_ref, wd_ref, g_ref, b_ref, *rest, sub, n_cast,
                stage_weights):
    cast_src = rest[:n_cast]
    op_ref, os_ref, *cast_dst = rest[n_cast:2 * n_cast + 2]
    h_ref, *stage_scratch = rest[2 * n_cast + 2:]
    i = pl.program_id(0)
    last = pl.num_programs(0) - 1

    if stage_weights:
        wg_v, wu_v, wd_v, stage_up, stage_down, sem = stage_scratch

        @pl.when(i == 0)
        def _():
            _stage_cast(wg_ref, wg_v, stage_up, sem)
            _stage_cast(wu_ref, wu_v, stage_up, sem)
            _stage_cast(wd_ref, wd_v, stage_down, sem)

        wg_ref, wu_ref, wd_ref = wg_v, wu_v, wd_v
    weights = (wg_ref, wu_ref, wd_ref, g_ref, b_ref)

    @pl.when(i < last)
    def _():
        for s in range(xp_ref.shape[0] // sub):
            rows = pl.ds(s * sub, sub)
            h_view = h_ref.at[pl.ds((s % 2) * sub, sub), :]
            op_ref[rows, :] = _ffn_rows(xp_ref[rows, :], *weights, h_view)
        _run_casts(cast_src, cast_dst)

    @pl.when(i == last)
    def _():
        ns = xs_ref.shape[0]
        xs = xs_ref[...] if len(xs_ref.shape) == 2 else xs_ref[:, 0, :]
        ys = _ffn_rows(xs, *weights, h_ref.at[pl.ds(0, ns), :])
        os_ref[...] = ys if len(os_ref.shape) == 2 else ys[:, None, :]


def _ffn(xp, xs, wg, wu, wd, g, b, tm, cast=(), sample_out_shape=None):
    n, ns = xp.shape[0], xs.shape[0]
    nt = n // tm
    assert ns <= tm
    sample_out_shape = sample_out_shape or (ns, D_MODEL)
    sample_zeros = (0,) * len(sample_out_shape)
    blk = lambda i: (jnp.minimum(i, nt - 1), 0)
    cast_in, cast_out, cast_shapes = _cast_specs(cast, nt, lambda i: jnp.minimum(i, nt - 1))
    stage_weights = wg.dtype == jnp.float32
    assert wu.dtype == wg.dtype and wd.dtype == wg.dtype
    sub = min(tm, SUB_ROWS)
    scratch = [pltpu.VMEM((min(tm, 2 * sub), D_FF), jnp.bfloat16)]
    if stage_weights:
        weight_specs = [pl.BlockSpec(memory_space=pl.ANY)] * 3
        scratch += [
            pltpu.VMEM((D_MODEL, D_FF), jnp.bfloat16),
            pltpu.VMEM((D_MODEL, D_FF), jnp.bfloat16),
            pltpu.VMEM((D_FF, D_MODEL), jnp.bfloat16),
            pltpu.VMEM((STAGE_SLOTS, D_MODEL // STAGE_CHUNKS, D_FF), jnp.float32),
            pltpu.VMEM((STAGE_SLOTS, D_FF // STAGE_CHUNKS, D_MODEL), jnp.float32),
            pltpu.SemaphoreType.DMA((STAGE_SLOTS,)),
        ]
    else:
        weight_specs = [_const_spec((D_MODEL, D_FF)), _const_spec((D_MODEL, D_FF)),
                        _const_spec((D_FF, D_MODEL))]
    outs = pl.pallas_call(
        functools.partial(_ffn_kernel, sub=sub, n_cast=len(cast),
                          stage_weights=stage_weights),
        out_shape=[jax.ShapeDtypeStruct((n, D_MODEL), jnp.float32),
                   jax.ShapeDtypeStruct(sample_out_shape, jnp.float32)] + cast_shapes,
        grid=(nt + 1,),
        in_specs=[pl.BlockSpec((tm, D_MODEL), blk), _const_spec(xs.shape)] + weight_specs
        + [_const_spec((1, D_MODEL)), _const_spec((1, D_MODEL))] + cast_in,
        out_specs=[pl.BlockSpec((tm, D_MODEL), blk),
                   pl.BlockSpec(sample_out_shape, lambda i: sample_zeros)] + cast_out,
        scratch_shapes=scratch,
        compiler_params=pltpu.CompilerParams(
            dimension_semantics=("arbitrary",), vmem_limit_bytes=VMEM_LIMIT_BYTES),
        name="ffn",
    )(xp, xs, wg, wu, wd, g, b, *cast)
    return outs[0], outs[1], outs[2:]


def _proj(xb, w_in_ref, off, width):
    return jnp.dot(xb, w_in_ref[:, off:off + width], preferred_element_type=jnp.float32)


def _shift_rows(ext, k, halo):
    return pltpu.roll(ext, k, 0)[halo:]


def _mixer_prompt_kernel(x_ref, w_in_ref, conv_w_ref, pool_w_ref, pool_scale_ref, w_out_ref,
                         g_ref, b_ref, *rest, sub, n_cast):
    cast_src = rest[:n_cast]
    o_ref, conv_out_ref, pool_out_ref, *cast_dst = rest[n_cast:2 * n_cast + 3]
    u_carry, p_carry, m_buf = rest[2 * n_cast + 3:]
    t = pl.program_id(1)
    tm = x_ref.shape[1]

    @pl.when(t == 0)
    def _():
        u_carry[...] = jnp.zeros(u_carry.shape, jnp.float32)
        p_carry[...] = jnp.zeros(p_carry.shape, jnp.float32)

    u_prev = u_carry[...]
    p_prev = p_carry[...]

    def project_out_and_norm(rows):
        out = jnp.dot(m_buf[rows, :], w_out_ref[...], preferred_element_type=jnp.float32)
        o_ref[0, rows, :] = _layer_norm(ALPHA * x_ref[0, rows, :] + out, g_ref[...], b_ref[...])

    pending_rows = None
    for s in range(tm // sub):
        rows = slice(s * sub, (s + 1) * sub)
        x = x_ref[0, rows, :]
        xb = x.astype(jnp.bfloat16)

        p = _proj(xb, w_in_ref, OFF_P, D_POOL)
        p_ext = jnp.concatenate([p_prev, p], axis=0)
        pos = t * tm + s * sub + lax.broadcasted_iota(jnp.int32, (sub, POOL_GROUP), 0)

        u_tails = []
        for j in range(N_POOL_GROUPS):
            csl = slice(j * POOL_OUT_GROUP, (j + 1) * POOL_OUT_GROUP)
            c_g = _proj(xb, w_in_ref, OFF_C + j * POOL_OUT_GROUP, POOL_OUT_GROUP)
            h = _proj(xb, w_in_ref, OFF_H + j * POOL_OUT_GROUP, POOL_OUT_GROUP)
            u = c_g * h
            u_ext = jnp.concatenate([u_prev[:, csl], u], axis=0)
            u_tails.append(u[sub - CONV_HALO:, :])
            cw = conv_w_ref[:, csl]
            conv = (cw[0:1] * _shift_rows(u_ext, 2, CONV_HALO)
                    + cw[1:2] * _shift_rows(u_ext, 1, CONV_HALO)
                    + cw[2:3] * u)
            b_g = _proj(xb, w_in_ref, OFF_B + j * POOL_OUT_GROUP, POOL_OUT_GROUP)
            y_a = b_g * conv

            psl = slice(j * POOL_GROUP, (j + 1) * POOL_GROUP)
            w = POOL_WINDOWS[j]
            acc = p_ext[:, psl]
            span = 1
            while span < w:
                acc = acc + pltpu.roll(acc, span, 0)
                span *= 2
            inv_cnt = 1.0 / jnp.minimum(pos + 1, w).astype(jnp.float32)
            q = acc[POOL_HALO:] * inv_cnt - p[:, psl]
            y_p = jnp.dot(q.astype(jnp.bfloat16), pool_w_ref[j],
                          preferred_element_type=jnp.float32) * pool_scale_ref[:, csl]

            g_a = _proj(xb, w_in_ref, OFF_GA + j * POOL_OUT_GROUP, POOL_OUT_GROUP)
            g_p = _proj(xb, w_in_ref, OFF_GP + j * POOL_OUT_GROUP, POOL_OUT_GROUP)
            m = jax.nn.sigmoid(g_a) * y_a + jax.nn.sigmoid(g_p) * y_p
            m_buf[rows, csl] = m.astype(jnp.bfloat16)

            if j == 0 and pending_rows is not None:
                project_out_and_norm(pending_rows)

        pending_rows = rows
        u_prev = jnp.concatenate(u_tails, axis=1)
        p_prev = p_ext[sub:, :]

    project_out_and_norm(pending_rows)
    _run_casts(cast_src, cast_dst)

    u_carry[...] = u_prev
    p_carry[...] = p_prev
    conv_out_ref[0] = u_carry[CONV_HALO - (CONV_WIDTH - 1):CONV_HALO, :]
    pool_out_ref[0] = p_carry[POOL_HALO - POOL_STATE:POOL_HALO, :]


def _mixer_prompt(x, w_in, conv_w, pool_w, pool_scale, w_out, g, b, tm, cast=()):
    bsz, seq, _ = x.shape
    tiles = seq // tm
    cast_in, cast_out, cast_shapes = _cast_specs(cast, bsz * tiles, lambda s, t: s * tiles + t)
    kern = functools.partial(_mixer_prompt_kernel, sub=min(tm, SUB_ROWS), n_cast=len(cast))
    outs = pl.pallas_call(
        kern,
        out_shape=[
            jax.ShapeDtypeStruct((bsz, seq, D_MODEL), jnp.float32),
            jax.ShapeDtypeStruct((bsz, CONV_WIDTH - 1, D_CONV), jnp.float32),
            jax.ShapeDtypeStruct((bsz, POOL_STATE, D_POOL), jnp.float32),
        ] + cast_shapes,
        grid=(bsz, tiles),
        in_specs=[
            pl.BlockSpec((1, tm, D_MODEL), lambda s, t: (s, t, 0)),
            _const_spec((D_MODEL, D_IN)),
            _const_spec((CONV_WIDTH, D_CONV)),
            _const_spec((N_POOL_GROUPS, POOL_GROUP, POOL_OUT_GROUP)),
            _const_spec((1, D_MODEL)),
            _const_spec((D_MODEL, D_MODEL)),
            _const_spec((1, D_MODEL)),
            _const_spec((1, D_MODEL)),
        ] + cast_in,
        out_specs=[
            pl.BlockSpec((1, tm, D_MODEL), lambda s, t: (s, t, 0)),
            pl.BlockSpec((1, CONV_WIDTH - 1, D_CONV), lambda s, t: (s, 0, 0)),
            pl.BlockSpec((1, POOL_STATE, D_POOL), lambda s, t: (s, 0, 0)),
        ] + cast_out,
        scratch_shapes=[
            pltpu.VMEM((CONV_HALO, D_CONV), jnp.float32),
            pltpu.VMEM((POOL_HALO, D_POOL), jnp.float32),
            pltpu.VMEM((tm, D_MODEL), jnp.bfloat16),
        ],
        compiler_params=pltpu.CompilerParams(
            dimension_semantics=("arbitrary", "arbitrary"), vmem_limit_bytes=VMEM_LIMIT_BYTES),
        name="mixer_prompt",
    )(x, w_in, conv_w, pool_w, pool_scale, w_out, g, b, *cast)
    return outs[0], outs[1], outs[2], outs[3:]


def _mixer_sample_kernel(x_ref, cs_ref, ps_ref, wc_ref, wb_ref, wh_ref, wp_ref, wga_ref, wgp_ref,
                         conv_w_ref, pool_w_ref, pool_scale_ref, w_out_ref, g_ref, b_ref,
                         o_ref, conv_out_ref, pool_out_ref, acc_ref):
    j = pl.program_id(0)
    x = x_ref[...]
    xb = x.astype(jnp.bfloat16)

    def proj(w_ref):
        return jnp.dot(xb, w_ref[...], preferred_element_type=jnp.float32)

    u = proj(wc_ref) * proj(wh_ref)
    conv_out_ref[:, CONV_WIDTH - 2:CONV_WIDTH - 1, :] = u[:, None, :]
    cw = conv_w_ref[...]
    conv = cw[0:1] * cs_ref[:, 0, :] + cw[1:2] * cs_ref[:, 1, :] + cw[2:3] * u
    y_a = proj(wb_ref) * conv

    p = proj(wp_ref)
    wsum = p
    inv_w = jnp.float32(1.0)
    hi = POOL_STATE
    for gi, w in enumerate(POOL_WINDOWS):
        lo = POOL_STATE - (w - 1)
        seg = ps_ref[lo]
        for k in range(lo + 1, hi):
            seg = seg + ps_ref[k]
        wsum = wsum + jnp.where(j >= gi, seg, 0.0)
        inv_w = jnp.where(j == gi, jnp.float32(1.0 / w), inv_w)
        hi = lo
    q = wsum * inv_w - p
    y_p = jnp.dot(q.astype(jnp.bfloat16), pool_w_ref[0],
                  preferred_element_type=jnp.float32) * pool_scale_ref[...]

    m = jax.nn.sigmoid(proj(wga_ref)) * y_a + jax.nn.sigmoid(proj(wgp_ref)) * y_p
    part_out = jnp.dot(m.astype(jnp.bfloat16), w_out_ref[...], preferred_element_type=jnp.float32)

    @pl.when(j == 0)
    def _():
        acc_ref[...] = part_out

    @pl.when(j > 0)
    def _():
        acc_ref[...] += part_out

    @pl.when(j == pl.num_programs(0) - 1)
    def _():
        o_ref[...] = _layer_norm(ALPHA * x + acc_ref[...], g_ref[...], b_ref[...])

    conv_out_ref[:, 0:CONV_WIDTH - 2, :] = cs_ref[:, 1:CONV_WIDTH - 1, :]
    pool_out_ref[0:POOL_STATE - 1] = ps_ref[1:POOL_STATE]
    pool_out_ref[POOL_STATE - 1] = p


def _mixer_sample(x2d, cs, ps, w_in, conv_w, pool_w, pool_scale, w_out, g, b):
    nb = x2d.shape[0]
    cg, pg = POOL_OUT_GROUP, POOL_GROUP

    def w_in_block(off, width):
        return pl.BlockSpec((D_MODEL, width), lambda j: (0, off // width + j))

    return pl.pallas_call(
        _mixer_sample_kernel,
        out_shape=(
            jax.ShapeDtypeStruct((nb, D_MODEL), jnp.float32),
            jax.ShapeDtypeStruct(cs.shape, jnp.float32),
            jax.ShapeDtypeStruct(ps.shape, jnp.float32),
        ),
        grid=(N_POOL_GROUPS,),
        in_specs=[
            _const_spec(x2d.shape),
            pl.BlockSpec((nb, CONV_WIDTH - 1, cg), lambda j: (0, 0, j)),
            pl.BlockSpec((POOL_STATE, nb, pg), lambda j: (0, 0, j)),
            w_in_block(OFF_C, cg), w_in_block(OFF_B, cg), w_in_block(OFF_H, cg),
            w_in_block(OFF_P, pg), w_in_block(OFF_GA, cg), w_in_block(OFF_GP, cg),
            pl.BlockSpec((CONV_WIDTH, cg), lambda j: (0, j)),
            pl.BlockSpec((1, pg, cg), lambda j: (j, 0, 0)),
            pl.BlockSpec((1, cg), lambda j: (0, j)),
            pl.BlockSpec((cg, D_MODEL), lambda j: (j, 0)),
            _const_spec(g.shape),
            _const_spec(b.shape),
        ],
        out_specs=(
            pl.BlockSpec((nb, D_MODEL), lambda j: (0, 0)),
            pl.BlockSpec((nb, CONV_WIDTH - 1, cg), lambda j: (0, 0, j)),
            pl.BlockSpec((POOL_STATE, nb, pg), lambda j: (0, 0, j)),
        ),
        scratch_shapes=[pltpu.VMEM((nb, D_MODEL), jnp.float32)],
        compiler_params=pltpu.CompilerParams(
            dimension_semantics=("arbitrary",), vmem_limit_bytes=VMEM_LIMIT_BYTES),
        name="mixer_sample",
    )(x2d, cs, ps, w_in, w_in, w_in, w_in, w_in, w_in, conv_w, pool_w, pool_scale, w_out, g, b)


def kernel(x_prompt, x_sample, state_conv, state_pool, ln1_g, ln1_b, ffn1_wg, ffn1_wu, ffn1_wd,
           w_in, conv_w, pool_w, pool_scale, w_out, ln2_g, ln2_b,
           ffn2_wg, ffn2_wu, ffn2_wd, ln3_g, ln3_b):
    bsz, seq, _ = x_prompt.shape
    nb = x_sample.shape[0]
    assert x_sample.shape[1] == 1 and DEPTH == 1
    assert seq % ROW_BLOCK == 0

    yp = x_prompt.reshape(bsz * seq, D_MODEL)
    l = 0
    wg1, wu1, wd1 = ffn1_wg[l], ffn1_wu[l], ffn1_wd[l]
    g1, b1 = ln1_g[l][None], ln1_b[l][None]
    g2, b2 = ln2_g[l][None], ln2_b[l][None]
    g3, b3 = ln3_g[l][None], ln3_b[l][None]
    scale = pool_scale[l][None]

    pool_w2d = pool_w[l].reshape(N_POOL_GROUPS * POOL_GROUP, POOL_OUT_GROUP)
    yp, ys, (w_in_b, w_out_b, pool_w_b) = _ffn(yp, x_sample.reshape(nb, D_MODEL), wg1, wu1, wd1,
                                               g1, b1, ROW_BLOCK, cast=(w_in[l], w_out[l], pool_w2d))
    pool_w_b = pool_w_b.reshape(N_POOL_GROUPS, POOL_GROUP, POOL_OUT_GROUP)
    yp, conv_p, pool_p, (wg2, wu2, wd2) = _mixer_prompt(
        yp.reshape(bsz, seq, D_MODEL), w_in_b, conv_w[l], pool_w_b, scale, w_out_b, g2, b2,
        ROW_BLOCK, cast=(ffn2_wg[l], ffn2_wu[l], ffn2_wd[l]))
    ps_rows = jnp.transpose(state_pool[l], (1, 0, 2))
    ys, conv_s, pool_s = _mixer_sample(ys, state_conv[l], ps_rows, w_in_b, conv_w[l],
                                       pool_w_b, scale, w_out_b, g2, b2)
    yp, ys, _ = _ffn(yp.reshape(bsz * seq, D_MODEL), ys, wg2, wu2, wd2, g3, b3, ROW_BLOCK,
                     sample_out_shape=x_sample.shape)

    return (yp.reshape(bsz, seq, D_MODEL),
            ys,
            conv_p[None],
            pool_p[None],
            conv_s[None],
            jnp.transpose(pool_s, (1, 0, 2))[None])
```

```python
import functools

import jax
import jax.numpy as jnp
from jax import lax
from jax.experimental import pallas as pl
from jax.experimental.pallas import tpu as pltpu

D_MODEL = 1024
D_CONV = D_MODEL
D_POOL = D_MODEL // 2
N_POOL_GROUPS = 4
POOL_GROUP = D_POOL // N_POOL_GROUPS
POOL_OUT_GROUP = D_MODEL // N_POOL_GROUPS
POOL_WINDOWS = (2, 4, 8, 16)
POOL_STATE = max(POOL_WINDOWS) - 1
CONV_WIDTH = 3
D_FF = 2816
DEPTH = 1
ALPHA = (2.0 * DEPTH) ** 0.25
LN_EPS = 1e-5
D_IN = 3 * D_CONV + D_POOL + 2 * D_MODEL

OFF_C = 0
OFF_B = D_CONV
OFF_H = 2 * D_CONV
OFF_P = 3 * D_CONV
OFF_GA = 3 * D_CONV + D_POOL
OFF_GP = 3 * D_CONV + D_POOL + D_MODEL

ROW_BLOCK = 1024
SUB_ROWS = 256
FF_CHUNK = 256
STAGE_CHUNKS = 16
STAGE_SLOTS = 6
POOL_HALO = 16
CONV_HALO = 8
VMEM_LIMIT_BYTES = 52 * 1024 * 1024


def _layer_norm(v, g, b):
    mu = jnp.mean(v, axis=-1, keepdims=True)
    d = v - mu
    var = jnp.mean(d * d, axis=-1, keepdims=True)
    return d * lax.rsqrt(var + LN_EPS) * g + b


def _const_spec(shape):
    zeros = (0,) * len(shape)
    return pl.BlockSpec(shape, lambda *_: zeros, pipeline_mode=pl.Buffered(1))


def _cast_specs(arrays, n_steps, step_index):
    in_specs, out_specs, out_shapes = [], [], []
    for a in arrays:
        rows, cols = a.shape
        assert rows % (n_steps * 16) == 0
        spec = pl.BlockSpec((rows // n_steps, cols), lambda *idx: (step_index(*idx), 0))
        in_specs.append(spec)
        out_specs.append(spec)
        out_shapes.append(jax.ShapeDtypeStruct(a.shape, jnp.bfloat16))
    return in_specs, out_specs, out_shapes


def _run_casts(src_refs, dst_refs):
    for src, dst in zip(src_refs, dst_refs):
        dst[...] = src[...].astype(jnp.bfloat16)


def _ffn_rows(x, wg_ref, wu_ref, wd_ref, g_ref, b_ref, h_view):
    xb = x.astype(jnp.bfloat16)
    for c in range(D_FF // FF_CHUNK):
        sl = slice(c * FF_CHUNK, (c + 1) * FF_CHUNK)
        gate = jnp.dot(xb, wg_ref[:, sl], preferred_element_type=jnp.float32)
        up = jnp.dot(xb, wu_ref[:, sl], preferred_element_type=jnp.float32)
        h_view[:, sl] = (gate * jax.nn.sigmoid(gate) * up).astype(jnp.bfloat16)
    y = jnp.dot(h_view[...], wd_ref[...], preferred_element_type=jnp.float32)
    return _layer_norm(ALPHA * x + 0.5 * y, g_ref[...], b_ref[...])


def _stage_cast(w_hbm, w_vmem, stage, sem):
    slots, chunk = stage.shape[0], stage.shape[1]
    n = w_hbm.shape[0] // chunk
    ahead = slots - 1

    def rows(k):
        return pl.ds(pl.multiple_of(k * chunk, chunk), chunk)

    def copy(k):
        return pltpu.make_async_copy(w_hbm.at[rows(k), :], stage.at[k % slots], sem.at[k % slots])

    for k in range(min(ahead, n)):
        copy(k).start()

    def body(k, carry):
        @pl.when(k + ahead < n)
        def _():
            copy(k + ahead).start()

        copy(k).wait()
        w_vmem[rows(k), :] = stage[k % slots].astype(jnp.bfloat16)
        return carry

    lax.fori_loop(0, n, body, 0)


def _ffn_kernel(xp_ref, xs_ref, wg_ref, wu_ref, wd_ref, g_ref, b_ref, *rest, sub, n_cast,
                stage_weights):
    cast_src = rest[:n_cast]
    op_ref, os_ref, *cast_dst = rest[n_cast:2 * n_cast + 2]
    h_ref, *stage_scratch = rest[2 * n_cast + 2:]
    i = pl.program_id(0)
    last = pl.num_programs(0) - 1

    if stage_weights:
        wg_v, wu_v, wd_v, stage_up, stage_down, sem = stage_scratch

        @pl.when(i == 0)
        def _():
            _stage_cast(wg_ref, wg_v, stage_up, sem)
            _stage_cast(wu_ref, wu_v, stage_up, sem)
            _stage_cast(wd_ref, wd_v, stage_down, sem)

        wg_ref, wu_ref, wd_ref = wg_v, wu_v, wd_v
    weights = (wg_ref, wu_ref, wd_ref, g_ref, b_ref)

    @pl.when(i < last)
    def _():
        for s in range(xp_ref.shape[0] // sub):
            rows = pl.ds(s * sub, sub)
            h_view = h_ref.at[pl.ds((s % 2) * sub, sub), :]
            op_ref[rows, :] = _ffn_rows(xp_ref[rows, :], *weights, h_view)
        _run_casts(cast_src, cast_dst)

    @pl.when(i == last)
    def _():
        ns = xs_ref.shape[0]
        xs = xs_ref[...] if len(xs_ref.shape) == 2 else xs_ref[:, 0, :]
        ys = _ffn_rows(xs, *weights, h_ref.at[pl.ds(0, ns), :])
        os_ref[...] = ys if len(os_ref.shape) == 2 else ys[:, None, :]


def _ffn(xp, xs, wg, wu, wd, g, b, tm, cast=(), sample_out_shape=None):
    n, ns = xp.shape[0], xs.shape[0]
    nt = n // tm
    assert ns <= tm
    sample_out_shape = sample_out_shape or (ns, D_MODEL)
    sample_zeros = (0,) * len(sample_out_shape)
    blk = lambda i: (jnp.minimum(i, nt - 1), 0)
    cast_in, cast_out, cast_shapes = _cast_specs(cast, nt, lambda i: jnp.minimum(i, nt - 1))
    stage_weights = wg.dtype == jnp.float32
    assert wu.dtype == wg.dtype and wd.dtype == wg.dtype
    sub = min(tm, SUB_ROWS)
    scratch = [pltpu.VMEM((min(tm, 2 * sub), D_FF), jnp.bfloat16)]
    if stage_weights:
        weight_specs = [pl.BlockSpec(memory_space=pl.ANY)] * 3
        scratch += [
            pltpu.VMEM((D_MODEL, D_FF), jnp.bfloat16),
            pltpu.VMEM((D_MODEL, D_FF), jnp.bfloat16),
            pltpu.VMEM((D_FF, D_MODEL), jnp.bfloat16),
            pltpu.VMEM((STAGE_SLOTS, D_MODEL // STAGE_CHUNKS, D_FF), jnp.float32),
            pltpu.VMEM((STAGE_SLOTS, D_FF // STAGE_CHUNKS, D_MODEL), jnp.float32),
            pltpu.SemaphoreType.DMA((STAGE_SLOTS,)),
        ]
    else:
        weight_specs = [_const_spec((D_MODEL, D_FF)), _const_spec((D_MODEL, D_FF)),
                        _const_spec((D_FF, D_MODEL))]
    outs = pl.pallas_call(
        functools.partial(_ffn_kernel, sub=sub, n_cast=len(cast),
                          stage_weights=stage_weights),
        out_shape=[jax.ShapeDtypeStruct((n, D_MODEL), jnp.float32),
                   jax.ShapeDtypeStruct(sample_out_shape, jnp.float32)] + cast_shapes,
        grid=(nt + 1,),
        in_specs=[pl.BlockSpec((tm, D_MODEL), blk), _const_spec(xs.shape)] + weight_specs
        + [_const_spec((1, D_MODEL)), _const_spec((1, D_MODEL))] + cast_in,
        out_specs=[pl.BlockSpec((tm, D_MODEL), blk),
                   pl.BlockSpec(sample_out_shape, lambda i: sample_zeros)] + cast_out,
        scratch_shapes=scratch,
        compiler_params=pltpu.CompilerParams(
            dimension_semantics=("arbitrary",),
            vmem_limit_bytes=VMEM_LIMIT_BYTES if stage_weights else 44 * 1024 * 1024),
        name="ffn",
    )(xp, xs, wg, wu, wd, g, b, *cast)
    return outs[0], outs[1], outs[2:]


def _proj(xb, w_in_ref, off, width):
    return jnp.dot(xb, w_in_ref[:, off:off + width], preferred_element_type=jnp.float32)


def _shift_rows(ext, k, halo):
    return pltpu.roll(ext, k, 0)[halo:]


def _mixer_prompt_kernel(x_ref, w_in_ref, conv_w_ref, pool_w_ref, pool_scale_ref, w_out_ref,
                         g_ref, b_ref, *rest, sub, n_cast):
    cast_src = rest[:n_cast]
    o_ref, conv_out_ref, pool_out_ref, *cast_dst = rest[n_cast:2 * n_cast + 3]
    u_carry, p_carry, m_buf = rest[2 * n_cast + 3:]
    t = pl.program_id(1)
    tm = x_ref.shape[1]

    @pl.when(t == 0)
    def _():
        u_carry[...] = jnp.zeros(u_carry.shape, jnp.float32)
        p_carry[...] = jnp.zeros(p_carry.shape, jnp.float32)

    u_prev = u_carry[...]
    p_prev = p_carry[...]

    def project_out_and_norm(rows):
        out = jnp.dot(m_buf[rows, :], w_out_ref[...], preferred_element_type=jnp.float32)
        o_ref[0, rows, :] = _layer_norm(ALPHA * x_ref[0, rows, :] + out, g_ref[...], b_ref[...])

    pending_rows = None
    for s in range(tm // sub):
        rows = slice(s * sub, (s + 1) * sub)
        x = x_ref[0, rows, :]
        xb = x.astype(jnp.bfloat16)

        p = _proj(xb, w_in_ref, OFF_P, D_POOL)
        p_ext = jnp.concatenate([p_prev, p], axis=0)
        pos = t * tm + s * sub + lax.broadcasted_iota(jnp.int32, (sub, POOL_GROUP), 0)

        u_tails = []
        for j in range(N_POOL_GROUPS):
            csl = slice(j * POOL_OUT_GROUP, (j + 1) * POOL_OUT_GROUP)
            c_g = _proj(xb, w_in_ref, OFF_C + j * POOL_OUT_GROUP, POOL_OUT_GROUP)
            h = _proj(xb, w_in_ref, OFF_H + j * POOL_OUT_GROUP, POOL_OUT_GROUP)
            u = c_g * h
            u_ext = jnp.concatenate([u_prev[:, csl], u], axis=0)
            u_tails.append(u[sub - CONV_HALO:, :])
            cw = conv_w_ref[:, csl]
            conv = (cw[0:1] * _shift_rows(u_ext, 2, CONV_HALO)
                    + cw[1:2] * _shift_rows(u_ext, 1, CONV_HALO)
                    + cw[2:3] * u)
            b_g = _proj(xb, w_in_ref, OFF_B + j * POOL_OUT_GROUP, POOL_OUT_GROUP)
            y_a = b_g * conv

            psl = slice(j * POOL_GROUP, (j + 1) * POOL_GROUP)
            w = POOL_WINDOWS[j]
            acc = p_ext[:, psl]
            span = 1
            while span < w:
                acc = acc + pltpu.roll(acc, span, 0)
                span *= 2
            inv_cnt = 1.0 / jnp.minimum(pos + 1, w).astype(jnp.float32)
            q = acc[POOL_HALO:] * inv_cnt - p[:, psl]
            y_p = jnp.dot(q.astype(jnp.bfloat16), pool_w_ref[j],
                          preferred_element_type=jnp.float32) * pool_scale_ref[:, csl]

            g_a = _proj(xb, w_in_ref, OFF_GA + j * POOL_OUT_GROUP, POOL_OUT_GROUP)
            g_p = _proj(xb, w_in_ref, OFF_GP + j * POOL_OUT_GROUP, POOL_OUT_GROUP)
            m = jax.nn.sigmoid(g_a) * y_a + jax.nn.sigmoid(g_p) * y_p
            m_buf[rows, csl] = m.astype(jnp.bfloat16)

            if j == 0 and pending_rows is not None:
                project_out_and_norm(pending_rows)

        pending_rows = rows
        u_prev = jnp.concatenate(u_tails, axis=1)
        p_prev = p_ext[sub:, :]

    project_out_and_norm(pending_rows)
    _run_casts(cast_src, cast_dst)

    u_carry[...] = u_prev
    p_carry[...] = p_prev
    conv_out_ref[0] = u_carry[CONV_HALO - (CONV_WIDTH - 1):CONV_HALO, :]
    pool_out_ref[0] = p_carry[POOL_HALO - POOL_STATE:POOL_HALO, :]


def _mixer_prompt(x, w_in, conv_w, pool_w, pool_scale, w_out, g, b, tm, cast=()):
    bsz, seq, _ = x.shape
    tiles = seq // tm
    cast_in, cast_out, cast_shapes = _cast_specs(cast, bsz * tiles, lambda s, t: s * tiles + t)
    kern = functools.partial(_mixer_prompt_kernel, sub=min(tm, SUB_ROWS), n_cast=len(cast))
    outs = pl.pallas_call(
        kern,
        out_shape=[
            jax.ShapeDtypeStruct((bsz, seq, D_MODEL), jnp.float32),
            jax.ShapeDtypeStruct((bsz, CONV_WIDTH - 1, D_CONV), jnp.float32),
            jax.ShapeDtypeStruct((bsz, POOL_STATE, D_POOL), jnp.float32),
        ] + cast_shapes,
        grid=(bsz, tiles),
        in_specs=[
            pl.BlockSpec((1, tm, D_MODEL), lambda s, t: (s, t, 0)),
            _const_spec((D_MODEL, D_IN)),
            _const_spec((CONV_WIDTH, D_CONV)),
            _const_spec((N_POOL_GROUPS, POOL_GROUP, POOL_OUT_GROUP)),
            _const_spec((1, D_MODEL)),
            _const_spec((D_MODEL, D_MODEL)),
            _const_spec((1, D_MODEL)),
            _const_spec((1, D_MODEL)),
        ] + cast_in,
        out_specs=[
            pl.BlockSpec((1, tm, D_MODEL), lambda s, t: (s, t, 0)),
            pl.BlockSpec((1, CONV_WIDTH - 1, D_CONV), lambda s, t: (s, 0, 0)),
            pl.BlockSpec((1, POOL_STATE, D_POOL), lambda s, t: (s, 0, 0)),
        ] + cast_out,
        scratch_shapes=[
            pltpu.VMEM((CONV_HALO, D_CONV), jnp.float32),
            pltpu.VMEM((POOL_HALO, D_POOL), jnp.float32),
            pltpu.VMEM((tm, D_MODEL), jnp.bfloat16),
        ],
        compiler_params=pltpu.CompilerParams(
            dimension_semantics=("arbitrary", "arbitrary"), vmem_limit_bytes=44 * 1024 * 1024),
        name="mixer_prompt",
    )(x, w_in, conv_w, pool_w, pool_scale, w_out, g, b, *cast)
    return outs[0], outs[1], outs[2], outs[3:]


def _mixer_sample_kernel(x_ref, cs_ref, ps_ref, wc_ref, wb_ref, wh_ref, wp_ref, wga_ref, wgp_ref,
                         conv_w_ref, pool_w_ref, pool_scale_ref, w_out_ref, g_ref, b_ref,
                         o_ref, conv_out_ref, pool_out_ref, acc_ref):
    j = pl.program_id(0)
    x = x_ref[...]
    xb = x.astype(jnp.bfloat16)

    def proj(w_ref):
        return jnp.dot(xb, w_ref[...], preferred_element_type=jnp.float32)

    u = proj(wc_ref) * proj(wh_ref)
    conv_out_ref[:, CONV_WIDTH - 2:CONV_WIDTH - 1, :] = u[:, None, :]
    cw = conv_w_ref[...]
    conv = cw[0:1] * cs_ref[:, 0, :] + cw[1:2] * cs_ref[:, 1, :] + cw[2:3] * u
    y_a = proj(wb_ref) * conv

    p = proj(wp_ref)
    wsum = p
    inv_w = jnp.float32(1.0)
    hi = POOL_STATE
    for gi, w in enumerate(POOL_WINDOWS):
        lo = POOL_STATE - (w - 1)
        seg = ps_ref[lo]
        for k in range(lo + 1, hi):
            seg = seg + ps_ref[k]
        wsum = wsum + jnp.where(j >= gi, seg, 0.0)
        inv_w = jnp.where(j == gi, jnp.float32(1.0 / w), inv_w)
        hi = lo
    q = wsum * inv_w - p
    y_p = jnp.dot(q.astype(jnp.bfloat16), pool_w_ref[0],
                  preferred_element_type=jnp.float32) * pool_scale_ref[...]

    m = jax.nn.sigmoid(proj(wga_ref)) * y_a + jax.nn.sigmoid(proj(wgp_ref)) * y_p
    part_out = jnp.dot(m.astype(jnp.bfloat16), w_out_ref[...], preferred_element_type=jnp.float32)

    @pl.when(j == 0)
    def _():
        acc_ref[...] = part_out

    @pl.when(j > 0)
    def _():
        acc_ref[...] += part_out

    @pl.when(j == pl.num_programs(0) - 1)
    def _():
        o_ref[...] = _layer_norm(ALPHA * x + acc_ref[...], g_ref[...], b_ref[...])

    conv_out_ref[:, 0:CONV_WIDTH - 2, :] = cs_ref[:, 1:CONV_WIDTH - 1, :]
    pool_out_ref[0:POOL_STATE - 1] = ps_ref[1:POOL_STATE]
    pool_out_ref[POOL_STATE - 1] = p


def _mixer_sample(x2d, cs, ps, w_in, conv_w, pool_w, pool_scale, w_out, g, b):
    nb = x2d.shape[0]
    cg, pg = POOL_OUT_GROUP, POOL_GROUP

    def w_in_block(off, width):
        return pl.BlockSpec((D_MODEL, width), lambda j: (0, off // width + j))

    return pl.pallas_call(
        _mixer_sample_kernel,
        out_shape=(
            jax.ShapeDtypeStruct((nb, D_MODEL), jnp.float32),
            jax.ShapeDtypeStruct(cs.shape, jnp.float32),
            jax.ShapeDtypeStruct(ps.shape, jnp.float32),
        ),
        grid=(N_POOL_GROUPS,),
        in_specs=[
            _const_spec(x2d.shape),
            pl.BlockSpec((nb, CONV_WIDTH - 1, cg), lambda j: (0, 0, j)),
            pl.BlockSpec((POOL_STATE, nb, pg), lambda j: (0, 0, j)),
            w_in_block(OFF_C, cg), w_in_block(OFF_B, cg), w_in_block(OFF_H, cg),
            w_in_block(OFF_P, pg), w_in_block(OFF_GA, cg), w_in_block(OFF_GP, cg),
            pl.BlockSpec((CONV_WIDTH, cg), lambda j: (0, j)),
            pl.BlockSpec((1, pg, cg), lambda j: (j, 0, 0)),
            pl.BlockSpec((1, cg), lambda j: (0, j)),
            pl.BlockSpec((cg, D_MODEL), lambda j: (j, 0)),
            _const_spec(g.shape),
            _const_spec(b.shape),
        ],
        out_specs=(
            pl.BlockSpec((nb, D_MODEL), lambda j: (0, 0)),
            pl.BlockSpec((nb, CONV_WIDTH - 1, cg), lambda j: (0, 0, j)),
            pl.BlockSpec((POOL_STATE, nb, pg), lambda j: (0, 0, j)),
        ),
        scratch_shapes=[pltpu.VMEM((nb, D_MODEL), jnp.float32)],
        compiler_params=pltpu.CompilerParams(
            dimension_semantics=("arbitrary",), vmem_limit_bytes=24 * 1024 * 1024),
        name="mixer_sample",
    )(x2d, cs, ps, w_in, w_in, w_in, w_in, w_in, w_in, conv_w, pool_w, pool_scale, w_out, g, b)


def kernel(x_prompt, x_sample, state_conv, state_pool, ln1_g, ln1_b, ffn1_wg, ffn1_wu, ffn1_wd,
           w_in, conv_w, pool_w, pool_scale, w_out, ln2_g, ln2_b,
           ffn2_wg, ffn2_wu, ffn2_wd, ln3_g, ln3_b):
    bsz, seq, _ = x_prompt.shape
    nb = x_sample.shape[0]
    assert x_sample.shape[1] == 1 and DEPTH == 1
    assert seq % ROW_BLOCK == 0

    yp = x_prompt.reshape(bsz * seq, D_MODEL)
    l = 0
    wg1, wu1, wd1 = ffn1_wg[l], ffn1_wu[l], ffn1_wd[l]
    g1, b1 = ln1_g[l][None], ln1_b[l][None]
    g2, b2 = ln2_g[l][None], ln2_b[l][None]
    g3, b3 = ln3_g[l][None], ln3_b[l][None]
    scale = pool_scale[l][None]

    pool_w2d = pool_w[l].reshape(N_POOL_GROUPS * POOL_GROUP, POOL_OUT_GROUP)
    yp, ys, (w_in_b, w_out_b, pool_w_b) = _ffn(yp, x_sample.reshape(nb, D_MODEL), wg1, wu1, wd1,
                                               g1, b1, ROW_BLOCK, cast=(w_in[l], w_out[l], pool_w2d))
    pool_w_b = pool_w_b.reshape(N_POOL_GROUPS, POOL_GROUP, POOL_OUT_GROUP)
    yp, conv_p, pool_p, (wg2, wu2, wd2) = _mixer_prompt(
        yp.reshape(bsz, seq, D_MODEL), w_in_b, conv_w[l], pool_w_b, scale, w_out_b, g2, b2,
        ROW_BLOCK, cast=(ffn2_wg[l], ffn2_wu[l], ffn2_wd[l]))
    ps_rows = jnp.transpose(state_pool[l], (1, 0, 2))
    ys, conv_s, pool_s = _mixer_sample(ys, state_conv[l], ps_rows, w_in_b, conv_w[l],
                                       pool_w_b, scale, w_out_b, g2, b2)
    yp, ys, _ = _ffn(yp.reshape(bsz * seq, D_MODEL), ys, wg2, wu2, wd2, g3, b3, ROW_BLOCK,
                     sample_out_shape=x_sample.shape)

    return (yp.reshape(bsz, seq, D_MODEL),
            ys,
            conv_p[None],
            pool_p[None],
            conv_s[None],
            jnp.transpose(pool_s, (1, 0, 2))[None])
```

```python
import functools

import jax
import jax.numpy as jnp
from jax import lax
from jax.experimental import pallas as pl
from jax.experimental.pallas import tpu as pltpu

D_MODEL = 1024
D_CONV = D_MODEL
D_POOL = D_MODEL // 2
N_POOL_GROUPS = 4
POOL_GROUP = D_POOL // N_POOL_GROUPS
POOL_OUT_GROUP = D_MODEL // N_POOL_GROUPS
POOL_WINDOWS = (2, 4, 8, 16)
POOL_STATE = max(POOL_WINDOWS) - 1
CONV_WIDTH = 3
D_FF = 2816
DEPTH = 1
ALPHA = (2.0 * DEPTH) ** 0.25
LN_EPS = 1e-5
D_IN = 3 * D_CONV + D_POOL + 2 * D_MODEL

OFF_C = 0
OFF_B = D_CONV
OFF_H = 2 * D_CONV
OFF_P = 3 * D_CONV
OFF_GA = 3 * D_CONV + D_POOL
OFF_GP = 3 * D_CONV + D_POOL + D_MODEL

ROW_BLOCK = 1024
SUB_ROWS = 256
FF_CHUNK = 256
STAGE_CHUNKS = 16
STAGE_SLOTS = 6
POOL_HALO = 16
CONV_HALO = 8
VMEM_LIMIT_BYTES = 52 * 1024 * 1024


def _layer_norm(v, g, b):
    mu = jnp.mean(v, axis=-1, keepdims=True)
    d = v - mu
    var = jnp.mean(d * d, axis=-1, keepdims=True)
    return d * lax.rsqrt(var + LN_EPS) * g + b


def _const_spec(shape):
    zeros = (0,) * len(shape)
    return pl.BlockSpec(shape, lambda *_: zeros, pipeline_mode=pl.Buffered(1))


def _cast_specs(arrays, n_steps, step_index):
    in_specs, out_specs, out_shapes = [], [], []
    for a in arrays:
        rows, cols = a.shape
        assert rows % (n_steps * 16) == 0
        spec = pl.BlockSpec((rows // n_steps, cols), lambda *idx: (step_index(*idx), 0))
        in_specs.append(spec)
        out_specs.append(spec)
        out_shapes.append(jax.ShapeDtypeStruct(a.shape, jnp.bfloat16))
    return in_specs, out_specs, out_shapes


def _run_casts(src_refs, dst_refs):
    for src, dst in zip(src_refs, dst_refs):
        dst[...] = src[...].astype(jnp.bfloat16)


def _ffn_rows(x, wg_ref, wu_ref, wd_ref, g_ref, b_ref, h_view):
    xb = x.astype(jnp.bfloat16)
    for c in range(D_FF // FF_CHUNK):
        sl = slice(c * FF_CHUNK, (c + 1) * FF_CHUNK)
        gate = jnp.dot(xb, wg_ref[:, sl], preferred_element_type=jnp.float32)
        up = jnp.dot(xb, wu_ref[:, sl], preferred_element_type=jnp.float32)
        h_view[:, sl] = (gate * jax.nn.sigmoid(gate) * up).astype(jnp.bfloat16)
    y = jnp.dot(h_view[...], wd_ref[...], preferred_element_type=jnp.float32)
    return _layer_norm(ALPHA * x + 0.5 * y, g_ref[...], b_ref[...])


def _stage_cast(w_hbm, w_vmem, stage, sem):
    slots, chunk = stage.shape[0], stage.shape[1]
    n = w_hbm.shape[0] // chunk
    ahead = slots - 1

    def rows(k):
        return pl.ds(pl.multiple_of(k * chunk, chunk), chunk)

    def copy(k):
        return pltpu.make_async_copy(w_hbm.at[rows(k), :], stage.at[k % slots], sem.at[k % slots])

    assert n % 2 == 0 and ahead <= n
    for k in range(ahead):
        copy(k).start(priority=k % 2)

    def body(pair, carry):
        for half in range(2):
            k = 2 * pair + half

            @pl.when(k + ahead < n)
            def _():
                copy(k + ahead).start(priority=(half + ahead) % 2)

            copy(k).wait()
            w_vmem[rows(k), :] = stage[k % slots].astype(jnp.bfloat16)
        return carry

    lax.fori_loop(0, n // 2, body, 0)


def _ffn_kernel(xp_ref, xs_ref, wg_ref, wu_ref, wd_ref, g_ref, b_ref, *rest, sub, n_cast,
                stage_weights):
    cast_src = rest[:n_cast]
    op_ref, os_ref, *cast_dst = rest[n_cast:2 * n_cast + 2]
    h_ref, *stage_scratch = rest[2 * n_cast + 2:]
    i = pl.program_id(0)
    last = pl.num_programs(0) - 1

    if stage_weights:
        wg_v, wu_v, wd_v, stage_up, stage_down, sem = stage_scratch

        @pl.when(i == 0)
        def _():
            _stage_cast(wg_ref, wg_v, stage_up, sem)
            _stage_cast(wu_ref, wu_v, stage_up, sem)
            _stage_cast(wd_ref, wd_v, stage_down, sem)

        wg_ref, wu_ref, wd_ref = wg_v, wu_v, wd_v
    weights = (wg_ref, wu_ref, wd_ref, g_ref, b_ref)

    @pl.when(i < last)
    def _():
        for s in range(xp_ref.shape[0] // sub):
            rows = pl.ds(s * sub, sub)
            h_view = h_ref.at[pl.ds((s % 2) * sub, sub), :]
            op_ref[rows, :] = _ffn_rows(xp_ref[rows, :], *weights, h_view)
        _run_casts(cast_src, cast_dst)

    @pl.when(i == last)
    def _():
        ns = xs_ref.shape[0]
        xs = xs_ref[...] if len(xs_ref.shape) == 2 else xs_ref[:, 0, :]
        ys = _ffn_rows(xs, *weights, h_ref.at[pl.ds(0, ns), :])
        os_ref[...] = ys if len(os_ref.shape) == 2 else ys[:, None, :]


def _ffn(xp, xs, wg, wu, wd, g, b, tm, cast=(), sample_out_shape=None):
    n, ns = xp.shape[0], xs.shape[0]
    nt = n // tm
    assert ns <= tm
    sample_out_shape = sample_out_shape or (ns, D_MODEL)
    sample_zeros = (0,) * len(sample_out_shape)
    blk = lambda i: (jnp.minimum(i, nt - 1), 0)
    cast_in, cast_out, cast_shapes = _cast_specs(cast, nt, lambda i: jnp.minimum(i, nt - 1))
    stage_weights = wg.dtype == jnp.float32
    assert wu.dtype == wg.dtype and wd.dtype == wg.dtype
    sub = min(tm, SUB_ROWS)
    scratch = [pltpu.VMEM((min(tm, 2 * sub), D_FF), jnp.bfloat16)]
    if stage_weights:
        weight_specs = [pl.BlockSpec(memory_space=pl.ANY)] * 3
        scratch += [
            pltpu.VMEM((D_MODEL, D_FF), jnp.bfloat16),
            pltpu.VMEM((D_MODEL, D_FF), jnp.bfloat16),
            pltpu.VMEM((D_FF, D_MODEL), jnp.bfloat16),
            pltpu.VMEM((STAGE_SLOTS, D_MODEL // STAGE_CHUNKS, D_FF), jnp.float32),
            pltpu.VMEM((STAGE_SLOTS, D_FF // STAGE_CHUNKS, D_MODEL), jnp.float32),
            pltpu.SemaphoreType.DMA((STAGE_SLOTS,)),
        ]
    else:
        weight_specs = [_const_spec((D_MODEL, D_FF)), _const_spec((D_MODEL, D_FF)),
                        _const_spec((D_FF, D_MODEL))]
    outs = pl.pallas_call(
        functools.partial(_ffn_kernel, sub=sub, n_cast=len(cast),
                          stage_weights=stage_weights),
        out_shape=[jax.ShapeDtypeStruct((n, D_MODEL), jnp.float32),
                   jax.ShapeDtypeStruct(sample_out_shape, jnp.float32)] + cast_shapes,
        grid=(nt + 1,),
        in_specs=[pl.BlockSpec((tm, D_MODEL), blk), _const_spec(xs.shape)] + weight_specs
        + [_const_spec((1, D_MODEL)), _const_spec((1, D_MODEL))] + cast_in,
        out_specs=[pl.BlockSpec((tm, D_MODEL), blk),
                   pl.BlockSpec(sample_out_shape, lambda i: sample_zeros)] + cast_out,
        scratch_shapes=scratch,
        compiler_params=pltpu.CompilerParams(
            dimension_semantics=("arbitrary",), vmem_limit_bytes=VMEM_LIMIT_BYTES),
        name="ffn",
    )(xp, xs, wg, wu, wd, g, b, *cast)
    return outs[0], outs[1], outs[2:]


def _proj(xb, w_in_ref, off, width):
    return jnp.dot(xb, w_in_ref[:, off:off + width], preferred_element_type=jnp.float32)


def _shift_rows(ext, k, halo):
    return pltpu.roll(ext, k, 0)[halo:]


def _mixer_prompt_kernel(x_ref, w_in_ref, conv_w_ref, pool_w_ref, pool_scale_ref, w_out_ref,
                         g_ref, b_ref, *rest, sub, n_cast):
    cast_src = rest[:n_cast]
    o_ref, conv_out_ref, pool_out_ref, *cast_dst = rest[n_cast:2 * n_cast + 3]
    u_carry, p_carry, m_buf = rest[2 * n_cast + 3:]
    t = pl.program_id(1)
    tm = x_ref.shape[1]

    @pl.when(t == 0)
    def _():
        u_carry[...] = jnp.zeros(u_carry.shape, jnp.float32)
        p_carry[...] = jnp.zeros(p_carry.shape, jnp.float32)

    u_prev = u_carry[...]
    p_prev = p_carry[...]

    def project_out_and_norm(rows):
        out = jnp.dot(m_buf[rows, :], w_out_ref[...], preferred_element_type=jnp.float32)
        o_ref[0, rows, :] = _layer_norm(ALPHA * x_ref[0, rows, :] + out, g_ref[...], b_ref[...])

    pending_rows = None
    for s in range(tm // sub):
        rows = slice(s * sub, (s + 1) * sub)
        x = x_ref[0, rows, :]
        xb = x.astype(jnp.bfloat16)

        p = _proj(xb, w_in_ref, OFF_P, D_POOL)
        p_ext = jnp.concatenate([p_prev, p], axis=0)
        pos = t * tm + s * sub + lax.broadcasted_iota(jnp.int32, (sub, POOL_GROUP), 0)

        u_tails = []
        for j in range(N_POOL_GROUPS):
            csl = slice(j * POOL_OUT_GROUP, (j + 1) * POOL_OUT_GROUP)
            c_g = _proj(xb, w_in_ref, OFF_C + j * POOL_OUT_GROUP, POOL_OUT_GROUP)
            h = _proj(xb, w_in_ref, OFF_H + j * POOL_OUT_GROUP, POOL_OUT_GROUP)
            u = c_g * h
            u_ext = jnp.concatenate([u_prev[:, csl], u], axis=0)
            u_tails.append(u[sub - CONV_HALO:, :])
            cw = conv_w_ref[:, csl]
            conv = (cw[0:1] * _shift_rows(u_ext, 2, CONV_HALO)
                    + cw[1:2] * _shift_rows(u_ext, 1, CONV_HALO)
                    + cw[2:3] * u)
            b_g = _proj(xb, w_in_ref, OFF_B + j * POOL_OUT_GROUP, POOL_OUT_GROUP)
            y_a = b_g * conv

            psl = slice(j * POOL_GROUP, (j + 1) * POOL_GROUP)
            w = POOL_WINDOWS[j]
            acc = p_ext[:, psl]
            span = 1
            while span < w:
                acc = acc + pltpu.roll(acc, span, 0)
                span *= 2
            inv_cnt = 1.0 / jnp.minimum(pos + 1, w).astype(jnp.float32)
            q = acc[POOL_HALO:] * inv_cnt - p[:, psl]
            y_p = jnp.dot(q.astype(jnp.bfloat16), pool_w_ref[j],
                          preferred_element_type=jnp.float32) * pool_scale_ref[:, csl]

            g_a = _proj(xb, w_in_ref, OFF_GA + j * POOL_OUT_GROUP, POOL_OUT_GROUP)
            g_p = _proj(xb, w_in_ref, OFF_GP + j * POOL_OUT_GROUP, POOL_OUT_GROUP)
            m = jax.nn.sigmoid(g_a) * y_a + jax.nn.sigmoid(g_p) * y_p
            m_buf[rows, csl] = m.astype(jnp.bfloat16)

            if j == 0 and pending_rows is not None:
                project_out_and_norm(pending_rows)

        pending_rows = rows
        u_prev = jnp.concatenate(u_tails, axis=1)
        p_prev = p_ext[sub:, :]

    project_out_and_norm(pending_rows)
    _run_casts(cast_src, cast_dst)

    u_carry[...] = u_prev
    p_carry[...] = p_prev
    conv_out_ref[0] = u_carry[CONV_HALO - (CONV_WIDTH - 1):CONV_HALO, :]
    pool_out_ref[0] = p_carry[POOL_HALO - POOL_STATE:POOL_HALO, :]


def _mixer_prompt(x, w_in, conv_w, pool_w, pool_scale, w_out, g, b, tm, cast=()):
    bsz, seq, _ = x.shape
    tiles = seq // tm
    cast_in, cast_out, cast_shapes = _cast_specs(cast, bsz * tiles, lambda s, t: s * tiles + t)
    kern = functools.partial(_mixer_prompt_kernel, sub=min(tm, SUB_ROWS), n_cast=len(cast))
    outs = pl.pallas_call(
        kern,
        out_shape=[
            jax.ShapeDtypeStruct((bsz, seq, D_MODEL), jnp.float32),
            jax.ShapeDtypeStruct((bsz, CONV_WIDTH - 1, D_CONV), jnp.float32),
            jax.ShapeDtypeStruct((bsz, POOL_STATE, D_POOL), jnp.float32),
        ] + cast_shapes,
        grid=(bsz, tiles),
        in_specs=[
            pl.BlockSpec((1, tm, D_MODEL), lambda s, t: (s, t, 0)),
            _const_spec((D_MODEL, D_IN)),
            _const_spec((CONV_WIDTH, D_CONV)),
            _const_spec((N_POOL_GROUPS, POOL_GROUP, POOL_OUT_GROUP)),
            _const_spec((1, D_MODEL)),
            _const_spec((D_MODEL, D_MODEL)),
            _const_spec((1, D_MODEL)),
            _const_spec((1, D_MODEL)),
        ] + cast_in,
        out_specs=[
            pl.BlockSpec((1, tm, D_MODEL), lambda s, t: (s, t, 0)),
            pl.BlockSpec((1, CONV_WIDTH - 1, D_CONV), lambda s, t: (s, 0, 0)),
            pl.BlockSpec((1, POOL_STATE, D_POOL), lambda s, t: (s, 0, 0)),
        ] + cast_out,
        scratch_shapes=[
            pltpu.VMEM((CONV_HALO, D_CONV), jnp.float32),
            pltpu.VMEM((POOL_HALO, D_POOL), jnp.float32),
            pltpu.VMEM((tm, D_MODEL), jnp.bfloat16),
        ],
        compiler_params=pltpu.CompilerParams(
            dimension_semantics=("arbitrary", "arbitrary"), vmem_limit_bytes=VMEM_LIMIT_BYTES),
        name="mixer_prompt",
    )(x, w_in, conv_w, pool_w, pool_scale, w_out, g, b, *cast)
    return outs[0], outs[1], outs[2], outs[3:]


def _mixer_sample_kernel(x_ref, cs_ref, ps_ref, wc_ref, wb_ref, wh_ref, wp_ref, wga_ref, wgp_ref,
                         conv_w_ref, pool_w_ref, pool_scale_ref, w_out_ref, g_ref, b_ref,
                         o_ref, conv_out_ref, pool_out_ref, acc_ref):
    j = pl.program_id(0)
    x = x_ref[...]
    xb = x.astype(jnp.bfloat16)

    def proj(w_ref):
        return jnp.dot(xb, w_ref[...], preferred_element_type=jnp.float32)

    u = proj(wc_ref) * proj(wh_ref)
    conv_out_ref[:, CONV_WIDTH - 2:CONV_WIDTH - 1, :] = u[:, None, :]
    cw = conv_w_ref[...]
    conv = cw[0:1] * cs_ref[:, 0, :] + cw[1:2] * cs_ref[:, 1, :] + cw[2:3] * u
    y_a = proj(wb_ref) * conv

    p = proj(wp_ref)
    wsum = p
    inv_w = jnp.float32(1.0)
    hi = POOL_STATE
    for gi, w in enumerate(POOL_WINDOWS):
        lo = POOL_STATE - (w - 1)
        seg = ps_ref[lo]
        for k in range(lo + 1, hi):
            seg = seg + ps_ref[k]
        wsum = wsum + jnp.where(j >= gi, seg, 0.0)
        inv_w = jnp.where(j == gi, jnp.float32(1.0 / w), inv_w)
        hi = lo
    q = wsum * inv_w - p
    y_p = jnp.dot(q.astype(jnp.bfloat16), pool_w_ref[0],
                  preferred_element_type=jnp.float32) * pool_scale_ref[...]

    m = jax.nn.sigmoid(proj(wga_ref)) * y_a + jax.nn.sigmoid(proj(wgp_ref)) * y_p
    part_out = jnp.dot(m.astype(jnp.bfloat16), w_out_ref[...], preferred_element_type=jnp.float32)

    @pl.when(j == 0)
    def _():
        acc_ref[...] = part_out

    @pl.when(j > 0)
    def _():
        acc_ref[...] += part_out

    @pl.when(j == pl.num_programs(0) - 1)
    def _():
        o_ref[...] = _layer_norm(ALPHA * x + acc_ref[...], g_ref[...], b_ref[...])

    conv_out_ref[:, 0:CONV_WIDTH - 2, :] = cs_ref[:, 1:CONV_WIDTH - 1, :]
    pool_out_ref[0:POOL_STATE - 1] = ps_ref[1:POOL_STATE]
    pool_out_ref[POOL_STATE - 1] = p


def _mixer_sample(x2d, cs, ps, w_in, conv_w, pool_w, pool_scale, w_out, g, b):
    nb = x2d.shape[0]
    cg, pg = POOL_OUT_GROUP, POOL_GROUP

    def w_in_block(off, width):
        return pl.BlockSpec((D_MODEL, width), lambda j: (0, off // width + j))

    return pl.pallas_call(
        _mixer_sample_kernel,
        out_shape=(
            jax.ShapeDtypeStruct((nb, D_MODEL), jnp.float32),
            jax.ShapeDtypeStruct(cs.shape, jnp.float32),
            jax.ShapeDtypeStruct(ps.shape, jnp.float32),
        ),
        grid=(N_POOL_GROUPS,),
        in_specs=[
            _const_spec(x2d.shape),
            pl.BlockSpec((nb, CONV_WIDTH - 1, cg), lambda j: (0, 0, j)),
            pl.BlockSpec((POOL_STATE, nb, pg), lambda j: (0, 0, j)),
            w_in_block(OFF_C, cg), w_in_block(OFF_B, cg), w_in_block(OFF_H, cg),
            w_in_block(OFF_P, pg), w_in_block(OFF_GA, cg), w_in_block(OFF_GP, cg),
            pl.BlockSpec((CONV_WIDTH, cg), lambda j: (0, j)),
            pl.BlockSpec((1, pg, cg), lambda j: (j, 0, 0)),
            pl.BlockSpec((1, cg), lambda j: (0, j)),
            pl.BlockSpec((cg, D_MODEL), lambda j: (j, 0)),
            _const_spec(g.shape),
            _const_spec(b.shape),
        ],
        out_specs=(
            pl.BlockSpec((nb, D_MODEL), lambda j: (0, 0)),
            pl.BlockSpec((nb, CONV_WIDTH - 1, cg), lambda j: (0, 0, j)),
            pl.BlockSpec((POOL_STATE, nb, pg), lambda j: (0, 0, j)),
        ),
        scratch_shapes=[pltpu.VMEM((nb, D_MODEL), jnp.float32)],
        compiler_params=pltpu.CompilerParams(
            dimension_semantics=("arbitrary",), vmem_limit_bytes=VMEM_LIMIT_BYTES),
        name="mixer_sample",
    )(x2d, cs, ps, w_in, w_in, w_in, w_in, w_in, w_in, conv_w, pool_w, pool_scale, w_out, g, b)


def kernel(x_prompt, x_sample, state_conv, state_pool, ln1_g, ln1_b, ffn1_wg, ffn1_wu, ffn1_wd,
           w_in, conv_w, pool_w, pool_scale, w_out, ln2_g, ln2_b,
           ffn2_wg, ffn2_wu, ffn2_wd, ln3_g, ln3_b):
    bsz, seq, _ = x_prompt.shape
    nb = x_sample.shape[0]
    assert x_sample.shape[1] == 1 and DEPTH == 1
    assert seq % ROW_BLOCK == 0

    yp = x_prompt.reshape(bsz * seq, D_MODEL)
    l = 0
    wg1, wu1, wd1 = ffn1_wg[l], ffn1_wu[l], ffn1_wd[l]
    g1, b1 = ln1_g[l][None], ln1_b[l][None]
    g2, b2 = ln2_g[l][None], ln2_b[l][None]
    g3, b3 = ln3_g[l][None], ln3_b[l][None]
    scale = pool_scale[l][None]

    pool_w2d = pool_w[l].reshape(N_POOL_GROUPS * POOL_GROUP, POOL_OUT_GROUP)
    yp, ys, (w_in_b, w_out_b, pool_w_b) = _ffn(yp, x_sample.reshape(nb, D_MODEL), wg1, wu1, wd1,
                                               g1, b1, ROW_BLOCK, cast=(w_in[l], w_out[l], pool_w2d))
    pool_w_b = pool_w_b.reshape(N_POOL_GROUPS, POOL_GROUP, POOL_OUT_GROUP)
    yp, conv_p, pool_p, (wg2, wu2, wd2) = _mixer_prompt(
        yp.reshape(bsz, seq, D_MODEL), w_in_b, conv_w[l], pool_w_b, scale, w_out_b, g2, b2,
        ROW_BLOCK, cast=(ffn2_wg[l], ffn2_wu[l], ffn2_wd[l]))
    ps_rows = jnp.transpose(state_pool[l], (1, 0, 2))
    ys, conv_s, pool_s = _mixer_sample(ys, state_conv[l], ps_rows, w_in_b, conv_w[l],
                                       pool_w_b, scale, w_out_b, g2, b2)
    yp, ys, _ = _ffn(yp.reshape(bsz * seq, D_MODEL), ys, wg2, wu2, wd2, g3, b3, ROW_BLOCK,
                     sample_out_shape=x_sample.shape)

    return (yp.reshape(bsz, seq, D_MODEL),
            ys,
            conv_p[None],
            pool_p[None],
            conv_s[None],
            jnp.transpose(pool_s, (1, 0, 2))[None])
```

```python
import functools

import jax
import jax.numpy as jnp
from jax import lax
from jax.experimental import pallas as pl
from jax.experimental.pallas import tpu as pltpu

D_MODEL = 1024
D_CONV = D_MODEL
D_POOL = D_MODEL // 2
N_POOL_GROUPS = 4
POOL_GROUP = D_POOL // N_POOL_GROUPS
POOL_OUT_GROUP = D_MODEL // N_POOL_GROUPS
POOL_WINDOWS = (2, 4, 8, 16)
POOL_STATE = max(POOL_WINDOWS) - 1
CONV_WIDTH = 3
D_FF = 2816
DEPTH = 1
ALPHA = (2.0 * DEPTH) ** 0.25
LN_EPS = 1e-5
D_IN = 3 * D_CONV + D_POOL + 2 * D_MODEL

OFF_C = 0
OFF_B = D_CONV
OFF_H = 2 * D_CONV
OFF_P = 3 * D_CONV
OFF_GA = 3 * D_CONV + D_POOL
OFF_GP = 3 * D_CONV + D_POOL + D_MODEL

ROW_BLOCK = 1024
SUB_ROWS = 256
FF_CHUNK = 256
NORM_AFTER_CHUNKS = (1, 3, 5, 7)
STAGE_CHUNKS = 16
STAGE_SLOTS = 6
POOL_HALO = 16
CONV_HALO = 8
VMEM_LIMIT_BYTES = 52 * 1024 * 1024


def _layer_norm(v, g, b, paced_zero=None):
    mu = jnp.mean(v, axis=-1, keepdims=True)
    if paced_zero is not None:
        mu = mu + paced_zero
    d = v - mu
    var = jnp.mean(d * d, axis=-1, keepdims=True)
    return d * lax.rsqrt(var + LN_EPS) * g + b


def _const_spec(shape):
    zeros = (0,) * len(shape)
    return pl.BlockSpec(shape, lambda *_: zeros, pipeline_mode=pl.Buffered(1))


def _cast_specs(arrays, n_steps, step_index):
    in_specs, out_specs, out_shapes = [], [], []
    for a in arrays:
        rows, cols = a.shape
        assert rows % (n_steps * 16) == 0
        spec = pl.BlockSpec((rows // n_steps, cols), lambda *idx: (step_index(*idx), 0))
        in_specs.append(spec)
        out_specs.append(spec)
        out_shapes.append(jax.ShapeDtypeStruct(a.shape, jnp.bfloat16))
    return in_specs, out_specs, out_shapes


def _run_casts(src_refs, dst_refs):
    for src, dst in zip(src_refs, dst_refs):
        dst[...] = src[...].astype(jnp.bfloat16)


def _paced_zero(v):
    bits = pltpu.bitcast(v[0:8, 0:128], jnp.uint32)
    return jnp.sum(pltpu.bitcast((bits >> 16) >> 16, jnp.float32))


def _ffn_up(x, wg_ref, wu_ref, h_view, after_chunk=None):
    xb = x.astype(jnp.bfloat16)
    for c in range(D_FF // FF_CHUNK):
        sl = slice(c * FF_CHUNK, (c + 1) * FF_CHUNK)
        gate = jnp.dot(xb, wg_ref[:, sl], preferred_element_type=jnp.float32)
        up = jnp.dot(xb, wu_ref[:, sl], preferred_element_type=jnp.float32)
        h_view[:, sl] = (gate * jax.nn.sigmoid(gate) * up).astype(jnp.bfloat16)
        if after_chunk is not None:
            after_chunk(c, gate)


def _ffn_down(x, wd_ref, h_view):
    y = jnp.dot(h_view[...], wd_ref[...], preferred_element_type=jnp.float32)
    return ALPHA * x + 0.5 * y


def _ffn_rows(x, wg_ref, wu_ref, wd_ref, g_ref, b_ref, h_view):
    _ffn_up(x, wg_ref, wu_ref, h_view)
    return _layer_norm(_ffn_down(x, wd_ref, h_view), g_ref[...], b_ref[...])


def _stage_cast(w_hbm, w_vmem, stage, sem):
    slots, chunk = stage.shape[0], stage.shape[1]
    n = w_hbm.shape[0] // chunk
    ahead = slots - 1

    def rows(k):
        return pl.ds(pl.multiple_of(k * chunk, chunk), chunk)

    def copy(k):
        return pltpu.make_async_copy(w_hbm.at[rows(k), :], stage.at[k % slots], sem.at[k % slots])

    for k in range(min(ahead, n)):
        copy(k).start()

    def body(k, carry):
        @pl.when(k + ahead < n)
        def _():
            copy(k + ahead).start()

        copy(k).wait()
        w_vmem[rows(k), :] = stage[k % slots].astype(jnp.bfloat16)
        return carry

    lax.fori_loop(0, n, body, 0)


def _ffn_kernel(xp_ref, xs_ref, wg_ref, wu_ref, wd_ref, g_ref, b_ref, *rest, sub, n_cast,
                stage_weights):
    cast_src = rest[:n_cast]
    op_ref, os_ref, *cast_dst = rest[n_cast:2 * n_cast + 2]
    h_ref, pre_ref, *stage_scratch = rest[2 * n_cast + 2:]
    i = pl.program_id(0)
    last = pl.num_programs(0) - 1

    if stage_weights:
        wg_v, wu_v, wd_v, stage_up, stage_down, sem = stage_scratch

        @pl.when(i == 0)
        def _():
            _stage_cast(wg_ref, wg_v, stage_up, sem)
            _stage_cast(wu_ref, wu_v, stage_up, sem)
            _stage_cast(wd_ref, wd_v, stage_down, sem)

        wg_ref, wu_ref, wd_ref = wg_v, wu_v, wd_v
    weights = (wg_ref, wu_ref, wd_ref, g_ref, b_ref)

    @pl.when(i < last)
    def _():
        n_sub = xp_ref.shape[0] // sub
        piece = sub // len(NORM_AFTER_CHUNKS)
        pending_rows = None
        for s in range(n_sub):
            rows = pl.ds(s * sub, sub)
            h_view = h_ref.at[pl.ds((s % 2) * sub, sub), :]
            x = xp_ref[rows, :]

            def norm_piece_after(c, gate, pending_rows=pending_rows):
                if pending_rows is None or c not in NORM_AFTER_CHUNKS:
                    return
                q = NORM_AFTER_CHUNKS.index(c)
                src = pl.ds(q * piece, piece)
                dst = pl.ds(pending_rows * sub + q * piece, piece)
                op_ref[dst, :] = _layer_norm(pre_ref[src, :], g_ref[...], b_ref[...],
                                             paced_zero=_paced_zero(gate))

            _ffn_up(x, wg_ref, wu_ref, h_view, norm_piece_after)
            pre = _ffn_down(x, wd_ref, h_view)
            if s + 1 < n_sub:
                pre_ref[...] = pre
                pending_rows = s
            else:
                op_ref[rows, :] = _layer_norm(pre, g_ref[...], b_ref[...])
        _run_casts(cast_src, cast_dst)

    @pl.when(i == last)
    def _():
        ns = xs_ref.shape[0]
        xs = xs_ref[...] if len(xs_ref.shape) == 2 else xs_ref[:, 0, :]
        ys = _ffn_rows(xs, *weights, h_ref.at[pl.ds(0, ns), :])
        os_ref[...] = ys if len(os_ref.shape) == 2 else ys[:, None, :]


def _ffn(xp, xs, wg, wu, wd, g, b, tm, cast=(), sample_out_shape=None):
    n, ns = xp.shape[0], xs.shape[0]
    nt = n // tm
    assert ns <= tm
    sample_out_shape = sample_out_shape or (ns, D_MODEL)
    sample_zeros = (0,) * len(sample_out_shape)
    blk = lambda i: (jnp.minimum(i, nt - 1), 0)
    cast_in, cast_out, cast_shapes = _cast_specs(cast, nt, lambda i: jnp.minimum(i, nt - 1))
    stage_weights = wg.dtype == jnp.float32
    assert wu.dtype == wg.dtype and wd.dtype == wg.dtype
    sub = min(tm, SUB_ROWS)
    scratch = [pltpu.VMEM((min(tm, 2 * sub), D_FF), jnp.bfloat16),
               pltpu.VMEM((sub, D_MODEL), jnp.float32)]
    if stage_weights:
        weight_specs = [pl.BlockSpec(memory_space=pl.ANY)] * 3
        scratch += [
            pltpu.VMEM((D_MODEL, D_FF), jnp.bfloat16),
            pltpu.VMEM((D_MODEL, D_FF), jnp.bfloat16),
            pltpu.VMEM((D_FF, D_MODEL), jnp.bfloat16),
            pltpu.VMEM((STAGE_SLOTS, D_MODEL // STAGE_CHUNKS, D_FF), jnp.float32),
            pltpu.VMEM((STAGE_SLOTS, D_FF // STAGE_CHUNKS, D_MODEL), jnp.float32),
            pltpu.SemaphoreType.DMA((STAGE_SLOTS,)),
        ]
    else:
        weight_specs = [_const_spec((D_MODEL, D_FF)), _const_spec((D_MODEL, D_FF)),
                        _const_spec((D_FF, D_MODEL))]
    outs = pl.pallas_call(
        functools.partial(_ffn_kernel, sub=sub, n_cast=len(cast),
                          stage_weights=stage_weights),
        out_shape=[jax.ShapeDtypeStruct((n, D_MODEL), jnp.float32),
                   jax.ShapeDtypeStruct(sample_out_shape, jnp.float32)] + cast_shapes,
        grid=(nt + 1,),
        in_specs=[pl.BlockSpec((tm, D_MODEL), blk), _const_spec(xs.shape)] + weight_specs
        + [_const_spec((1, D_MODEL)), _const_spec((1, D_MODEL))] + cast_in,
        out_specs=[pl.BlockSpec((tm, D_MODEL), blk),
                   pl.BlockSpec(sample_out_shape, lambda i: sample_zeros)] + cast_out,
        scratch_shapes=scratch,
        compiler_params=pltpu.CompilerParams(
            dimension_semantics=("arbitrary",), vmem_limit_bytes=VMEM_LIMIT_BYTES),
        name="ffn",
    )(xp, xs, wg, wu, wd, g, b, *cast)
    return outs[0], outs[1], outs[2:]


def _proj(xb, w_in_ref, off, width):
    return jnp.dot(xb, w_in_ref[:, off:off + width], preferred_element_type=jnp.float32)


def _shift_rows(ext, k, halo):
    return pltpu.roll(ext, k, 0)[halo:]


def _mixer_prompt_kernel(x_ref, w_in_ref, conv_w_ref, pool_w_ref, pool_scale_ref, w_out_ref,
                         g_ref, b_ref, *rest, sub, n_cast):
    cast_src = rest[:n_cast]
    o_ref, conv_out_ref, pool_out_ref, *cast_dst = rest[n_cast:2 * n_cast + 3]
    u_carry, p_carry, m_buf = rest[2 * n_cast + 3:]
    t = pl.program_id(1)
    tm = x_ref.shape[1]

    @pl.when(t == 0)
    def _():
        u_carry[...] = jnp.zeros(u_carry.shape, jnp.float32)
        p_carry[...] = jnp.zeros(p_carry.shape, jnp.float32)

    u_prev = u_carry[...]
    p_prev = p_carry[...]

    def project_out_and_norm(rows):
        out = jnp.dot(m_buf[rows, :], w_out_ref[...], preferred_element_type=jnp.float32)
        o_ref[0, rows, :] = _layer_norm(ALPHA * x_ref[0, rows, :] + out, g_ref[...], b_ref[...])

    pending_rows = None
    for s in range(tm // sub):
        rows = slice(s * sub, (s + 1) * sub)
        x = x_ref[0, rows, :]
        xb = x.astype(jnp.bfloat16)

        p = _proj(xb, w_in_ref, OFF_P, D_POOL)
        p_ext = jnp.concatenate([p_prev, p], axis=0)
        pos = t * tm + s * sub + lax.broadcasted_iota(jnp.int32, (sub, POOL_GROUP), 0)

        u_tails = []
        for j in range(N_POOL_GROUPS):
            csl = slice(j * POOL_OUT_GROUP, (j + 1) * POOL_OUT_GROUP)
            c_g = _proj(xb, w_in_ref, OFF_C + j * POOL_OUT_GROUP, POOL_OUT_GROUP)
            h = _proj(xb, w_in_ref, OFF_H + j * POOL_OUT_GROUP, POOL_OUT_GROUP)
            u = c_g * h
            u_ext = jnp.concatenate([u_prev[:, csl], u], axis=0)
            u_tails.append(u[sub - CONV_HALO:, :])
            cw = conv_w_ref[:, csl]
            conv = (cw[0:1] * _shift_rows(u_ext, 2, CONV_HALO)
                    + cw[1:2] * _shift_rows(u_ext, 1, CONV_HALO)
                    + cw[2:3] * u)
            b_g = _proj(xb, w_in_ref, OFF_B + j * POOL_OUT_GROUP, POOL_OUT_GROUP)
            y_a = b_g * conv

            psl = slice(j * POOL_GROUP, (j + 1) * POOL_GROUP)
            w = POOL_WINDOWS[j]
            acc = p_ext[:, psl]
            span = 1
            while span < w:
                acc = acc + pltpu.roll(acc, span, 0)
                span *= 2
            inv_cnt = 1.0 / jnp.minimum(pos + 1, w).astype(jnp.float32)
            q = acc[POOL_HALO:] * inv_cnt - p[:, psl]
            y_p = jnp.dot(q.astype(jnp.bfloat16), pool_w_ref[j],
                          preferred_element_type=jnp.float32) * pool_scale_ref[:, csl]

            g_a = _proj(xb, w_in_ref, OFF_GA + j * POOL_OUT_GROUP, POOL_OUT_GROUP)
            g_p = _proj(xb, w_in_ref, OFF_GP + j * POOL_OUT_GROUP, POOL_OUT_GROUP)
            m = jax.nn.sigmoid(g_a) * y_a + jax.nn.sigmoid(g_p) * y_p
            m_buf[rows, csl] = m.astype(jnp.bfloat16)

            if j == 0 and pending_rows is not None:
                project_out_and_norm(pending_rows)

        pending_rows = rows
        u_prev = jnp.concatenate(u_tails, axis=1)
        p_prev = p_ext[sub:, :]

    project_out_and_norm(pending_rows)
    _run_casts(cast_src, cast_dst)

    u_carry[...] = u_prev
    p_carry[...] = p_prev
    conv_out_ref[0] = u_carry[CONV_HALO - (CONV_WIDTH - 1):CONV_HALO, :]
    pool_out_ref[0] = p_carry[POOL_HALO - POOL_STATE:POOL_HALO, :]


def _mixer_prompt(x, w_in, conv_w, pool_w, pool_scale, w_out, g, b, tm, cast=()):
    bsz, seq, _ = x.shape
    tiles = seq // tm
    cast_in, cast_out, cast_shapes = _cast_specs(cast, bsz * tiles, lambda s, t: s * tiles + t)
    kern = functools.partial(_mixer_prompt_kernel, sub=min(tm, SUB_ROWS), n_cast=len(cast))
    outs = pl.pallas_call(
        kern,
        out_shape=[
            jax.ShapeDtypeStruct((bsz, seq, D_MODEL), jnp.float32),
            jax.ShapeDtypeStruct((bsz, CONV_WIDTH - 1, D_CONV), jnp.float32),
            jax.ShapeDtypeStruct((bsz, POOL_STATE, D_POOL), jnp.float32),
        ] + cast_shapes,
        grid=(bsz, tiles),
        in_specs=[
            pl.BlockSpec((1, tm, D_MODEL), lambda s, t: (s, t, 0)),
            _const_spec((D_MODEL, D_IN)),
            _const_spec((CONV_WIDTH, D_CONV)),
            _const_spec((N_POOL_GROUPS, POOL_GROUP, POOL_OUT_GROUP)),
            _const_spec((1, D_MODEL)),
            _const_spec((D_MODEL, D_MODEL)),
            _const_spec((1, D_MODEL)),
            _const_spec((1, D_MODEL)),
        ] + cast_in,
        out_specs=[
            pl.BlockSpec((1, tm, D_MODEL), lambda s, t: (s, t, 0)),
            pl.BlockSpec((1, CONV_WIDTH - 1, D_CONV), lambda s, t: (s, 0, 0)),
            pl.BlockSpec((1, POOL_STATE, D_POOL), lambda s, t: (s, 0, 0)),
        ] + cast_out,
        scratch_shapes=[
            pltpu.VMEM((CONV_HALO, D_CONV), jnp.float32),
            pltpu.VMEM((POOL_HALO, D_POOL), jnp.float32),
            pltpu.VMEM((tm, D_MODEL), jnp.bfloat16),
        ],
        compiler_params=pltpu.CompilerParams(
            dimension_semantics=("arbitrary", "arbitrary"), vmem_limit_bytes=VMEM_LIMIT_BYTES),
        name="mixer_prompt",
    )(x, w_in, conv_w, pool_w, pool_scale, w_out, g, b, *cast)
    return outs[0], outs[1], outs[2], outs[3:]


def _mixer_sample_kernel(x_ref, cs_ref, ps_ref, wc_ref, wb_ref, wh_ref, wp_ref, wga_ref, wgp_ref,
                         conv_w_ref, pool_w_ref, pool_scale_ref, w_out_ref, g_ref, b_ref,
                         o_ref, conv_out_ref, pool_out_ref, acc_ref):
    j = pl.program_id(0)
    x = x_ref[...]
    xb = x.astype(jnp.bfloat16)

    def proj(w_ref):
        return jnp.dot(xb, w_ref[...], preferred_element_type=jnp.float32)

    u = proj(wc_ref) * proj(wh_ref)
    conv_out_ref[:, CONV_WIDTH - 2:CONV_WIDTH - 1, :] = u[:, None, :]
    cw = conv_w_ref[...]
    conv = cw[0:1] * cs_ref[:, 0, :] + cw[1:2] * cs_ref[:, 1, :] + cw[2:3] * u
    y_a = proj(wb_ref) * conv

    p = proj(wp_ref)
    wsum = p
    inv_w = jnp.float32(1.0)
    hi = POOL_STATE
    for gi, w in enumerate(POOL_WINDOWS):
        lo = POOL_STATE - (w - 1)
        seg = ps_ref[lo]
        for k in range(lo + 1, hi):
            seg = seg + ps_ref[k]
        wsum = wsum + jnp.where(j >= gi, seg, 0.0)
        inv_w = jnp.where(j == gi, jnp.float32(1.0 / w), inv_w)
        hi = lo
    q = wsum * inv_w - p
    y_p = jnp.dot(q.astype(jnp.bfloat16), pool_w_ref[0],
                  preferred_element_type=jnp.float32) * pool_scale_ref[...]

    m = jax.nn.sigmoid(proj(wga_ref)) * y_a + jax.nn.sigmoid(proj(wgp_ref)) * y_p
    part_out = jnp.dot(m.astype(jnp.bfloat16), w_out_ref[...], preferred_element_type=jnp.float32)

    @pl.when(j == 0)
    def _():
        acc_ref[...] = part_out

    @pl.when(j > 0)
    def _():
        acc_ref[...] += part_out

    @pl.when(j == pl.num_programs(0) - 1)
    def _():
        o_ref[...] = _layer_norm(ALPHA * x + acc_ref[...], g_ref[...], b_ref[...])

    conv_out_ref[:, 0:CONV_WIDTH - 2, :] = cs_ref[:, 1:CONV_WIDTH - 1, :]
    pool_out_ref[0:POOL_STATE - 1] = ps_ref[1:POOL_STATE]
    pool_out_ref[POOL_STATE - 1] = p


def _mixer_sample(x2d, cs, ps, w_in, conv_w, pool_w, pool_scale, w_out, g, b):
    nb = x2d.shape[0]
    cg, pg = POOL_OUT_GROUP, POOL_GROUP

    def w_in_block(off, width):
        return pl.BlockSpec((D_MODEL, width), lambda j: (0, off // width + j))

    return pl.pallas_call(
        _mixer_sample_kernel,
        out_shape=(
            jax.ShapeDtypeStruct((nb, D_MODEL), jnp.float32),
            jax.ShapeDtypeStruct(cs.shape, jnp.float32),
            jax.ShapeDtypeStruct(ps.shape, jnp.float32),
        ),
        grid=(N_POOL_GROUPS,),
        in_specs=[
            _const_spec(x2d.shape),
            pl.BlockSpec((nb, CONV_WIDTH - 1, cg), lambda j: (0, 0, j)),
            pl.BlockSpec((POOL_STATE, nb, pg), lambda j: (0, 0, j)),
            w_in_block(OFF_C, cg), w_in_block(OFF_B, cg), w_in_block(OFF_H, cg),
            w_in_block(OFF_P, pg), w_in_block(OFF_GA, cg), w_in_block(OFF_GP, cg),
            pl.BlockSpec((CONV_WIDTH, cg), lambda j: (0, j)),
            pl.BlockSpec((1, pg, cg), lambda j: (j, 0, 0)),
            pl.BlockSpec((1, cg), lambda j: (0, j)),
            pl.BlockSpec((cg, D_MODEL), lambda j: (j, 0)),
            _const_spec(g.shape),
            _const_spec(b.shape),
        ],
        out_specs=(
            pl.BlockSpec((nb, D_MODEL), lambda j: (0, 0)),
            pl.BlockSpec((nb, CONV_WIDTH - 1, cg), lambda j: (0, 0, j)),
            pl.BlockSpec((POOL_STATE, nb, pg), lambda j: (0, 0, j)),
        ),
        scratch_shapes=[pltpu.VMEM((nb, D_MODEL), jnp.float32)],
        compiler_params=pltpu.CompilerParams(
            dimension_semantics=("arbitrary",), vmem_limit_bytes=VMEM_LIMIT_BYTES),
        name="mixer_sample",
    )(x2d, cs, ps, w_in, w_in, w_in, w_in, w_in, w_in, conv_w, pool_w, pool_scale, w_out, g, b)


def kernel(x_prompt, x_sample, state_conv, state_pool, ln1_g, ln1_b, ffn1_wg, ffn1_wu, ffn1_wd,
           w_in, conv_w, pool_w, pool_scale, w_out, ln2_g, ln2_b,
           ffn2_wg, ffn2_wu, ffn2_wd, ln3_g, ln3_b):
    bsz, seq, _ = x_prompt.shape
    nb = x_sample.shape[0]
    assert x_sample.shape[1] == 1 and DEPTH == 1
    assert seq % ROW_BLOCK == 0

    yp = x_prompt.reshape(bsz * seq, D_MODEL)
    l = 0
    wg1, wu1, wd1 = ffn1_wg[l], ffn1_wu[l], ffn1_wd[l]
    g1, b1 = ln1_g[l][None], ln1_b[l][None]
    g2, b2 = ln2_g[l][None], ln2_b[l][None]
    g3, b3 = ln3_g[l][None], ln3_b[l][None]
    scale = pool_scale[l][None]

    pool_w2d = pool_w[l].reshape(N_POOL_GROUPS * POOL_GROUP, POOL_OUT_GROUP)
    yp, ys, (w_in_b, w_out_b, pool_w_b) = _ffn(yp, x_sample.reshape(nb, D_MODEL), wg1, wu1, wd1,
                                               g1, b1, ROW_BLOCK, cast=(w_in[l], w_out[l], pool_w2d))
    pool_w_b = pool_w_b.reshape(N_POOL_GROUPS, POOL_GROUP, POOL_OUT_GROUP)
    yp, conv_p, pool_p, (wg2, wu2, wd2) = _mixer_prompt(
        yp.reshape(bsz, seq, D_MODEL), w_in_b, conv_w[l], pool_w_b, scale, w_out_b, g2, b2,
        ROW_BLOCK, cast=(ffn2_wg[l], ffn2_wu[l], ffn2_wd[l]))
    ps_rows = jnp.transpose(state_pool[l], (1, 0, 2))
    ys, conv_s, pool_s = _mixer_sample(ys, state_conv[l], ps_rows, w_in_b, conv_w[l],
                                       pool_w_b, scale, w_out_b, g2, b2)
    yp, ys, _ = _ffn(yp.reshape(bsz * seq, D_MODEL), ys, wg2, wu2, wd2, g3, b3, ROW_BLOCK,
                     sample_out_shape=x_sample.shape)

    return (yp.reshape(bsz, seq, D_MODEL),
            ys,
            conv_p[None],
            pool_p[None],
            conv_s[None],
            jnp.transpose(pool_s, (1, 0, 2))[None])
```

```python
import functools

import jax
import jax.numpy as jnp
from jax import lax
from jax.experimental import pallas as pl
from jax.experimental.pallas import tpu as pltpu

D_MODEL = 1024
D_CONV = D_MODEL
D_POOL = D_MODEL // 2
N_POOL_GROUPS = 4
POOL_GROUP = D_POOL // N_POOL_GROUPS
POOL_OUT_GROUP = D_MODEL // N_POOL_GROUPS
POOL_WINDOWS = (2, 4, 8, 16)
POOL_STATE = max(POOL_WINDOWS) - 1
CONV_WIDTH = 3
D_FF = 2816
DEPTH = 1
ALPHA = (2.0 * DEPTH) ** 0.25
LN_EPS = 1e-5
D_IN = 3 * D_CONV + D_POOL + 2 * D_MODEL

OFF_C = 0
OFF_B = D_CONV
OFF_H = 2 * D_CONV
OFF_P = 3 * D_CONV
OFF_GA = 3 * D_CONV + D_POOL
OFF_GP = 3 * D_CONV + D_POOL + D_MODEL

BF16_ROW_TILE = 16
ROW_BLOCK = 1024
SUB_ROWS = 256
FF_CHUNK = 256
STAGE_CHUNKS = 16
STAGE_SLOTS = 6
POOL_HALO = 16
CONV_HALO = 8
VMEM_LIMIT_BYTES = 52 * 1024 * 1024


def _layer_norm(v, g, b):
    mu = jnp.mean(v, axis=-1, keepdims=True)
    d = v - mu
    var = jnp.mean(d * d, axis=-1, keepdims=True)
    return d * lax.rsqrt(var + LN_EPS) * g + b


def _const_spec(shape):
    zeros = (0,) * len(shape)
    return pl.BlockSpec(shape, lambda *_: zeros, pipeline_mode=pl.Buffered(1))


def _cast_specs(arrays, n_steps, step_index):
    in_specs, out_specs, out_shapes = [], [], []
    for a in arrays:
        rows, cols = a.shape
        assert rows % (n_steps * BF16_ROW_TILE) == 0
        spec = pl.BlockSpec((rows // n_steps, cols), lambda *idx: (step_index(*idx), 0))
        in_specs.append(spec)
        out_specs.append(spec)
        out_shapes.append(jax.ShapeDtypeStruct(a.shape, jnp.bfloat16))
    return in_specs, out_specs, out_shapes


def _run_casts(src_refs, dst_refs):
    for src, dst in zip(src_refs, dst_refs):
        dst[...] = src[...].astype(jnp.bfloat16)


def _ffn_rows(x, wg_ref, wu_ref, wd_ref, g_ref, b_ref, h_view):
    xb = x.astype(jnp.bfloat16)
    for c in range(D_FF // FF_CHUNK):
        sl = slice(c * FF_CHUNK, (c + 1) * FF_CHUNK)
        gate = jnp.dot(xb, wg_ref[:, sl], preferred_element_type=jnp.float32)
        up = jnp.dot(xb, wu_ref[:, sl], preferred_element_type=jnp.float32)
        h_view[:, sl] = (gate * jax.nn.sigmoid(gate) * up).astype(jnp.bfloat16)
    y = jnp.dot(h_view[...], wd_ref[...], preferred_element_type=jnp.float32)
    return _layer_norm(ALPHA * x + 0.5 * y, g_ref[...], b_ref[...])


def _stage_cast(w_hbm, w_vmem, stage, sem):
    slots, chunk = stage.shape[0], stage.shape[1]
    n = w_hbm.shape[0] // chunk
    ahead = slots - 1

    def rows(k):
        return pl.ds(pl.multiple_of(k * chunk, chunk), chunk)

    def copy(k):
        return pltpu.make_async_copy(w_hbm.at[rows(k), :], stage.at[k % slots], sem.at[k % slots])

    for k in range(min(ahead, n)):
        copy(k).start()

    def body(k, carry):
        @pl.when(k + ahead < n)
        def _():
            copy(k + ahead).start()

        copy(k).wait()
        w_vmem[rows(k), :] = stage[k % slots].astype(jnp.bfloat16)
        return carry

    lax.fori_loop(0, n, body, 0)


def _ffn_kernel(xp_ref, xs_ref, wg_ref, wu_ref, wd_ref, g_ref, b_ref, *rest, sub, n_cast,
                stage_weights):
    cast_src = rest[:n_cast]
    op_ref, os_ref, *cast_dst = rest[n_cast:2 * n_cast + 2]
    h_ref, *stage_scratch = rest[2 * n_cast + 2:]
    i = pl.program_id(0)
    last = pl.num_programs(0) - 1

    if stage_weights:
        wg_v, wu_v, wd_v, stage_up, stage_down, sem = stage_scratch

        @pl.when(i == 0)
        def _():
            _stage_cast(wg_ref, wg_v, stage_up, sem)
            _stage_cast(wu_ref, wu_v, stage_up, sem)
            _stage_cast(wd_ref, wd_v, stage_down, sem)

        wg_ref, wu_ref, wd_ref = wg_v, wu_v, wd_v
    weights = (wg_ref, wu_ref, wd_ref, g_ref, b_ref)

    @pl.when(i < last)
    def _():
        for s in range(xp_ref.shape[0] // sub):
            rows = pl.ds(s * sub, sub)
            h_view = h_ref.at[pl.ds((s % 2) * sub, sub), :]
            op_ref[rows, :] = _ffn_rows(xp_ref[rows, :], *weights, h_view)
        _run_casts(cast_src, cast_dst)

    @pl.when(i == last)
    def _():
        ns = xs_ref.shape[0]
        xs = xs_ref[...] if len(xs_ref.shape) == 2 else xs_ref[:, 0, :]
        ys = _ffn_rows(xs, *weights, h_ref.at[pl.ds(0, ns), :])
        os_ref[...] = ys if len(os_ref.shape) == 2 else ys[:, None, :]


def _ffn(xp, xs, wg, wu, wd, g, b, tm, cast=(), sample_out_shape=None):
    n, ns = xp.shape[0], xs.shape[0]
    nt = n // tm
    assert ns <= tm
    sample_out_shape = sample_out_shape or (ns, D_MODEL)
    sample_zeros = (0,) * len(sample_out_shape)
    blk = lambda i: (jnp.minimum(i, nt - 1), 0)
    cast_in, cast_out, cast_shapes = _cast_specs(cast, nt, lambda i: jnp.minimum(i, nt - 1))
    stage_weights = wg.dtype == jnp.float32
    assert wu.dtype == wg.dtype and wd.dtype == wg.dtype
    sub = min(tm, SUB_ROWS)
    scratch = [pltpu.VMEM((min(tm, 2 * sub), D_FF), jnp.bfloat16)]
    if stage_weights:
        weight_specs = [pl.BlockSpec(memory_space=pl.ANY)] * 3
        scratch += [
            pltpu.VMEM((D_MODEL, D_FF), jnp.bfloat16),
            pltpu.VMEM((D_MODEL, D_FF), jnp.bfloat16),
            pltpu.VMEM((D_FF, D_MODEL), jnp.bfloat16),
            pltpu.VMEM((STAGE_SLOTS, D_MODEL // STAGE_CHUNKS, D_FF), jnp.float32),
            pltpu.VMEM((STAGE_SLOTS, D_FF // STAGE_CHUNKS, D_MODEL), jnp.float32),
            pltpu.SemaphoreType.DMA((STAGE_SLOTS,)),
        ]
    else:
        weight_specs = [_const_spec((D_MODEL, D_FF)), _const_spec((D_MODEL, D_FF)),
                        _const_spec((D_FF, D_MODEL))]
    outs = pl.pallas_call(
        functools.partial(_ffn_kernel, sub=sub, n_cast=len(cast),
                          stage_weights=stage_weights),
        out_shape=[jax.ShapeDtypeStruct((n, D_MODEL), jnp.float32),
                   jax.ShapeDtypeStruct(sample_out_shape, jnp.float32)] + cast_shapes,
        grid=(nt + 1,),
        in_specs=[pl.BlockSpec((tm, D_MODEL), blk), _const_spec(xs.shape)] + weight_specs
        + [_const_spec((1, D_MODEL)), _const_spec((1, D_MODEL))] + cast_in,
        out_specs=[pl.BlockSpec((tm, D_MODEL), blk),
                   pl.BlockSpec(sample_out_shape, lambda i: sample_zeros)] + cast_out,
        scratch_shapes=scratch,
        compiler_params=pltpu.CompilerParams(
            dimension_semantics=("arbitrary",), vmem_limit_bytes=VMEM_LIMIT_BYTES),
        name="ffn",
    )(xp, xs, wg, wu, wd, g, b, *cast)
    return outs[0], outs[1], outs[2:]


def _proj(xb, w_in_ref, off, width):
    return jnp.dot(xb, w_in_ref[:, off:off + width], preferred_element_type=jnp.float32)


def _shift_rows(ext, k, halo):
    return pltpu.roll(ext, k, 0)[halo:]


def _mixer_prompt_kernel(x_ref, w_in_ref, conv_w_ref, pool_w_ref, pool_scale_ref, w_out_ref,
                         g_ref, b_ref, *rest, sub, n_cast):
    cast_src = rest[:n_cast]
    o_ref, conv_out_ref, pool_out_ref, *cast_dst = rest[n_cast:2 * n_cast + 3]
    u_carry, p_carry, m_buf = rest[2 * n_cast + 3:]
    t = pl.program_id(1)
    tm = x_ref.shape[1]

    @pl.when(t == 0)
    def _():
        u_carry[...] = jnp.zeros(u_carry.shape, jnp.float32)
        p_carry[...] = jnp.zeros(p_carry.shape, jnp.float32)

    u_prev = u_carry[...]
    p_prev = p_carry[...]

    def project_out_and_norm(rows):
        out = jnp.dot(m_buf[rows, :], w_out_ref[...], preferred_element_type=jnp.float32)
        o_ref[0, rows, :] = _layer_norm(ALPHA * x_ref[0, rows, :] + out, g_ref[...], b_ref[...])

    pending_rows = None
    for s in range(tm // sub):
        rows = slice(s * sub, (s + 1) * sub)
        x = x_ref[0, rows, :]
        xb = x.astype(jnp.bfloat16)

        p = _proj(xb, w_in_ref, OFF_P, D_POOL)
        p_ext = jnp.concatenate([p_prev, p], axis=0)
        pos = t * tm + s * sub + lax.broadcasted_iota(jnp.int32, (sub, POOL_GROUP), 0)

        u_tails = []
        for j in range(N_POOL_GROUPS):
            csl = slice(j * POOL_OUT_GROUP, (j + 1) * POOL_OUT_GROUP)
            c_g = _proj(xb, w_in_ref, OFF_C + j * POOL_OUT_GROUP, POOL_OUT_GROUP)
            h = _proj(xb, w_in_ref, OFF_H + j * POOL_OUT_GROUP, POOL_OUT_GROUP)
            u = c_g * h
            u_ext = jnp.concatenate([u_prev[:, csl], u], axis=0)
            u_tails.append(u[sub - CONV_HALO:, :])
            cw = conv_w_ref[:, csl]
            conv = (cw[0:1] * _shift_rows(u_ext, 2, CONV_HALO)
                    + cw[1:2] * _shift_rows(u_ext, 1, CONV_HALO)
                    + cw[2:3] * u)
            b_g = _proj(xb, w_in_ref, OFF_B + j * POOL_OUT_GROUP, POOL_OUT_GROUP)
            y_a = b_g * conv

            psl = slice(j * POOL_GROUP, (j + 1) * POOL_GROUP)
            w = POOL_WINDOWS[j]
            acc = p_ext[:, psl]
            span = 1
            while span < w:
                acc = acc + pltpu.roll(acc, span, 0)
                span *= 2
            inv_cnt = 1.0 / jnp.minimum(pos + 1, w).astype(jnp.float32)
            q = acc[POOL_HALO:] * inv_cnt - p[:, psl]
            y_p = jnp.dot(q.astype(jnp.bfloat16), pool_w_ref[j],
                          preferred_element_type=jnp.float32) * pool_scale_ref[:, csl]

            g_a = _proj(xb, w_in_ref, OFF_GA + j * POOL_OUT_GROUP, POOL_OUT_GROUP)
            g_p = _proj(xb, w_in_ref, OFF_GP + j * POOL_OUT_GROUP, POOL_OUT_GROUP)
            m = jax.nn.sigmoid(g_a) * y_a + jax.nn.sigmoid(g_p) * y_p
            m_buf[rows, csl] = m.astype(jnp.bfloat16)

            if j == 0 and pending_rows is not None:
                project_out_and_norm(pending_rows)

        pending_rows = rows
        u_prev = jnp.concatenate(u_tails, axis=1)
        p_prev = p_ext[sub:, :]

    project_out_and_norm(pending_rows)
    _run_casts(cast_src, cast_dst)

    u_carry[...] = u_prev
    p_carry[...] = p_prev
    conv_out_ref[0] = u_carry[CONV_HALO - (CONV_WIDTH - 1):CONV_HALO, :]
    pool_out_ref[0] = p_carry[POOL_HALO - POOL_STATE:POOL_HALO, :]


def _mixer_prompt(x, w_in, conv_w, pool_w, pool_scale, w_out, g, b, tm, cast=()):
    bsz, seq, _ = x.shape
    tiles = seq // tm
    cast_in, cast_out, cast_shapes = _cast_specs(cast, bsz * tiles, lambda s, t: s * tiles + t)
    kern = functools.partial(_mixer_prompt_kernel, sub=min(tm, SUB_ROWS), n_cast=len(cast))
    outs = pl.pallas_call(
        kern,
        out_shape=[
            jax.ShapeDtypeStruct((bsz, seq, D_MODEL), jnp.float32),
            jax.ShapeDtypeStruct((bsz, CONV_WIDTH - 1, D_CONV), jnp.float32),
            jax.ShapeDtypeStruct((bsz, POOL_STATE, D_POOL), jnp.float32),
        ] + cast_shapes,
        grid=(bsz, tiles),
        in_specs=[
            pl.BlockSpec((1, tm, D_MODEL), lambda s, t: (s, t, 0)),
            _const_spec((D_MODEL, D_IN)),
            _const_spec((CONV_WIDTH, D_CONV)),
            _const_spec((N_POOL_GROUPS, POOL_GROUP, POOL_OUT_GROUP)),
            _const_spec((1, D_MODEL)),
            _const_spec((D_MODEL, D_MODEL)),
            _const_spec((1, D_MODEL)),
            _const_spec((1, D_MODEL)),
        ] + cast_in,
        out_specs=[
            pl.BlockSpec((1, tm, D_MODEL), lambda s, t: (s, t, 0)),
            pl.BlockSpec((1, CONV_WIDTH - 1, D_CONV), lambda s, t: (s, 0, 0)),
            pl.BlockSpec((1, POOL_STATE, D_POOL), lambda s, t: (s, 0, 0)),
        ] + cast_out,
        scratch_shapes=[
            pltpu.VMEM((CONV_HALO, D_CONV), jnp.float32),
            pltpu.VMEM((POOL_HALO, D_POOL), jnp.float32),
            pltpu.VMEM((tm, D_MODEL), jnp.bfloat16),
        ],
        compiler_params=pltpu.CompilerParams(
            dimension_semantics=("arbitrary", "arbitrary"), vmem_limit_bytes=VMEM_LIMIT_BYTES),
        name="mixer_prompt",
    )(x, w_in, conv_w, pool_w, pool_scale, w_out, g, b, *cast)
    return outs[0], outs[1], outs[2], outs[3:]


def _mixer_sample_kernel(x_ref, cs_ref, ps_ref, wc_ref, wb_ref, wh_ref, wp_ref, wga_ref, wgp_ref,
                         conv_w_ref, pool_w_ref, pool_scale_ref, w_out_ref, g_ref, b_ref,
                         o_ref, conv_out_ref, pool_out_ref, acc_ref):
    j = pl.program_id(0)
    x = x_ref[...]
    xb = x.astype(jnp.bfloat16)

    def proj(w_ref):
        return jnp.dot(xb, w_ref[...], preferred_element_type=jnp.float32)

    u = proj(wc_ref) * proj(wh_ref)
    conv_out_ref[:, CONV_WIDTH - 2:CONV_WIDTH - 1, :] = u[:, None, :]
    cw = conv_w_ref[...]
    conv = cw[0:1] * cs_ref[:, 0, :] + cw[1:2] * cs_ref[:, 1, :] + cw[2:3] * u
    y_a = proj(wb_ref) * conv

    p = proj(wp_ref)
    wsum = p
    inv_w = jnp.float32(1.0)
    hi = POOL_STATE
    for gi, w in enumerate(POOL_WINDOWS):
        lo = POOL_STATE - (w - 1)
        seg = ps_ref[lo]
        for k in range(lo + 1, hi):
            seg = seg + ps_ref[k]
        wsum = wsum + jnp.where(j >= gi, seg, 0.0)
        inv_w = jnp.where(j == gi, jnp.float32(1.0 / w), inv_w)
        hi = lo
    q = wsum * inv_w - p
    y_p = jnp.dot(q.astype(jnp.bfloat16), pool_w_ref[0],
                  preferred_element_type=jnp.float32) * pool_scale_ref[...]

    m = jax.nn.sigmoid(proj(wga_ref)) * y_a + jax.nn.sigmoid(proj(wgp_ref)) * y_p
    part_out = jnp.dot(m.astype(jnp.bfloat16), w_out_ref[...], preferred_element_type=jnp.float32)

    @pl.when(j == 0)
    def _():
        acc_ref[...] = part_out

    @pl.when(j > 0)
    def _():
        acc_ref[...] += part_out

    @pl.when(j == pl.num_programs(0) - 1)
    def _():
        o_ref[...] = _layer_norm(ALPHA * x + acc_ref[...], g_ref[...], b_ref[...])

    conv_out_ref[:, 0:CONV_WIDTH - 2, :] = cs_ref[:, 1:CONV_WIDTH - 1, :]
    pool_out_ref[0:POOL_STATE - 1] = ps_ref[1:POOL_STATE]
    pool_out_ref[POOL_STATE - 1] = p


def _mixer_sample(x2d, cs, ps, w_in, conv_w, pool_w, pool_scale, w_out, g, b):
    nb = x2d.shape[0]
    cg, pg = POOL_OUT_GROUP, POOL_GROUP

    def w_in_block(off, width):
        return pl.BlockSpec((D_MODEL, width), lambda j: (0, off // width + j))

    return pl.pallas_call(
        _mixer_sample_kernel,
        out_shape=(
            jax.ShapeDtypeStruct((nb, D_MODEL), jnp.float32),
            jax.ShapeDtypeStruct(cs.shape, jnp.float32),
            jax.ShapeDtypeStruct(ps.shape, jnp.float32),
        ),
        grid=(N_POOL_GROUPS,),
        in_specs=[
            _const_spec(x2d.shape),
            pl.BlockSpec((nb, CONV_WIDTH - 1, cg), lambda j: (0, 0, j)),
            pl.BlockSpec((POOL_STATE, nb, pg), lambda j: (0, 0, j)),
            w_in_block(OFF_C, cg), w_in_block(OFF_B, cg), w_in_block(OFF_H, cg),
            w_in_block(OFF_P, pg), w_in_block(OFF_GA, cg), w_in_block(OFF_GP, cg),
            pl.BlockSpec((CONV_WIDTH, cg), lambda j: (0, j)),
            pl.BlockSpec((1, pg, cg), lambda j: (j, 0, 0)),
            pl.BlockSpec((1, cg), lambda j: (0, j)),
            pl.BlockSpec((cg, D_MODEL), lambda j: (j, 0)),
            _const_spec(g.shape),
            _const_spec(b.shape),
        ],
        out_specs=(
            pl.BlockSpec((nb, D_MODEL), lambda j: (0, 0)),
            pl.BlockSpec((nb, CONV_WIDTH - 1, cg), lambda j: (0, 0, j)),
            pl.BlockSpec((POOL_STATE, nb, pg), lambda j: (0, 0, j)),
        ),
        scratch_shapes=[pltpu.VMEM((nb, D_MODEL), jnp.float32)],
        compiler_params=pltpu.CompilerParams(
            dimension_semantics=("arbitrary",), vmem_limit_bytes=VMEM_LIMIT_BYTES),
        name="mixer_sample",
    )(x2d, cs, ps, w_in, w_in, w_in, w_in, w_in, w_in, conv_w, pool_w, pool_scale, w_out, g, b)


def kernel(x_prompt, x_sample, state_conv, state_pool, ln1_g, ln1_b, ffn1_wg, ffn1_wu, ffn1_wd,
           w_in, conv_w, pool_w, pool_scale, w_out, ln2_g, ln2_b,
           ffn2_wg, ffn2_wu, ffn2_wd, ln3_g, ln3_b):
    bsz, seq, _ = x_prompt.shape
    nb = x_sample.shape[0]
    assert x_sample.shape[1] == 1 and DEPTH == 1
    assert seq % ROW_BLOCK == 0

    yp = x_prompt.reshape(bsz * seq, D_MODEL)
    l = 0
    wg1, wu1, wd1 = ffn1_wg[l], ffn1_wu[l], ffn1_wd[l]
    g1, b1 = ln1_g[l][None], ln1_b[l][None]
    g2, b2 = ln2_g[l][None], ln2_b[l][None]
    g3, b3 = ln3_g[l][None], ln3_b[l][None]
    scale = pool_scale[l][None]

    pool_w2d = pool_w[l].reshape(N_POOL_GROUPS * POOL_GROUP, POOL_OUT_GROUP)
    yp, ys, (w_in_b, w_out_b, pool_w_b) = _ffn(yp, x_sample.reshape(nb, D_MODEL), wg1, wu1, wd1,
                                               g1, b1, ROW_BLOCK, cast=(w_in[l], w_out[l], pool_w2d))
    pool_w_b = pool_w_b.reshape(N_POOL_GROUPS, POOL_GROUP, POOL_OUT_GROUP)
    yp, conv_p, pool_p, (wg2, wu2, wd2) = _mixer_prompt(
        yp.reshape(bsz, seq, D_MODEL), w_in_b, conv_w[l], pool_w_b, scale, w_out_b, g2, b2,
        ROW_BLOCK, cast=(ffn2_wg[l], ffn2_wu[l], ffn2_wd[l]))
    ps_rows = jnp.transpose(state_pool[l], (1, 0, 2))
    ys, conv_s, pool_s = _mixer_sample(ys, state_conv[l], ps_rows, w_in_b, conv_w[l],
                                       pool_w_b, scale, w_out_b, g2, b2)
    yp, ys, _ = _ffn(yp.reshape(bsz * seq, D_MODEL), ys, wg2, wu2, wd2, g3, b3, ROW_BLOCK,
                     sample_out_shape=x_sample.shape)

    return (yp.reshape(bsz, seq, D_MODEL),
            ys,
            conv_p[None],
            pool_p[None],
            conv_s[None],
            jnp.transpose(pool_s, (1, 0, 2))[None])
```

```python
import functools

import jax
import jax.numpy as jnp
from jax import lax
from jax.experimental import pallas as pl
from jax.experimental.pallas import tpu as pltpu

D_MODEL = 1024
D_CONV = D_MODEL
D_POOL = D_MODEL // 2
N_POOL_GROUPS = 4
POOL_GROUP = D_POOL // N_POOL_GROUPS
POOL_OUT_GROUP = D_MODEL // N_POOL_GROUPS
POOL_WINDOWS = (2, 4, 8, 16)
POOL_STATE = max(POOL_WINDOWS) - 1
CONV_WIDTH = 3
D_FF = 2816
DEPTH = 1
ALPHA = (2.0 * DEPTH) ** 0.25
LN_EPS = 1e-5
D_IN = 3 * D_CONV + D_POOL + 2 * D_MODEL

OFF_C = 0
OFF_B = D_CONV
OFF_H = 2 * D_CONV
OFF_P = 3 * D_CONV
OFF_GA = 3 * D_CONV + D_POOL
OFF_GP = 3 * D_CONV + D_POOL + D_MODEL

SEG_C, SEG_B, SEG_H, SEG_GA, SEG_GP = (k * POOL_OUT_GROUP for k in range(5))
SEG_P = 5 * POOL_OUT_GROUP
GROUP_COLS = SEG_P + POOL_GROUP
GROUP_SEGMENTS = ((SEG_C, OFF_C, POOL_OUT_GROUP), (SEG_B, OFF_B, POOL_OUT_GROUP),
                  (SEG_H, OFF_H, POOL_OUT_GROUP), (SEG_GA, OFF_GA, POOL_OUT_GROUP),
                  (SEG_GP, OFF_GP, POOL_OUT_GROUP), (SEG_P, OFF_P, POOL_GROUP))

ROW_BLOCK = 1024
SUB_ROWS = 256
FF_CHUNK = 256
STAGE_CHUNKS = 16
STAGE_SLOTS = 6
POOL_HALO = 16
CONV_HALO = 8
VMEM_LIMIT_BYTES = 52 * 1024 * 1024


def _layer_norm(v, g, b):
    mu = jnp.mean(v, axis=-1, keepdims=True)
    d = v - mu
    var = jnp.mean(d * d, axis=-1, keepdims=True)
    return d * lax.rsqrt(var + LN_EPS) * g + b


def _const_spec(shape):
    zeros = (0,) * len(shape)
    return pl.BlockSpec(shape, lambda *_: zeros, pipeline_mode=pl.Buffered(1))


def _cast_specs(arrays, n_steps, step_index):
    in_specs, out_specs, out_shapes = [], [], []
    for a in arrays:
        rows, cols = a.shape
        assert rows % (n_steps * 16) == 0
        spec = pl.BlockSpec((rows // n_steps, cols), lambda *idx: (step_index(*idx), 0))
        in_specs.append(spec)
        out_specs.append(spec)
        out_shapes.append(jax.ShapeDtypeStruct(a.shape, jnp.bfloat16))
    return in_specs, out_specs, out_shapes


def _run_casts(src_refs, dst_refs):
    for src, dst in zip(src_refs, dst_refs):
        dst[...] = src[...].astype(jnp.bfloat16)


def _regroup_cast(src, dst, dst_p):
    for j in range(N_POOL_GROUPS):
        for seg, off, width in GROUP_SEGMENTS:
            dst[j, :, seg:seg + width] = src[:, off + j * width:off + (j + 1) * width].astype(
                jnp.bfloat16)
    dst_p[...] = src[:, OFF_P:OFF_P + D_POOL].astype(jnp.bfloat16)


def _ffn_rows(x, wg_ref, wu_ref, wd_ref, g_ref, b_ref, h_view):
    xb = x.astype(jnp.bfloat16)
    for c in range(D_FF // FF_CHUNK):
        sl = slice(c * FF_CHUNK, (c + 1) * FF_CHUNK)
        gate = jnp.dot(xb, wg_ref[:, sl], preferred_element_type=jnp.float32)
        up = jnp.dot(xb, wu_ref[:, sl], preferred_element_type=jnp.float32)
        h_view[:, sl] = (gate * jax.nn.sigmoid(gate) * up).astype(jnp.bfloat16)
    y = jnp.dot(h_view[...], wd_ref[...], preferred_element_type=jnp.float32)
    return _layer_norm(ALPHA * x + 0.5 * y, g_ref[...], b_ref[...])


def _stage_cast(w_hbm, w_vmem, stage, sem):
    slots, chunk = stage.shape[0], stage.shape[1]
    n = w_hbm.shape[0] // chunk
    ahead = slots - 1

    def rows(k):
        return pl.ds(pl.multiple_of(k * chunk, chunk), chunk)

    def copy(k):
        return pltpu.make_async_copy(w_hbm.at[rows(k), :], stage.at[k % slots], sem.at[k % slots])

    for k in range(min(ahead, n)):
        copy(k).start()

    def body(k, carry):
        @pl.when(k + ahead < n)
        def _():
            copy(k + ahead).start()

        copy(k).wait()
        w_vmem[rows(k), :] = stage[k % slots].astype(jnp.bfloat16)
        return carry

    lax.fori_loop(0, n, body, 0)


def _ffn_kernel(xp_ref, xs_ref, wg_ref, wu_ref, wd_ref, g_ref, b_ref, *rest, sub, n_cast,
                stage_weights, regroup):
    cast_src = rest[:n_cast]
    n_dst = n_cast + regroup
    op_ref, os_ref, *cast_dst = rest[n_cast:n_cast + 2 + n_dst]
    h_ref, *stage_scratch = rest[n_cast + 2 + n_dst:]
    i = pl.program_id(0)
    last = pl.num_programs(0) - 1

    if stage_weights:
        wg_v, wu_v, wd_v, stage_up, stage_down, sem = stage_scratch

        @pl.when(i == 0)
        def _():
            _stage_cast(wg_ref, wg_v, stage_up, sem)
            _stage_cast(wu_ref, wu_v, stage_up, sem)
            _stage_cast(wd_ref, wd_v, stage_down, sem)

        wg_ref, wu_ref, wd_ref = wg_v, wu_v, wd_v
    weights = (wg_ref, wu_ref, wd_ref, g_ref, b_ref)

    @pl.when(i < last)
    def _():
        for s in range(xp_ref.shape[0] // sub):
            rows = pl.ds(s * sub, sub)
            h_view = h_ref.at[pl.ds((s % 2) * sub, sub), :]
            op_ref[rows, :] = _ffn_rows(xp_ref[rows, :], *weights, h_view)
        if regroup:
            _regroup_cast(cast_src[0], cast_dst[0], cast_dst[1])
        _run_casts(cast_src[regroup:], cast_dst[2 * regroup:])

    @pl.when(i == last)
    def _():
        ns = xs_ref.shape[0]
        xs = xs_ref[...] if len(xs_ref.shape) == 2 else xs_ref[:, 0, :]
        ys = _ffn_rows(xs, *weights, h_ref.at[pl.ds(0, ns), :])
        os_ref[...] = ys if len(os_ref.shape) == 2 else ys[:, None, :]


def _ffn(xp, xs, wg, wu, wd, g, b, tm, cast=(), sample_out_shape=None, w_in=None):
    n, ns = xp.shape[0], xs.shape[0]
    nt = n // tm
    assert ns <= tm
    sample_out_shape = sample_out_shape or (ns, D_MODEL)
    sample_zeros = (0,) * len(sample_out_shape)
    blk = lambda i: (jnp.minimum(i, nt - 1), 0)
    step = lambda i: jnp.minimum(i, nt - 1)
    cast_in, cast_out, cast_shapes = _cast_specs(cast, nt, step)
    regroup = w_in is not None
    if regroup:
        chunk = D_MODEL // nt
        assert w_in.shape == (D_MODEL, D_IN) and chunk % 16 == 0
        cast = (w_in,) + tuple(cast)
        cast_in = [pl.BlockSpec((chunk, D_IN), lambda i: (step(i), 0))] + cast_in
        cast_out = [pl.BlockSpec((N_POOL_GROUPS, chunk, GROUP_COLS), lambda i: (0, step(i), 0)),
                    pl.BlockSpec((chunk, D_POOL), lambda i: (step(i), 0))] + cast_out
        cast_shapes = [jax.ShapeDtypeStruct((N_POOL_GROUPS, D_MODEL, GROUP_COLS), jnp.bfloat16),
                       jax.ShapeDtypeStruct((D_MODEL, D_POOL), jnp.bfloat16)] + cast_shapes
    stage_weights = wg.dtype == jnp.float32
    assert wu.dtype == wg.dtype and wd.dtype == wg.dtype
    sub = min(tm, SUB_ROWS)
    scratch = [pltpu.VMEM((min(tm, 2 * sub), D_FF), jnp.bfloat16)]
    if stage_weights:
        weight_specs = [pl.BlockSpec(memory_space=pl.ANY)] * 3
        scratch += [
            pltpu.VMEM((D_MODEL, D_FF), jnp.bfloat16),
            pltpu.VMEM((D_MODEL, D_FF), jnp.bfloat16),
            pltpu.VMEM((D_FF, D_MODEL), jnp.bfloat16),
            pltpu.VMEM((STAGE_SLOTS, D_MODEL // STAGE_CHUNKS, D_FF), jnp.float32),
            pltpu.VMEM((STAGE_SLOTS, D_FF // STAGE_CHUNKS, D_MODEL), jnp.float32),
            pltpu.SemaphoreType.DMA((STAGE_SLOTS,)),
        ]
    else:
        weight_specs = [_const_spec((D_MODEL, D_FF)), _const_spec((D_MODEL, D_FF)),
                        _const_spec((D_FF, D_MODEL))]
    outs = pl.pallas_call(
        functools.partial(_ffn_kernel, sub=sub, n_cast=len(cast),
                          stage_weights=stage_weights, regroup=regroup),
        out_shape=[jax.ShapeDtypeStruct((n, D_MODEL), jnp.float32),
                   jax.ShapeDtypeStruct(sample_out_shape, jnp.float32)] + cast_shapes,
        grid=(nt + 1,),
        in_specs=[pl.BlockSpec((tm, D_MODEL), blk), _const_spec(xs.shape)] + weight_specs
        + [_const_spec((1, D_MODEL)), _const_spec((1, D_MODEL))] + cast_in,
        out_specs=[pl.BlockSpec((tm, D_MODEL), blk),
                   pl.BlockSpec(sample_out_shape, lambda i: sample_zeros)] + cast_out,
        scratch_shapes=scratch,
        compiler_params=pltpu.CompilerParams(
            dimension_semantics=("arbitrary",), vmem_limit_bytes=VMEM_LIMIT_BYTES),
        name="ffn",
    )(xp, xs, wg, wu, wd, g, b, *cast)
    return outs[0], outs[1], outs[2:]


def _proj(xb, w_group_ref, seg, width=POOL_OUT_GROUP):
    return jnp.dot(xb, w_group_ref[:, seg:seg + width], preferred_element_type=jnp.float32)


def _shift_rows(ext, k, halo):
    return pltpu.roll(ext, k, 0)[halo:]


def _mixer_prompt_kernel(x_ref, w_in_ref, w_p_ref, conv_w_ref, pool_w_ref, pool_scale_ref,
                         w_out_ref, g_ref, b_ref, *rest, sub, n_cast):
    cast_src = rest[:n_cast]
    o_ref, conv_out_ref, pool_out_ref, *cast_dst = rest[n_cast:2 * n_cast + 3]
    u_carry, p_carry, m_buf = rest[2 * n_cast + 3:]
    t = pl.program_id(1)
    tm = x_ref.shape[1]

    @pl.when(t == 0)
    def _():
        u_carry[...] = jnp.zeros(u_carry.shape, jnp.float32)
        p_carry[...] = jnp.zeros(p_carry.shape, jnp.float32)

    u_prev = u_carry[...]
    p_prev = p_carry[...]

    def project_out_and_norm(rows):
        out = jnp.dot(m_buf[rows, :], w_out_ref[...], preferred_element_type=jnp.float32)
        o_ref[0, rows, :] = _layer_norm(ALPHA * x_ref[0, rows, :] + out, g_ref[...], b_ref[...])

    pending_rows = None
    for s in range(tm // sub):
        rows = slice(s * sub, (s + 1) * sub)
        x = x_ref[0, rows, :]
        xb = x.astype(jnp.bfloat16)

        p = _proj(xb, w_p_ref, 0, D_POOL)
        p_ext = jnp.concatenate([p_prev, p], axis=0)
        pos = t * tm + s * sub + lax.broadcasted_iota(jnp.int32, (sub, POOL_GROUP), 0)

        u_tails = []
        for j in range(N_POOL_GROUPS):
            csl = slice(j * POOL_OUT_GROUP, (j + 1) * POOL_OUT_GROUP)
            w_group = w_in_ref.at[j]
            c_g = _proj(xb, w_group, SEG_C)
            h = _proj(xb, w_group, SEG_H)
            u = c_g * h
            u_ext = jnp.concatenate([u_prev[:, csl], u], axis=0)
            u_tails.append(u[sub - CONV_HALO:, :])
            cw = conv_w_ref[:, csl]
            conv = (cw[0:1] * _shift_rows(u_ext, 2, CONV_HALO)
                    + cw[1:2] * _shift_rows(u_ext, 1, CONV_HALO)
                    + cw[2:3] * u)
            b_g = _proj(xb, w_group, SEG_B)
            y_a = b_g * conv

            psl = slice(j * POOL_GROUP, (j + 1) * POOL_GROUP)
            w = POOL_WINDOWS[j]
            acc = p_ext[:, psl]
            span = 1
            while span < w:
                acc = acc + pltpu.roll(acc, span, 0)
                span *= 2
            inv_cnt = 1.0 / jnp.minimum(pos + 1, w).astype(jnp.float32)
            q = acc[POOL_HALO:] * inv_cnt - p[:, psl]
            y_p = jnp.dot(q.astype(jnp.bfloat16), pool_w_ref[j],
                          preferred_element_type=jnp.float32) * pool_scale_ref[:, csl]

            g_a = _proj(xb, w_group, SEG_GA)
            g_p = _proj(xb, w_group, SEG_GP)
            m = jax.nn.sigmoid(g_a) * y_a + jax.nn.sigmoid(g_p) * y_p
            m_buf[rows, csl] = m.astype(jnp.bfloat16)

            if j == 0 and pending_rows is not None:
                project_out_and_norm(pending_rows)

        pending_rows = rows
        u_prev = jnp.concatenate(u_tails, axis=1)
        p_prev = p_ext[sub:, :]

    project_out_and_norm(pending_rows)
    _run_casts(cast_src, cast_dst)

    u_carry[...] = u_prev
    p_carry[...] = p_prev
    conv_out_ref[0] = u_carry[CONV_HALO - (CONV_WIDTH - 1):CONV_HALO, :]
    pool_out_ref[0] = p_carry[POOL_HALO - POOL_STATE:POOL_HALO, :]


def _mixer_prompt(x, w_in, w_p, conv_w, pool_w, pool_scale, w_out, g, b, tm, cast=()):
    bsz, seq, _ = x.shape
    tiles = seq // tm
    cast_in, cast_out, cast_shapes = _cast_specs(cast, bsz * tiles, lambda s, t: s * tiles + t)
    kern = functools.partial(_mixer_prompt_kernel, sub=min(tm, SUB_ROWS), n_cast=len(cast))
    outs = pl.pallas_call(
        kern,
        out_shape=[
            jax.ShapeDtypeStruct((bsz, seq, D_MODEL), jnp.float32),
            jax.ShapeDtypeStruct((bsz, CONV_WIDTH - 1, D_CONV), jnp.float32),
            jax.ShapeDtypeStruct((bsz, POOL_STATE, D_POOL), jnp.float32),
        ] + cast_shapes,
        grid=(bsz, tiles),
        in_specs=[
            pl.BlockSpec((1, tm, D_MODEL), lambda s, t: (s, t, 0)),
            _const_spec((N_POOL_GROUPS, D_MODEL, GROUP_COLS)),
            _const_spec((D_MODEL, D_POOL)),
            _const_spec((CONV_WIDTH, D_CONV)),
            _const_spec((N_POOL_GROUPS, POOL_GROUP, POOL_OUT_GROUP)),
            _const_spec((1, D_MODEL)),
            _const_spec((D_MODEL, D_MODEL)),
            _const_spec((1, D_MODEL)),
            _const_spec((1, D_MODEL)),
        ] + cast_in,
        out_specs=[
            pl.BlockSpec((1, tm, D_MODEL), lambda s, t: (s, t, 0)),
            pl.BlockSpec((1, CONV_WIDTH - 1, D_CONV), lambda s, t: (s, 0, 0)),
            pl.BlockSpec((1, POOL_STATE, D_POOL), lambda s, t: (s, 0, 0)),
        ] + cast_out,
        scratch_shapes=[
            pltpu.VMEM((CONV_HALO, D_CONV), jnp.float32),
            pltpu.VMEM((POOL_HALO, D_POOL), jnp.float32),
            pltpu.VMEM((tm, D_MODEL), jnp.bfloat16),
        ],
        compiler_params=pltpu.CompilerParams(
            dimension_semantics=("arbitrary", "arbitrary"), vmem_limit_bytes=VMEM_LIMIT_BYTES),
        name="mixer_prompt",
    )(x, w_in, w_p, conv_w, pool_w, pool_scale, w_out, g, b, *cast)
    return outs[0], outs[1], outs[2], outs[3:]


def _mixer_sample_kernel(x_ref, cs_ref, ps_ref, w_group_ref, conv_w_ref, pool_w_ref, pool_scale_ref,
                         w_out_ref, g_ref, b_ref, o_ref, conv_out_ref, pool_out_ref, acc_ref):
    j = pl.program_id(0)
    x = x_ref[...]
    xb = x.astype(jnp.bfloat16)

    w_group = w_group_ref.at[0]

    def proj(seg, width=POOL_OUT_GROUP):
        return _proj(xb, w_group, seg, width)

    u = proj(SEG_C) * proj(SEG_H)
    conv_out_ref[:, CONV_WIDTH - 2:CONV_WIDTH - 1, :] = u[:, None, :]
    cw = conv_w_ref[...]
    conv = cw[0:1] * cs_ref[:, 0, :] + cw[1:2] * cs_ref[:, 1, :] + cw[2:3] * u
    y_a = proj(SEG_B) * conv

    p = proj(SEG_P, POOL_GROUP)
    wsum = p
    inv_w = jnp.float32(1.0)
    hi = POOL_STATE
    for gi, w in enumerate(POOL_WINDOWS):
        lo = POOL_STATE - (w - 1)
        seg = ps_ref[lo]
        for k in range(lo + 1, hi):
            seg = seg + ps_ref[k]
        wsum = wsum + jnp.where(j >= gi, seg, 0.0)
        inv_w = jnp.where(j == gi, jnp.float32(1.0 / w), inv_w)
        hi = lo
    q = wsum * inv_w - p
    y_p = jnp.dot(q.astype(jnp.bfloat16), pool_w_ref[0],
                  preferred_element_type=jnp.float32) * pool_scale_ref[...]

    m = jax.nn.sigmoid(proj(SEG_GA)) * y_a + jax.nn.sigmoid(proj(SEG_GP)) * y_p
    part_out = jnp.dot(m.astype(jnp.bfloat16), w_out_ref[...], preferred_element_type=jnp.float32)

    @pl.when(j == 0)
    def _():
        acc_ref[...] = part_out

    @pl.when(j > 0)
    def _():
        acc_ref[...] += part_out

    @pl.when(j == pl.num_programs(0) - 1)
    def _():
        o_ref[...] = _layer_norm(ALPHA * x + acc_ref[...], g_ref[...], b_ref[...])

    conv_out_ref[:, 0:CONV_WIDTH - 2, :] = cs_ref[:, 1:CONV_WIDTH - 1, :]
    pool_out_ref[0:POOL_STATE - 1] = ps_ref[1:POOL_STATE]
    pool_out_ref[POOL_STATE - 1] = p


def _mixer_sample(x2d, cs, ps, w_in, conv_w, pool_w, pool_scale, w_out, g, b):
    nb = x2d.shape[0]
    cg, pg = POOL_OUT_GROUP, POOL_GROUP
    return pl.pallas_call(
        _mixer_sample_kernel,
        out_shape=(
            jax.ShapeDtypeStruct((nb, D_MODEL), jnp.float32),
            jax.ShapeDtypeStruct(cs.shape, jnp.float32),
            jax.ShapeDtypeStruct(ps.shape, jnp.float32),
        ),
        grid=(N_POOL_GROUPS,),
        in_specs=[
            _const_spec(x2d.shape),
            pl.BlockSpec((nb, CONV_WIDTH - 1, cg), lambda j: (0, 0, j)),
            pl.BlockSpec((POOL_STATE, nb, pg), lambda j: (0, 0, j)),
            pl.BlockSpec((1, D_MODEL, GROUP_COLS), lambda j: (j, 0, 0)),
            pl.BlockSpec((CONV_WIDTH, cg), lambda j: (0, j)),
            pl.BlockSpec((1, pg, cg), lambda j: (j, 0, 0)),
            pl.BlockSpec((1, cg), lambda j: (0, j)),
            pl.BlockSpec((cg, D_MODEL), lambda j: (j, 0)),
            _const_spec(g.shape),
            _const_spec(b.shape),
        ],
        out_specs=(
            pl.BlockSpec((nb, D_MODEL), lambda j: (0, 0)),
            pl.BlockSpec((nb, CONV_WIDTH - 1, cg), lambda j: (0, 0, j)),
            pl.BlockSpec((POOL_STATE, nb, pg), lambda j: (0, 0, j)),
        ),
        scratch_shapes=[pltpu.VMEM((nb, D_MODEL), jnp.float32)],
        compiler_params=pltpu.CompilerParams(
            dimension_semantics=("arbitrary",), vmem_limit_bytes=VMEM_LIMIT_BYTES),
        name="mixer_sample",
    )(x2d, cs, ps, w_in, conv_w, pool_w, pool_scale, w_out, g, b)


def kernel(x_prompt, x_sample, state_conv, state_pool, ln1_g, ln1_b, ffn1_wg, ffn1_wu, ffn1_wd,
           w_in, conv_w, pool_w, pool_scale, w_out, ln2_g, ln2_b,
           ffn2_wg, ffn2_wu, ffn2_wd, ln3_g, ln3_b):
    bsz, seq, _ = x_prompt.shape
    nb = x_sample.shape[0]
    assert x_sample.shape[1] == 1 and DEPTH == 1
    assert seq % ROW_BLOCK == 0

    yp = x_prompt.reshape(bsz * seq, D_MODEL)
    l = 0
    wg1, wu1, wd1 = ffn1_wg[l], ffn1_wu[l], ffn1_wd[l]
    g1, b1 = ln1_g[l][None], ln1_b[l][None]
    g2, b2 = ln2_g[l][None], ln2_b[l][None]
    g3, b3 = ln3_g[l][None], ln3_b[l][None]
    scale = pool_scale[l][None]

    pool_w2d = pool_w[l].reshape(N_POOL_GROUPS * POOL_GROUP, POOL_OUT_GROUP)
    yp, ys, (w_in_b, w_p_b, w_out_b, pool_w_b) = _ffn(
        yp, x_sample.reshape(nb, D_MODEL), wg1, wu1, wd1, g1, b1, ROW_BLOCK,
        cast=(w_out[l], pool_w2d), w_in=w_in[l])
    pool_w_b = pool_w_b.reshape(N_POOL_GROUPS, POOL_GROUP, POOL_OUT_GROUP)
    yp, conv_p, pool_p, (wg2, wu2, wd2) = _mixer_prompt(
        yp.reshape(bsz, seq, D_MODEL), w_in_b, w_p_b, conv_w[l], pool_w_b, scale, w_out_b, g2, b2,
        ROW_BLOCK, cast=(ffn2_wg[l], ffn2_wu[l], ffn2_wd[l]))
    ps_rows = jnp.transpose(state_pool[l], (1, 0, 2))
    ys, conv_s, pool_s = _mixer_sample(ys, state_conv[l], ps_rows, w_in_b, conv_w[l],
                                       pool_w_b, scale, w_out_b, g2, b2)
    yp, ys, _ = _ffn(yp.reshape(bsz * seq, D_MODEL), ys, wg2, wu2, wd2, g3, b3, ROW_BLOCK,
                     sample_out_shape=x_sample.shape)

    return (yp.reshape(bsz, seq, D_MODEL),
            ys,
            conv_p[None],
            pool_p[None],
            conv_s[None],
            jnp.transpose(pool_s, (1, 0, 2))[None])
```

```python
import functools

import jax
import jax.numpy as jnp
from jax import lax
from jax.experimental import pallas as pl
from jax.experimental.pallas import tpu as pltpu

D_MODEL = 1024
D_CONV = D_MODEL
D_POOL = D_MODEL // 2
N_POOL_GROUPS = 4
POOL_GROUP = D_POOL // N_POOL_GROUPS
POOL_OUT_GROUP = D_MODEL // N_POOL_GROUPS
POOL_WINDOWS = (2, 4, 8, 16)
POOL_STATE = max(POOL_WINDOWS) - 1
CONV_WIDTH = 3
D_FF = 2816
DEPTH = 1
ALPHA = (2.0 * DEPTH) ** 0.25
LN_EPS = 1e-5
D_IN = 3 * D_CONV + D_POOL + 2 * D_MODEL

OFF_C = 0
OFF_B = D_CONV
OFF_H = 2 * D_CONV
OFF_P = 3 * D_CONV
OFF_GA = 3 * D_CONV + D_POOL
OFF_GP = 3 * D_CONV + D_POOL + D_MODEL

ROW_BLOCK = 1024
SUB_ROWS = 256
FF_CHUNK = 256
STAGE_CHUNKS = 16
STAGE_SLOTS = 6
POOL_HALO = 16
CONV_HALO = 8
VMEM_LIMIT_BYTES = 52 * 1024 * 1024


def _layer_norm(v, g, b):
    mu = jnp.mean(v, axis=-1, keepdims=True)
    d = v - mu
    var = jnp.mean(d * d, axis=-1, keepdims=True)
    return d * lax.rsqrt(var + LN_EPS) * g + b


def _const_spec(shape):
    zeros = (0,) * len(shape)
    return pl.BlockSpec(shape, lambda *_: zeros, pipeline_mode=pl.Buffered(1))


def _cast_specs(arrays, n_steps, step_index):
    in_specs, out_specs, out_shapes = [], [], []
    for a in arrays:
        rows, cols = a.shape
        assert rows % (n_steps * 16) == 0
        spec = pl.BlockSpec((rows // n_steps, cols), lambda *idx: (step_index(*idx), 0))
        in_specs.append(spec)
        out_specs.append(spec)
        out_shapes.append(jax.ShapeDtypeStruct(a.shape, jnp.bfloat16))
    return in_specs, out_specs, out_shapes


def _run_casts(src_refs, dst_refs):
    for src, dst in zip(src_refs, dst_refs):
        dst[...] = src[...].astype(jnp.bfloat16)


def _ffn_rows(x, wg_ref, wu_ref, wd_ref, g_ref, b_ref, h_view):
    xb = x.astype(jnp.bfloat16)
    for c in range(D_FF // FF_CHUNK):
        sl = slice(c * FF_CHUNK, (c + 1) * FF_CHUNK)
        gate = jnp.dot(xb, wg_ref[:, sl], preferred_element_type=jnp.float32)
        up = jnp.dot(xb, wu_ref[:, sl], preferred_element_type=jnp.float32)
        h_view[:, sl] = (gate * jax.nn.sigmoid(gate) * up).astype(jnp.bfloat16)
    y = jnp.dot(h_view[...], wd_ref[...], preferred_element_type=jnp.float32)
    return _layer_norm(ALPHA * x + 0.5 * y, g_ref[...], b_ref[...])


def _stage_cast(w_hbm, w_vmem, stage, sem):
    slots, chunk = stage.shape[0], stage.shape[1]
    n = w_hbm.shape[0] // chunk
    ahead = slots - 1

    def rows(k):
        return pl.ds(pl.multiple_of(k * chunk, chunk), chunk)

    def copy(k):
        return pltpu.make_async_copy(w_hbm.at[rows(k), :], stage.at[k % slots], sem.at[k % slots])

    for k in range(min(ahead, n)):
        copy(k).start()

    def body(k, carry):
        @pl.when(k + ahead < n)
        def _():
            copy(k + ahead).start()

        copy(k).wait()
        w_vmem[rows(k), :] = stage[k % slots].astype(jnp.bfloat16)
        return carry

    lax.fori_loop(0, n, body, 0)


def _ffn_kernel(xp_ref, xs_ref, wg_ref, wu_ref, wd_ref, g_ref, b_ref, *rest, sub, n_cast,
                stage_weights):
    cast_src = rest[:n_cast]
    op_ref, os_ref, *cast_dst = rest[n_cast:2 * n_cast + 2]
    h_ref, *stage_scratch = rest[2 * n_cast + 2:]
    i = pl.program_id(0)
    last = pl.num_programs(0) - 1

    if stage_weights:
        wg_v, wu_v, wd_v, stage_up, stage_down, sem = stage_scratch

        @pl.when(i == 0)
        def _():
            _stage_cast(wg_ref, wg_v, stage_up, sem)
            _stage_cast(wu_ref, wu_v, stage_up, sem)
            _stage_cast(wd_ref, wd_v, stage_down, sem)

        wg_ref, wu_ref, wd_ref = wg_v, wu_v, wd_v
    weights = (wg_ref, wu_ref, wd_ref, g_ref, b_ref)

    @pl.when(i < last)
    def _():
        for s in range(xp_ref.shape[0] // sub):
            rows = pl.ds(s * sub, sub)
            h_view = h_ref.at[pl.ds((s % 2) * sub, sub), :]
            op_ref[rows, :] = _ffn_rows(xp_ref[rows, :], *weights, h_view)
        _run_casts(cast_src, cast_dst)

    @pl.when(i == last)
    def _():
        ns = xs_ref.shape[0]
        xs = xs_ref[...] if len(xs_ref.shape) == 2 else xs_ref[:, 0, :]
        ys = _ffn_rows(xs, *weights, h_ref.at[pl.ds(0, ns), :])
        os_ref[...] = ys if len(os_ref.shape) == 2 else ys[:, None, :]


def _ffn(xp, xs, wg, wu, wd, g, b, tm, cast=(), sample_out_shape=None):
    n, ns = xp.shape[0], xs.shape[0]
    nt = n // tm
    assert ns <= tm
    sample_out_shape = sample_out_shape or (ns, D_MODEL)
    sample_zeros = (0,) * len(sample_out_shape)
    blk = lambda i: (jnp.minimum(i, nt - 1), 0)
    cast_in, cast_out, cast_shapes = _cast_specs(cast, nt, lambda i: jnp.minimum(i, nt - 1))
    stage_weights = wg.dtype == jnp.float32
    assert wu.dtype == wg.dtype and wd.dtype == wg.dtype
    sub = min(tm, SUB_ROWS)
    scratch = [pltpu.VMEM((min(tm, 2 * sub), D_FF), jnp.bfloat16)]
    if stage_weights:
        weight_specs = [pl.BlockSpec(memory_space=pl.ANY)] * 3
        scratch += [
            pltpu.VMEM((D_MODEL, D_FF), jnp.bfloat16),
            pltpu.VMEM((D_MODEL, D_FF), jnp.bfloat16),
            pltpu.VMEM((D_FF, D_MODEL), jnp.bfloat16),
            pltpu.VMEM((STAGE_SLOTS, D_MODEL // STAGE_CHUNKS, D_FF), jnp.float32),
            pltpu.VMEM((STAGE_SLOTS, D_FF // STAGE_CHUNKS, D_MODEL), jnp.float32),
            pltpu.SemaphoreType.DMA((STAGE_SLOTS,)),
        ]
    else:
        weight_specs = [_const_spec((D_MODEL, D_FF)), _const_spec((D_MODEL, D_FF)),
                        _const_spec((D_FF, D_MODEL))]
    outs = pl.pallas_call(
        functools.partial(_ffn_kernel, sub=sub, n_cast=len(cast),
                          stage_weights=stage_weights),
        out_shape=[jax.ShapeDtypeStruct((n, D_MODEL), jnp.float32),
                   jax.ShapeDtypeStruct(sample_out_shape, jnp.float32)] + cast_shapes,
        grid=(nt + 1,),
        in_specs=[pl.BlockSpec((tm, D_MODEL), blk), _const_spec(xs.shape)] + weight_specs
        + [_const_spec((1, D_MODEL)), _const_spec((1, D_MODEL))] + cast_in,
        out_specs=[pl.BlockSpec((tm, D_MODEL), blk),
                   pl.BlockSpec(sample_out_shape, lambda i: sample_zeros)] + cast_out,
        scratch_shapes=scratch,
        compiler_params=pltpu.CompilerParams(
            dimension_semantics=("arbitrary",), vmem_limit_bytes=VMEM_LIMIT_BYTES),
        name="ffn",
    )(xp, xs, wg, wu, wd, g, b, *cast)
    return outs[0], outs[1], outs[2:]


def _proj(xb, w_in_ref, off, width):
    return jnp.dot(xb, w_in_ref[:, off:off + width], preferred_element_type=jnp.float32)


def _shift_rows(ext, k, halo):
    return pltpu.roll(ext, k, 0)[halo:]


def _sample_group(j, x, xb, cs_ref, ps_ref, w_in_ref, conv_w_ref, pool_w_ref, pool_scale_ref,
                  w_out_ref, conv_out_ref, pool_out_ref):
    csl = slice(j * POOL_OUT_GROUP, (j + 1) * POOL_OUT_GROUP)
    u = (_proj(xb, w_in_ref, OFF_C + j * POOL_OUT_GROUP, POOL_OUT_GROUP)
         * _proj(xb, w_in_ref, OFF_H + j * POOL_OUT_GROUP, POOL_OUT_GROUP))
    conv_out_ref[:, CONV_WIDTH - 2:CONV_WIDTH - 1, :] = u[:, None, :]
    cw = conv_w_ref[:, csl]
    conv = cw[0:1] * cs_ref[:, 0, :] + cw[1:2] * cs_ref[:, 1, :] + cw[2:3] * u
    y_a = _proj(xb, w_in_ref, OFF_B + j * POOL_OUT_GROUP, POOL_OUT_GROUP) * conv

    w = POOL_WINDOWS[j]
    p = _proj(xb, w_in_ref, OFF_P + j * POOL_GROUP, POOL_GROUP)
    wsum = p
    for k in range(POOL_STATE - (w - 1), POOL_STATE):
        wsum = wsum + ps_ref[k]
    q = wsum * (1.0 / w) - p
    y_p = jnp.dot(q.astype(jnp.bfloat16), pool_w_ref[j],
                  preferred_element_type=jnp.float32) * pool_scale_ref[:, csl]

    g_a = _proj(xb, w_in_ref, OFF_GA + j * POOL_OUT_GROUP, POOL_OUT_GROUP)
    g_p = _proj(xb, w_in_ref, OFF_GP + j * POOL_OUT_GROUP, POOL_OUT_GROUP)
    m = jax.nn.sigmoid(g_a) * y_a + jax.nn.sigmoid(g_p) * y_p

    conv_out_ref[:, 0:CONV_WIDTH - 2, :] = cs_ref[:, 1:CONV_WIDTH - 1, :]
    pool_out_ref[0:POOL_STATE - 1] = ps_ref[1:POOL_STATE]
    pool_out_ref[POOL_STATE - 1] = p
    return jnp.dot(m.astype(jnp.bfloat16), w_out_ref[csl, :], preferred_element_type=jnp.float32)


def _mixer_kernel(x_ref, xs_ref, cs_ref, ps_ref, w_in_ref, conv_w_ref, pool_w_ref, pool_scale_ref,
                  w_out_ref, g_ref, b_ref, *rest, sub, n_cast, tiles):
    cast_src = rest[:n_cast]
    o_ref, conv_out_ref, pool_out_ref, os_ref, conv_s_ref, pool_s_ref, *cast_dst = (
        rest[n_cast:2 * n_cast + 6])
    u_carry, p_carry, m_buf, acc_ref = rest[2 * n_cast + 6:]
    i = pl.program_id(0)
    n_prompt = pl.num_programs(0) - N_POOL_GROUPS
    t = i % tiles

    @pl.when(jnp.logical_and(i < n_prompt, t == 0))
    def _():
        u_carry[...] = jnp.zeros(u_carry.shape, jnp.float32)
        p_carry[...] = jnp.zeros(p_carry.shape, jnp.float32)

    @pl.when(i < n_prompt)
    def _():
        _mixer_prompt_block(x_ref, w_in_ref, conv_w_ref, pool_w_ref, pool_scale_ref, w_out_ref,
                            g_ref, b_ref, o_ref, conv_out_ref, pool_out_ref, u_carry, p_carry,
                            m_buf, t=t, sub=sub)
        _run_casts(cast_src, cast_dst)

    for j in range(N_POOL_GROUPS):
        @pl.when(i == n_prompt + j)
        def _(j=j):
            x = xs_ref[...]
            part = _sample_group(j, x, x.astype(jnp.bfloat16), cs_ref, ps_ref, w_in_ref,
                                 conv_w_ref, pool_w_ref, pool_scale_ref, w_out_ref,
                                 conv_s_ref, pool_s_ref)
            if j == 0:
                acc_ref[...] = part
            elif j < N_POOL_GROUPS - 1:
                acc_ref[...] += part
            else:
                os_ref[...] = _layer_norm(ALPHA * x + acc_ref[...] + part, g_ref[...], b_ref[...])


def _mixer_prompt_block(x_ref, w_in_ref, conv_w_ref, pool_w_ref, pool_scale_ref, w_out_ref,
                        g_ref, b_ref, o_ref, conv_out_ref, pool_out_ref, u_carry, p_carry, m_buf,
                        *, t, sub):
    tm = x_ref.shape[1]
    u_prev = u_carry[...]
    p_prev = p_carry[...]

    def project_out_and_norm(rows):
        out = jnp.dot(m_buf[rows, :], w_out_ref[...], preferred_element_type=jnp.float32)
        o_ref[0, rows, :] = _layer_norm(ALPHA * x_ref[0, rows, :] + out, g_ref[...], b_ref[...])

    pending_rows = None
    for s in range(tm // sub):
        rows = slice(s * sub, (s + 1) * sub)
        x = x_ref[0, rows, :]
        xb = x.astype(jnp.bfloat16)

        p = _proj(xb, w_in_ref, OFF_P, D_POOL)
        p_ext = jnp.concatenate([p_prev, p], axis=0)
        pos = t * tm + s * sub + lax.broadcasted_iota(jnp.int32, (sub, POOL_GROUP), 0)

        u_tails = []
        for j in range(N_POOL_GROUPS):
            csl = slice(j * POOL_OUT_GROUP, (j + 1) * POOL_OUT_GROUP)
            c_g = _proj(xb, w_in_ref, OFF_C + j * POOL_OUT_GROUP, POOL_OUT_GROUP)
            h = _proj(xb, w_in_ref, OFF_H + j * POOL_OUT_GROUP, POOL_OUT_GROUP)
            u = c_g * h
            u_ext = jnp.concatenate([u_prev[:, csl], u], axis=0)
            u_tails.append(u[sub - CONV_HALO:, :])
            cw = conv_w_ref[:, csl]
            conv = (cw[0:1] * _shift_rows(u_ext, 2, CONV_HALO)
                    + cw[1:2] * _shift_rows(u_ext, 1, CONV_HALO)
                    + cw[2:3] * u)
            b_g = _proj(xb, w_in_ref, OFF_B + j * POOL_OUT_GROUP, POOL_OUT_GROUP)
            y_a = b_g * conv

            psl = slice(j * POOL_GROUP, (j + 1) * POOL_GROUP)
            w = POOL_WINDOWS[j]
            acc = p_ext[:, psl]
            span = 1
            while span < w:
                acc = acc + pltpu.roll(acc, span, 0)
                span *= 2
            inv_cnt = 1.0 / jnp.minimum(pos + 1, w).astype(jnp.float32)
            q = acc[POOL_HALO:] * inv_cnt - p[:, psl]
            y_p = jnp.dot(q.astype(jnp.bfloat16), pool_w_ref[j],
                          preferred_element_type=jnp.float32) * pool_scale_ref[:, csl]

            g_a = _proj(xb, w_in_ref, OFF_GA + j * POOL_OUT_GROUP, POOL_OUT_GROUP)
            g_p = _proj(xb, w_in_ref, OFF_GP + j * POOL_OUT_GROUP, POOL_OUT_GROUP)
            m = jax.nn.sigmoid(g_a) * y_a + jax.nn.sigmoid(g_p) * y_p
            m_buf[rows, csl] = m.astype(jnp.bfloat16)

            if j == 0 and pending_rows is not None:
                project_out_and_norm(pending_rows)

        pending_rows = rows
        u_prev = jnp.concatenate(u_tails, axis=1)
        p_prev = p_ext[sub:, :]

    project_out_and_norm(pending_rows)

    u_carry[...] = u_prev
    p_carry[...] = p_prev
    conv_out_ref[0] = u_carry[CONV_HALO - (CONV_WIDTH - 1):CONV_HALO, :]
    pool_out_ref[0] = p_carry[POOL_HALO - POOL_STATE:POOL_HALO, :]


def _mixer(x, xs, cs, ps, w_in, conv_w, pool_w, pool_scale, w_out, g, b, tm, cast=()):
    bsz, seq, _ = x.shape
    nb = xs.shape[0]
    tiles = seq // tm
    n_prompt = bsz * tiles
    cg, pg = POOL_OUT_GROUP, POOL_GROUP
    step = lambda i: jnp.minimum(i, n_prompt - 1)
    group = lambda i: jnp.maximum(i - n_prompt, 0)
    cast_in, cast_out, cast_shapes = _cast_specs(cast, n_prompt, step)
    row_spec = pl.BlockSpec((1, tm, D_MODEL), lambda i: (step(i) // tiles, step(i) % tiles, 0))
    cs_spec = pl.BlockSpec((nb, CONV_WIDTH - 1, cg), lambda i: (0, 0, group(i)))
    ps_spec = pl.BlockSpec((POOL_STATE, nb, pg), lambda i: (0, 0, group(i)))
    outs = pl.pallas_call(
        functools.partial(_mixer_kernel, sub=min(tm, SUB_ROWS), n_cast=len(cast), tiles=tiles),
        out_shape=[
            jax.ShapeDtypeStruct((bsz, seq, D_MODEL), jnp.float32),
            jax.ShapeDtypeStruct((bsz, CONV_WIDTH - 1, D_CONV), jnp.float32),
            jax.ShapeDtypeStruct((bsz, POOL_STATE, D_POOL), jnp.float32),
            jax.ShapeDtypeStruct((nb, D_MODEL), jnp.float32),
            jax.ShapeDtypeStruct(cs.shape, jnp.float32),
            jax.ShapeDtypeStruct(ps.shape, jnp.float32),
        ] + cast_shapes,
        grid=(n_prompt + N_POOL_GROUPS,),
        in_specs=[
            row_spec,
            _const_spec(xs.shape),
            cs_spec,
            ps_spec,
            _const_spec((D_MODEL, D_IN)),
            _const_spec((CONV_WIDTH, D_CONV)),
            _const_spec((N_POOL_GROUPS, POOL_GROUP, POOL_OUT_GROUP)),
            _const_spec((1, D_MODEL)),
            _const_spec((D_MODEL, D_MODEL)),
            _const_spec((1, D_MODEL)),
            _const_spec((1, D_MODEL)),
        ] + cast_in,
        out_specs=[
            row_spec,
            pl.BlockSpec((1, CONV_WIDTH - 1, D_CONV), lambda i: (step(i) // tiles, 0, 0)),
            pl.BlockSpec((1, POOL_STATE, D_POOL), lambda i: (step(i) // tiles, 0, 0)),
            pl.BlockSpec((nb, D_MODEL), lambda i: (0, 0)),
            cs_spec,
            ps_spec,
        ] + cast_out,
        scratch_shapes=[
            pltpu.VMEM((CONV_HALO, D_CONV), jnp.float32),
            pltpu.VMEM((POOL_HALO, D_POOL), jnp.float32),
            pltpu.VMEM((tm, D_MODEL), jnp.bfloat16),
            pltpu.VMEM((nb, D_MODEL), jnp.float32),
        ],
        compiler_params=pltpu.CompilerParams(
            dimension_semantics=("arbitrary",), vmem_limit_bytes=VMEM_LIMIT_BYTES),
        name="mixer",
    )(x, xs, cs, ps, w_in, conv_w, pool_w, pool_scale, w_out, g, b, *cast)
    return outs[:6], outs[6:]


def kernel(x_prompt, x_sample, state_conv, state_pool, ln1_g, ln1_b, ffn1_wg, ffn1_wu, ffn1_wd,
           w_in, conv_w, pool_w, pool_scale, w_out, ln2_g, ln2_b,
           ffn2_wg, ffn2_wu, ffn2_wd, ln3_g, ln3_b):
    bsz, seq, _ = x_prompt.shape
    nb = x_sample.shape[0]
    assert x_sample.shape[1] == 1 and DEPTH == 1
    assert seq % ROW_BLOCK == 0

    yp = x_prompt.reshape(bsz * seq, D_MODEL)
    l = 0
    wg1, wu1, wd1 = ffn1_wg[l], ffn1_wu[l], ffn1_wd[l]
    g1, b1 = ln1_g[l][None], ln1_b[l][None]
    g2, b2 = ln2_g[l][None], ln2_b[l][None]
    g3, b3 = ln3_g[l][None], ln3_b[l][None]
    scale = pool_scale[l][None]

    pool_w2d = pool_w[l].reshape(N_POOL_GROUPS * POOL_GROUP, POOL_OUT_GROUP)
    yp, ys, (w_in_b, w_out_b, pool_w_b) = _ffn(yp, x_sample.reshape(nb, D_MODEL), wg1, wu1, wd1,
                                               g1, b1, ROW_BLOCK, cast=(w_in[l], w_out[l], pool_w2d))
    pool_w_b = pool_w_b.reshape(N_POOL_GROUPS, POOL_GROUP, POOL_OUT_GROUP)
    ps_rows = jnp.transpose(state_pool[l], (1, 0, 2))
    (yp, conv_p, pool_p, ys, conv_s, pool_s), (wg2, wu2, wd2) = _mixer(
        yp.reshape(bsz, seq, D_MODEL), ys, state_conv[l], ps_rows, w_in_b, conv_w[l], pool_w_b,
        scale, w_out_b, g2, b2, ROW_BLOCK, cast=(ffn2_wg[l], ffn2_wu[l], ffn2_wd[l]))
    yp, ys, _ = _ffn(yp.reshape(bsz * seq, D_MODEL), ys, wg2, wu2, wd2, g3, b3, ROW_BLOCK,
                     sample_out_shape=x_sample.shape)

    return (yp.reshape(bsz, seq, D_MODEL),
            ys,
            conv_p[None],
            pool_p[None],
            conv_s[None],
            jnp.transpose(pool_s, (1, 0, 2))[None])
```
